```python
import jax, jax.numpy as jnp
from jax import lax
import numpy as np

D_MODEL = 1024
BATCH = 32
SEQ = 256
DEPTH = 2
DEC_BATCH = 4
DEC_SEQ = 2048
PAST_LEN = 512

GRID_W = 64
BLOCK_Q = 128
N_HEADS = 8
N_KV_HEADS = 2
HEAD_DIM = 64
Q_PER_KV = N_HEADS // N_KV_HEADS
ATTN_W = N_HEADS * HEAD_DIM
KV_W = N_KV_HEADS * HEAD_DIM
N_FREQ = HEAD_DIM // 4
ROPE_THETA = 10000.0
CHUNK = 128
SGU_GROUPS = 4
SGU_W = 512
SGU_GW = SGU_W // SGU_GROUPS
R_HEADS = 8
R_HEAD = 64
RW = R_HEADS * R_HEAD
LORA_W = 64
LORA_A = 64
LORA_G = 128
CONV_W = 3
DECAY_SCALE = 0.606531
GN_EPS = 64e-5
N_BRANCH = 3
N_EXPERTS = 64
TOP_K = 6
D_EXPERT = 128
D_SHARED = 256
ROUTED_SCALE = 2.5
EPS = 1e-6
IN_SIZES = (ATTN_W, KV_W, KV_W, SGU_W, SGU_W, 3 * RW, 2 * LORA_W, 2 * LORA_A, LORA_G, N_BRANCH * D_MODEL)
N_IN = ATTN_W + 2 * KV_W + 2 * SGU_W + 3 * RW + 2 * LORA_W + 2 * LORA_A + LORA_G + N_BRANCH * D_MODEL

kernel_name = 'hybrid_dit_prefix_ctx_step'


def rms_norm(x, g):
    xf = x.astype(jnp.float32)
    y = xf * lax.rsqrt(jnp.mean(xf * xf, axis=-1, keepdims=True) + EPS)
    return (y * g.astype(jnp.float32)).astype(x.dtype)


def split_cols(z, sizes):
    parts, off = [], 0
    for s in sizes:
        parts.append(z[..., off:off + s])
        off += s
    return parts


def axial_rope(n_tok):
    rows = n_tok // GRID_W
    t = jnp.arange(rows * GRID_W)
    row = (t // GRID_W).astype(jnp.float32)
    col = (t % GRID_W).astype(jnp.float32)
    inv = ROPE_THETA ** (-jnp.arange(N_FREQ, dtype=jnp.float32) / N_FREQ)
    ang = jnp.stack([row[:, None] * inv[None, :], col[:, None] * inv[None, :]], axis=1)
    return jnp.cos(ang), jnp.sin(ang)


def apply_rope(x, cos, sin):
    b, t, h, _ = x.shape
    xs = x.astype(jnp.float32).reshape(b, t, h, 2, 2, N_FREQ)
    x1, x2 = xs[..., 0, :], xs[..., 1, :]
    c, s = cos[None, :, None], sin[None, :, None]
    out = jnp.stack([x1 * c - x2 * s, x2 * c + x1 * s], axis=-2)
    return out.reshape(b, t, h, HEAD_DIM).astype(x.dtype)


def block_attention(q, k, v):
    b, t = q.shape[:2]
    nb = t // BLOCK_Q
    qb = jnp.moveaxis(q.reshape(b, nb, BLOCK_Q, N_KV_HEADS, Q_PER_KV, HEAD_DIM), 1, 0)
    scale = HEAD_DIM ** -0.5

    def one_block(qblk):
        s = jnp.einsum('bqkgd,bskd->bkgqs', qblk, k).astype(jnp.float32) * scale
        p = jax.nn.softmax(s, axis=-1).astype(v.dtype)
        return jnp.einsum('bkgqs,bskd->bqkgd', p, v)

    o = lax.map(one_block, qb)
    return jnp.moveaxis(o, 0, 1).reshape(b, t, ATTN_W)


def chunk_sgu(u, v, g, ws, bs):
    b, t, _ = v.shape
    vc = rms_norm(v, g).reshape(b, t // CHUNK, CHUNK, SGU_GROUPS, SGU_GW)
    mixed = jnp.einsum('gij,bnjgc->bnigc', ws, vc) + bs.T[None, None, :, :, None]
    return u * mixed.reshape(b, t, SGU_W)


def centred_conv(x, w):
    t = x.shape[1]
    pad = CONV_W // 2
    xp = jnp.pad(x, ((0, 0), (pad, pad), (0, 0)))
    return sum(xp[:, j:j + t] * w[j] for j in range(CONV_W))


def wkv_scan(s0, r, w, k, v, kk, a, reverse):
    def step(s, inp):
        r_t, w_t, k_t, v_t, kk_t, a_t = inp
        sa = jnp.einsum('bhij,bhj->bhi', s, -kk_t)
        s = s * w_t[:, :, None, :] + sa[..., None] * (kk_t * a_t)[:, :, None, :] + v_t[..., None] * k_t[:, :, None, :]
        return s, jnp.einsum('bhij,bhj->bhi', s, r_t)

    xs = tuple(jnp.moveaxis(z.astype(jnp.float32), 1, 0) for z in (r, w, k, v, kk, a))
    s_fin, o = lax.scan(step, s0.astype(jnp.float32), xs, reverse=reverse)
    return s_fin, jnp.moveaxis(o, 0, 1)


def moe(h, P, l):
    b, t, d = h.shape
    hf = h.reshape(b * t, d)
    scores = jax.nn.sigmoid(jnp.dot(hf, P['router_w'][l]).astype(jnp.float32))
    _, idx = lax.top_k(scores + P['router_bias'][l].astype(jnp.float32), TOP_K)
    sel = jnp.take_along_axis(scores, idx, axis=-1)
    wts = sel / jnp.sum(sel, axis=-1, keepdims=True) * ROUTED_SCALE
    combine = jnp.sum(jax.nn.one_hot(idx, N_EXPERTS, dtype=jnp.float32) * wts[..., None], axis=1).astype(h.dtype)
    hid = jax.nn.silu(jnp.einsum('nd,edf->nef', hf, P['exp_gate'][l])) * jnp.einsum('nd,edf->nef', hf, P['exp_up'][l])
    routed = jnp.einsum('nef,efd->nd', hid * combine[..., None], P['exp_down'][l])
    shared = jnp.dot(jax.nn.silu(jnp.dot(hf, P['sh_gate'][l])) * jnp.dot(hf, P['sh_up'][l]), P['sh_down'][l])
    return (routed + shared).reshape(b, t, d)


def trunk_layer(x, cvec, P, l, ctx=None, rope=None):
    b, t, _ = x.shape
    dt = x.dtype
    mod = jnp.dot(jax.nn.silu(cvec), P['mod_w'][l]) + P['mod_b'][l]
    sh1, sc1, gt1, sh2, sc2, gt2 = jnp.split(mod[:, None, :], 6, axis=-1)
    h = rms_norm(x, P['norm1_g'][l]) * (1 + sc1) + sh1
    z = jnp.dot(h, P['w_in'][l])
    q, k, v, su, sv, rkv, xw, xa, xg, gates = split_cols(z, IN_SIZES)

    q = rms_norm(q.reshape(b, t, N_HEADS, HEAD_DIM), P['q_norm'][l])
    k = rms_norm(k.reshape(b, t, N_KV_HEADS, HEAD_DIM), P['k_norm'][l])
    v = v.reshape(b, t, N_KV_HEADS, HEAD_DIM)
    if ctx is None:
        k_all, v_all = k, v
        s_init = (jnp.zeros((b, R_HEADS, R_HEAD, R_HEAD), jnp.float32),) * 2
    else:
        cos, sin = rope
        q = apply_rope(q, cos, sin)
        k_lat = apply_rope(k, cos, sin)
        k_all = jnp.concatenate([ctx[0].astype(dt), k_lat], axis=1)
        v_all = jnp.concatenate([ctx[1].astype(dt), v], axis=1)
        s_init = (ctx[2], ctx[3])
    o_attn = block_attention(q, k_all, v_all)

    o_sgu = chunk_sgu(su, sv, P['sgu_norm_g'][l], P['sgu_ws'][l], P['sgu_bs'][l])

    def heads(a_):
        return a_.reshape(b, t, R_HEADS, R_HEAD)

    rkv = centred_conv(rkv, P['rwkv_conv'][l])
    r, kr, vr = jnp.split(rkv, 3, axis=-1)
    r_h, v_h = heads(r), heads(vr)
    kk = heads(kr * P['rwkv_k_k'][l]).astype(jnp.float32)
    kk = kk * lax.rsqrt(jnp.sum(kk * kk, axis=-1, keepdims=True) + 1e-12)
    xw = xw.reshape(b, t, 2, LORA_W)
    xa = xa.reshape(b, t, 2, LORA_A)
    outs, bonuses, finals = [], [], []
    for d in range(2):
        w_d = jnp.exp(-DECAY_SCALE * jax.nn.sigmoid(P['rwkv_w0'][l, d] + jnp.dot(jnp.tanh(xw[:, :, d]), P['rwkv_w2'][l, d])))
        a_d = jax.nn.sigmoid(P['rwkv_a0'][l, d] + jnp.dot(xa[:, :, d], P['rwkv_a2'][l, d]))
        k_d = heads(kr * (1 + (a_d - 1) * P['rwkv_k_a'][l]))
        s_fin, o_d = wkv_scan(s_init[d], r_h, heads(w_d), k_d, v_h, kk, heads(a_d), reverse=(d == 1))
        outs.append(o_d)
        bonuses.append(jnp.sum(r_h * k_d * P['rwkv_r_k'][l], axis=-1, keepdims=True) * v_h)
        finals.append(s_fin)
    of = outs[0] + outs[1]
    mu = jnp.mean(of, axis=-1, keepdims=True)
    var = jnp.mean(jnp.square(of - mu), axis=-1, keepdims=True)
    on = ((of - mu) * lax.rsqrt(var + GN_EPS)).reshape(b, t, RW).astype(dt) * P['rwkv_ln_g'][l] + P['rwkv_ln_b'][l]
    g_out = jnp.dot(jax.nn.sigmoid(xg), P['rwkv_g2'][l])
    o_rwkv = (on + (bonuses[0] + bonuses[1]).reshape(b, t, RW)) * g_out

    g_a, g_s, g_r = jnp.split(jax.nn.sigmoid(gates), N_BRANCH, axis=-1)
    merged = (g_a * jnp.dot(o_attn, P['proj_attn'][l])
              + g_s * jnp.dot(o_sgu, P['proj_sgu'][l])
              + g_r * jnp.dot(o_rwkv, P['proj_rwkv'][l]))
    x = x + gt1 * jnp.dot(merged, P['w_out'][l])

    h2 = rms_norm(x, P['norm2_g'][l]) * (1 + sc2) + sh2
    x = x + gt2 * moe(h2, P, l)
    return x, (k, v, finals[0], finals[1])


def setup_inputs(seed: int = 0) -> dict:
    key = jax.random.key(seed)
    keys = iter(jax.random.split(key, 64))

    def nrm(shape, scale):
        return jax.random.normal(next(keys), shape, jnp.float32) * scale

    L, D = DEPTH, D_MODEL
    conv_base = jnp.array([0.25, 0.5, 0.25], jnp.float32)[None, :, None]
    return {
        'x_prompt': nrm((BATCH, SEQ, D), 1.0),
        'x_sample': nrm((DEC_BATCH, DEC_SEQ, D), 1.0),
        'cache_k': nrm((DEC_BATCH, L, PAST_LEN, N_KV_HEADS, HEAD_DIM), 1.0),
        'cache_v': nrm((DEC_BATCH, L, PAST_LEN, N_KV_HEADS, HEAD_DIM), 1.0),
        'state_wkv': nrm((DEC_BATCH, L, 2, R_HEADS, R_HEAD, R_HEAD), 0.3),
        'c': nrm((DEC_BATCH, D), 1.0),
        'c_ctx': nrm((D,), 1.0),
        'mod_w': nrm((L, D, 6 * D), 0.5 * D ** -0.5),
        'mod_b': nrm((L, 6 * D), 0.02),
        'norm1_g': 1.0 + nrm((L, D), 0.05),
        'norm2_g': 1.0 + nrm((L, D), 0.05),
        'w_in': nrm((L, D, N_IN), D ** -0.5),
        'q_norm': 1.0 + nrm((L, HEAD_DIM), 0.05),
        'k_norm': 1.0 + nrm((L, HEAD_DIM), 0.05),
        'sgu_norm_g': 1.0 + nrm((L, SGU_W), 0.05),
        'sgu_ws': nrm((L, SGU_GROUPS, CHUNK, CHUNK), CHUNK ** -0.5),
        'sgu_bs': 1.0 + nrm((L, SGU_GROUPS, CHUNK), 0.1),
        'rwkv_conv': conv_base + nrm((L, CONV_W, 3 * RW), 0.05),
        'rwkv_w0': nrm((L, 2, RW), 1.0),
        'rwkv_w2': nrm((L, 2, LORA_W, RW), LORA_W ** -0.5),
        'rwkv_a0': nrm((L, 2, RW), 0.5),
        'rwkv_a2': nrm((L, 2, LORA_A, RW), LORA_A ** -0.5),
        'rwkv_g2': nrm((L, LORA_G, RW), LORA_G ** -0.5),
        'rwkv_k_k': 0.85 + nrm((L, RW), 0.05),
        'rwkv_k_a': 1.0 + nrm((L, RW), 0.05),
        'rwkv_r_k': nrm((L, R_HEADS, R_HEAD), 0.1),
        'rwkv_ln_g': 1.0 + nrm((L, RW), 0.05),
        'rwkv_ln_b': nrm((L, RW), 0.02),
        'proj_attn': nrm((L, ATTN_W, D), ATTN_W ** -0.5),
        'proj_sgu': nrm((L, SGU_W, D), SGU_W ** -0.5),
        'proj_rwkv': nrm((L, RW, D), RW ** -0.5),
        'w_out': nrm((L, D, D), D ** -0.5),
        'router_w': nrm((L, D, N_EXPERTS), D ** -0.5),
        'router_bias': nrm((L, N_EXPERTS), 0.01),
        'exp_gate': nrm((L, N_EXPERTS, D, D_EXPERT), D ** -0.5),
        'exp_up': nrm((L, N_EXPERTS, D, D_EXPERT), D ** -0.5),
        'exp_down': nrm((L, N_EXPERTS, D_EXPERT, D), D_EXPERT ** -0.5),
        'sh_gate': nrm((L, D, D_SHARED), D ** -0.5),
        'sh_up': nrm((L, D, D_SHARED), D ** -0.5),
        'sh_down': nrm((L, D_SHARED, D), D_SHARED ** -0.5),
    }


def reference(x_prompt, x_sample, cache_k, cache_v, state_wkv, c, c_ctx, mod_w, mod_b, norm1_g, norm2_g,
              w_in, q_norm, k_norm, sgu_norm_g, sgu_ws, sgu_bs, rwkv_conv, rwkv_w0, rwkv_w2, rwkv_a0,
              rwkv_a2, rwkv_g2, rwkv_k_k, rwkv_k_a, rwkv_r_k, rwkv_ln_g, rwkv_ln_b, proj_attn, proj_sgu,
              proj_rwkv, w_out, router_w, router_bias, exp_gate, exp_up, exp_down, sh_gate, sh_up, sh_down):
    P = {
        'mod_w': mod_w, 'mod_b': mod_b, 'norm1_g': norm1_g, 'norm2_g': norm2_g, 'w_in': w_in,
        'q_norm': q_norm, 'k_norm': k_norm, 'sgu_norm_g': sgu_norm_g, 'sgu_ws': sgu_ws, 'sgu_bs': sgu_bs,
        'rwkv_conv': rwkv_conv, 'rwkv_w0': rwkv_w0, 'rwkv_w2': rwkv_w2, 'rwkv_a0': rwkv_a0,
        'rwkv_a2': rwkv_a2, 'rwkv_g2': rwkv_g2, 'rwkv_k_k': rwkv_k_k, 'rwkv_k_a': rwkv_k_a,
        'rwkv_r_k': rwkv_r_k, 'rwkv_ln_g': rwkv_ln_g, 'rwkv_ln_b': rwkv_ln_b, 'proj_attn': proj_attn,
        'proj_sgu': proj_sgu, 'proj_rwkv': proj_rwkv, 'w_out': w_out, 'router_w': router_w,
        'router_bias': router_bias, 'exp_gate': exp_gate, 'exp_up': exp_up, 'exp_down': exp_down,
        'sh_gate': sh_gate, 'sh_up': sh_up, 'sh_down': sh_down,
    }

    xp = x_prompt
    c_p = jnp.broadcast_to(c_ctx, (x_prompt.shape[0], D_MODEL))
    ks, vs, ss = [], [], []
    for l in range(DEPTH):
        xp, (k_l, v_l, s_f, s_b) = trunk_layer(xp, c_p, P, l)
        ks.append(k_l)
        vs.append(v_l)
        ss.append(jnp.stack([s_f, s_b], axis=1))
    new_cache_k = jnp.stack(ks, axis=1)
    new_cache_v = jnp.stack(vs, axis=1)
    new_state_wkv = jnp.stack(ss, axis=1).astype(x_prompt.dtype)

    rope = axial_rope(x_sample.shape[1])
    xs = x_sample
    for l in range(DEPTH):
        ctx = (cache_k[:, l], cache_v[:, l], state_wkv[:, l, 0], state_wkv[:, l, 1])
        xs, _ = trunk_layer(xs, c, P, l, ctx=ctx, rope=rope)

    return (xp, xs, new_cache_k, new_cache_v, new_state_wkv)
```

```python
import functools

import jax
import jax.numpy as jnp
from jax import lax
from jax.experimental import pallas as pl
from jax.experimental.pallas import tpu as pltpu

F32 = jnp.float32
BF16 = jnp.bfloat16
HI = lax.Precision.HIGHEST

D_MODEL = 1024
DEPTH = 2
GRID_W = 64
N_HEADS = 8
N_KV_HEADS = 2
HEAD_DIM = 64
ATTN_W = N_HEADS * HEAD_DIM
KV_W = N_KV_HEADS * HEAD_DIM
N_FREQ = HEAD_DIM // 4
ROPE_THETA = 10000.0
CHUNK = 128
SGU_GROUPS = 4
SGU_W = 512
R_HEADS = 8
R_HEAD = 64
RW = R_HEADS * R_HEAD
LORA_W = 64
LORA_A = 64
LORA_G = 128
DECAY_SCALE = 0.606531
GN_EPS = 64e-5
N_EXPERTS = 64
TOP_K = 6
D_EXPERT = 128
D_SHARED = 256
ROUTED_SCALE = 2.5
EPS = 1e-6
LORA_ALL = 2 * LORA_W + 2 * LORA_A + LORA_G
N_IN = ATTN_W + 2 * KV_W + 2 * SGU_W + 3 * RW + LORA_ALL + 3 * D_MODEL

LANES = 128
WKV_C = 64
MOE_EB = 8
VMEM_LIMIT = 56 * 1024 * 1024


def _cparams(sem, vmem=VMEM_LIMIT):
    return pltpu.CompilerParams(dimension_semantics=sem, vmem_limit_bytes=vmem)


def _dot(a, b, prec=None):
    return jnp.dot(a, b, preferred_element_type=F32, precision=prec)


def _dot_nt(a, b, prec=None):
    return lax.dot_general(a, b, (((1,), (1,)), ((), ())), preferred_element_type=F32, precision=prec)


def _sigmoid(x):
    return 1.0 / (1.0 + jnp.exp(-x))


def _silu(x):
    return x / (1.0 + jnp.exp(-x))


def _head_block_diag(value):
    r = lax.broadcasted_iota(jnp.int32, (LANES, LANES), 0) // HEAD_DIM
    c = lax.broadcasted_iota(jnp.int32, (LANES, LANES), 1) // HEAD_DIM
    return jnp.where(r == c, value, 0.0).astype(F32)


def _mod_row(ref, row):
    return ref[pl.ds(row, 1), :]


def _mod_kernel(c_ref, w_ref, b_ref, o_ref):
    o_ref[0] = _dot(_silu(c_ref[...]), w_ref[0], HI) + b_ref[0]


def _modulation(cmat, mod_w, mod_b):
    n_l, d, n6 = mod_w.shape
    tn = n6 // 4
    return pl.pallas_call(
        _mod_kernel,
        grid=(n_l, n6 // tn),
        in_specs=[
            pl.BlockSpec((8, d), lambda l, j: (0, 0)),
            pl.BlockSpec((1, d, tn), lambda l, j: (l, 0, j)),
            pl.BlockSpec((1, 1, tn), lambda l, j: (l, 0, j)),
        ],
        out_specs=pl.BlockSpec((1, 8, tn), lambda l, j: (l, 0, j)),
        out_shape=jax.ShapeDtypeStruct((n_l, 8, n6), F32),
        compiler_params=_cparams(("arbitrary", "arbitrary")),
        name="modulation",
    )(cmat, mod_w, mod_b.reshape(n_l, 1, n6))


_IN_SPLIT = (ATTN_W, 2 * KV_W, SGU_W, SGU_W, 3 * RW, LORA_ALL, 3 * D_MODEL)


def _inproj_kernel(x_ref, sh_ref, sc_ref, g_ref, w_ref, *out_refs, row0, rstride, bps):
    row = row0 + (pl.program_id(0) // bps) * rstride if rstride else row0
    x = x_ref[...]
    h = x * lax.rsqrt(jnp.mean(x * x, axis=-1, keepdims=True) + EPS) * g_ref[...]
    h = (h * (1.0 + _mod_row(sc_ref, row)) + _mod_row(sh_ref, row)).astype(BF16)
    off = 0
    for ref, size in zip(out_refs, _IN_SPLIT):
        ref[...] = _dot(h, w_ref[:, off:off + size])
        off += size


def _inproj(x2, mod_l, g1, w_in_b, row0, rstride, t_seq):
    m, d = x2.shape
    tm = 256
    kern = functools.partial(_inproj_kernel, row0=row0, rstride=rstride, bps=t_seq // tm)
    return pl.pallas_call(
        kern,
        grid=(m // tm,),
        in_specs=[
            pl.BlockSpec((tm, d), lambda i: (i, 0)),
            pl.BlockSpec((8, d), lambda i: (0, 0)),
            pl.BlockSpec((8, d), lambda i: (0, 1)),
            pl.BlockSpec((1, d), lambda i: (0, 0)),
            pl.BlockSpec((d, N_IN), lambda i: (0, 0), pipeline_mode=pl.Buffered(1)),
        ],
        out_specs=[pl.BlockSpec((tm, s), lambda i: (i, 0)) for s in _IN_SPLIT],
        out_shape=[jax.ShapeDtypeStruct((m, s), F32) for s in _IN_SPLIT],
        compiler_params=_cparams(("arbitrary",)),
        name="inproj",
    )(x2, mod_l, mod_l, g1, w_in_b)


def _attn_kernel(*refs, t_seq, past, tq, latent):
    if latent:
        (q_ref, kv_ref, ck_ref, cv_ref, qg_ref, kg_ref, cq_ref, sq_ref, ckk_ref, skk_ref,
         o_ref, kdup, vdup) = refs
    else:
        q_ref, kv_ref, qg_ref, kg_ref, o_ref, kn_ref, kdup, vdup = refs
    lane = lax.broadcasted_iota(jnp.int32, (1, LANES), 1)
    lo = lane < HEAD_DIM
    first_half = ((lane // N_FREQ) % 2) == 0
    mean_bd = _head_block_diag(1.0 / HEAD_DIM)

    def head_norm(x, g):
        return x * lax.rsqrt(_dot(x * x, mean_bd, HI) + EPS) * g

    def rope(x, cos, sin_signed):
        rot = jnp.where(first_half, pltpu.roll(x, LANES - N_FREQ, 1), pltpu.roll(x, N_FREQ, 1))
        return x * cos + rot * sin_signed

    def store_dup(dst, x, r0, rows):
        sw = pltpu.roll(x, HEAD_DIM, 1)
        dst[0, pl.ds(r0, rows), :] = jnp.where(lo, x, sw).astype(BF16)
        dst[1, pl.ds(r0, rows), :] = jnp.where(lo, sw, x).astype(BF16)

    @pl.when(pl.program_id(1) == 0)
    def _():
        rb = 256

        def body(i, carry):
            r0 = pl.multiple_of(i * rb, rb)
            k = kv_ref[0, pl.ds(r0, rb), 0:KV_W]
            v = kv_ref[0, pl.ds(r0, rb), KV_W:2 * KV_W]
            kn = head_norm(k, kg_ref[...])
            if latent:
                kn = rope(kn, ckk_ref[pl.ds(r0, rb), :], skk_ref[pl.ds(r0, rb), :])
            else:
                kn_ref[0, pl.ds(r0, rb), :] = kn
            store_dup(kdup, kn, pl.multiple_of(past + r0, rb), rb)
            store_dup(vdup, v, pl.multiple_of(past + r0, rb), rb)
            return carry

        lax.fori_loop(0, t_seq // rb, body, 0)
        if latent:
            def cbody(i, carry):
                r0 = pl.multiple_of(i * rb, rb)
                store_dup(kdup, ck_ref[0, pl.ds(r0, rb), :], r0, rb)
                store_dup(vdup, cv_ref[0, pl.ds(r0, rb), :], r0, rb)
                return carry

            lax.fori_loop(0, past // rb, cbody, 0)

    for g in range(N_KV_HEADS):
        parts = []
        for j in (2 * g, 2 * g + 1):
            qn = head_norm(q_ref[0, :, LANES * j:LANES * (j + 1)], qg_ref[...])
            if latent:
                qn = rope(qn, cq_ref[...], sq_ref[...])
            qn = qn * (HEAD_DIM ** -0.5)
            parts += [jnp.where(lo, qn, 0.0).astype(BF16), jnp.where(lo, 0.0, qn).astype(BF16)]
        q4 = jnp.concatenate(parts, axis=0)
        s = _dot_nt(q4, kdup[g])
        p = jnp.exp(s - jnp.max(s, axis=-1, keepdims=True))
        den = jnp.sum(p, axis=-1, keepdims=True)
        o4 = _dot(p.astype(BF16), vdup[g]) / den
        o_ref[0, :, LANES * (2 * g):LANES * (2 * g + 1)] = jnp.where(lo, o4[0:tq], o4[tq:2 * tq])
        o_ref[0, :, LANES * (2 * g + 1):LANES * (2 * g + 2)] = jnp.where(lo, o4[2 * tq:3 * tq], o4[3 * tq:4 * tq])


def _attention(q, kv, qg, kg, ctx_kv, rope_tabs):
    b, t, _ = q.shape
    latent = ctx_kv is not None
    past = ctx_kv[0].shape[1] if latent else 0
    s_all = past + t
    tq = 128
    kern = functools.partial(_attn_kernel, t_seq=t, past=past, tq=tq, latent=latent)
    in_specs = [
        pl.BlockSpec((1, tq, ATTN_W), lambda bi, qi: (bi, qi, 0)),
        pl.BlockSpec((1, t, 2 * KV_W), lambda bi, qi: (bi, 0, 0)),
    ]
    args = [q, kv]
    if latent:
        in_specs += [pl.BlockSpec((1, past, KV_W), lambda bi, qi: (bi, 0, 0))] * 2
        args += list(ctx_kv)
    in_specs += [pl.BlockSpec((1, LANES), lambda bi, qi: (0, 0))] * 2
    args += [qg, kg]
    out_specs = [pl.BlockSpec((1, tq, ATTN_W), lambda bi, qi: (bi, qi, 0))]
    out_shape = [jax.ShapeDtypeStruct((b, t, ATTN_W), F32)]
    if latent:
        cos_t, sin_t = rope_tabs
        in_specs += [pl.BlockSpec((tq, LANES), lambda bi, qi: (qi, 0))] * 2
        in_specs += [pl.BlockSpec((t, LANES), lambda bi, qi: (0, 0))] * 2
        args += [cos_t, sin_t, cos_t, sin_t]
    else:
        out_specs.append(pl.BlockSpec((1, t, KV_W), lambda bi, qi: (bi, 0, 0)))
        out_shape.append(jax.ShapeDtypeStruct((b, t, KV_W), F32))
    res = pl.pallas_call(
        kern,
        grid=(b, t // tq),
        in_specs=in_specs,
        out_specs=out_specs,
        out_shape=out_shape,
        scratch_shapes=[pltpu.VMEM((N_KV_HEADS, s_all, LANES), BF16)] * 2,
        compiler_params=_cparams(("arbitrary", "arbitrary")),
        name="attention_lat" if latent else "attention_ctx",
    )(*args)
    return res if not latent else (res[0], None)


def _sgu_kernel(su_ref, sv_ref, g_ref, ws_ref, bs_ref, o_ref):
    v = sv_ref[...]
    vn = (v * lax.rsqrt(jnp.mean(v * v, axis=-1, keepdims=True) + EPS) * g_ref[...]).astype(BF16)
    for g in range(SGU_GROUPS):
        sl = slice(LANES * g, LANES * (g + 1))
        o_ref[:, sl] = su_ref[:, sl] * (_dot(ws_ref[g], vn[:, sl]) + bs_ref[g])


def _sgu(su, sv, g, ws_b, bs_b):
    m = su.shape[0]
    return pl.pallas_call(
        _sgu_kernel,
        grid=(m // CHUNK,),
        in_specs=[
            pl.BlockSpec((CHUNK, SGU_W), lambda i: (i, 0)),
            pl.BlockSpec((CHUNK, SGU_W), lambda i: (i, 0)),
            pl.BlockSpec((1, SGU_W), lambda i: (0, 0)),
            pl.BlockSpec((SGU_GROUPS, CHUNK, CHUNK), lambda i: (0, 0, 0)),
            pl.BlockSpec((SGU_GROUPS, CHUNK, LANES), lambda i: (0, 0, 0)),
        ],
        out_specs=pl.BlockSpec((CHUNK, SGU_W), lambda i: (i, 0)),
        out_shape=jax.ShapeDtypeStruct((m, SGU_W), F32),
        compiler_params=_cparams(("arbitrary",)),
        name="sgu",
    )(su, sv, g, ws_b, bs_b)


def _wkv_kernel(*refs, t_seq, latent):
    (r_ref, k_ref, v_ref, lora_ref, cwr_ref, cwk_ref, cwv_ref, w0_ref, a0_ref, w2_ref, a2_ref, g2_ref,
     kk_ref, ka_ref, rk_ref, lng_ref, lnb_ref) = refs[:17]
    rest = refs[17:]
    if latent:
        st0_ref, o_ref = rest[:2]
        scr = rest[2:]
        sf_ref = None
    else:
        o_ref, sf_ref = rest[:2]
        scr = rest[2:]
        st0_ref = None
    ta_s, tv_s, ov_s, rt_s, kbt_s, gc_s, arb_s, v_s, bon_s, of_s = scr
    c_len = WKV_C
    c2 = 2 * c_len
    nc = t_seq // c_len

    lane = lax.broadcasted_iota(jnp.int32, (1, LANES), 1)
    lo = lane < R_HEAD
    sum_bd = _head_block_diag(1.0)
    mean_bd = _head_block_diag(1.0 / R_HEAD)
    row_c = lax.broadcasted_iota(jnp.int32, (c_len, LANES), 0)
    ri = lax.broadcasted_iota(jnp.int32, (c2, c2), 0)
    ci = lax.broadcasted_iota(jnp.int32, (c2, c2), 1)
    same_head = (ri // c_len) == (ci // c_len)
    tt, ss = ri % c_len, ci % c_len
    eye2 = jnp.where(ri == ci, 1.0, 0.0).astype(F32)
    strict = (same_head & (ss < tt), same_head & (ss > tt))
    incl = (same_head & (ss <= tt), same_head & (ss >= tt))
    ti = lax.broadcasted_iota(jnp.int32, (c_len, c_len), 0)
    si = lax.broadcasted_iota(jnp.int32, (c_len, c_len), 1)
    cum_incl = (jnp.where(si <= ti, 1.0, 0.0).astype(F32), jnp.where(si >= ti, 1.0, 0.0).astype(F32))

    def stack_heads(x):
        return jnp.concatenate([jnp.where(lo, x, 0.0), jnp.where(lo, 0.0, x)], axis=0)

    def fold_heads(x):
        return x[0:c_len] + x[c_len:c2]

    def conv_tokens(ref, w_ref, c, t0):
        x = ref[0, pl.ds(t0, c_len), :]
        prev = ref[0, pl.ds(jnp.maximum(t0 - 1, 0), 1), :]
        nxt = ref[0, pl.ds(jnp.minimum(t0 + c_len, t_seq - 1), 1), :]
        prev = jnp.where(c > 0, prev, 0.0)
        nxt = jnp.where(c < nc - 1, nxt, 0.0)
        xm = jnp.where(row_c == 0, prev, pltpu.roll(x, 1, 0))
        xp = jnp.where(row_c == c_len - 1, nxt, pltpu.roll(x, c_len - 1, 0))
        return xm * w_ref[0:1, :] + x * w_ref[1:2, :] + xp * w_ref[2:3, :]

    def phase1(c, carry):
        t0 = pl.multiple_of(c * c_len, c_len)
        rc = conv_tokens(r_ref, cwr_ref, c, t0)
        kc = conv_tokens(k_ref, cwk_ref, c, t0)
        vc = conv_tokens(v_ref, cwv_ref, c, t0)
        kk = kc * kk_ref[...]
        kk = kk * lax.rsqrt(_dot(kk * kk, sum_bd, HI) + 1e-12)
        tw = jnp.tanh(lora_ref[0, pl.ds(t0, c_len), 0:2 * LORA_W])
        xa = lora_ref[0, pl.ds(t0, c_len), 2 * LORA_W:2 * LORA_W + 2 * LORA_A]
        v_hs = stack_heads(vc)
        v_s[c] = vc
        bonus = jnp.zeros((c_len, LANES), F32)
        for d in range(2):
            dsel = lo if d == 0 else jnp.logical_not(lo)
            lw = -DECAY_SCALE * _sigmoid(w0_ref[d:d + 1, :] + _dot(jnp.where(dsel, tw, 0.0).astype(BF16), w2_ref[...]))
            ag = _sigmoid(a0_ref[d:d + 1, :] + _dot(jnp.where(dsel, xa, 0.0).astype(BF16), a2_ref[...]))
            kd = kc * (1.0 + (ag - 1.0) * ka_ref[...])
            bonus = bonus + _dot(rc * kd * rk_ref[...], sum_bd, HI) * vc
            cs_in = _dot(cum_incl[d], lw, HI)
            tot = jnp.sum(lw, axis=0, keepdims=True)
            e_neg = jnp.exp(-cs_in)
            e_rem = jnp.exp(tot - cs_in)
            a_t = -kk * jnp.exp(cs_in - lw)
            r_t = rc * jnp.exp(cs_in)
            kb = kk * ag
            lhs = jnp.concatenate([stack_heads(a_t), stack_heads(r_t)], axis=0)
            rhs = jnp.concatenate([stack_heads(kd * e_neg), stack_heads(kb * e_neg)], axis=0)
            m4 = _dot_nt(lhs, rhs, HI)
            a_ak = jnp.where(strict[d], m4[0:c2, 0:c2], 0.0)
            l_ab = jnp.where(strict[d], m4[0:c2, c2:2 * c2], 0.0)
            a_rk = jnp.where(incl[d], m4[c2:2 * c2, 0:c2], 0.0)
            a_rb = jnp.where(incl[d], m4[c2:2 * c2, c2:2 * c2], 0.0)
            t_inv = eye2 + l_ab
            pw = l_ab
            for _ in range(5):
                pw = _dot(pw, pw, HI)
                t_inv = t_inv + _dot(t_inv, pw, HI)
            ta_s[d, c] = fold_heads(_dot(t_inv, stack_heads(a_t), HI))
            tv_s[d, c] = fold_heads(_dot(t_inv, _dot(a_ak, v_hs, HI), HI))
            ov_s[d, c] = fold_heads(_dot(a_rk, v_hs, HI))
            rt_s[d, c] = r_t
            arb_s[d, c] = a_rb
            kbt_s[d, c] = jnp.concatenate([kd * e_rem, kb * e_rem], axis=0).T
            gc_s[d, c] = jnp.broadcast_to(jnp.exp(tot), (LANES, LANES)).T
        bon_s[pl.ds(t0, c_len), :] = bonus
        return carry

    lax.fori_loop(0, nc, phase1, 0)

    bd_mask = sum_bd > 0.5

    def phase2(i, carry):
        new = []
        for d in range(2):
            c = i if d == 0 else nc - 1 - i
            st = carry[d]
            u = _dot(ta_s[d, c], st, HI) + tv_s[d, c]
            o = ov_s[d, c] + _dot(rt_s[d, c], st, HI) + fold_heads(_dot(arb_s[d, c], stack_heads(u), HI))
            of_s[d, pl.ds(pl.multiple_of(c * c_len, c_len), c_len), :] = o
            vu = jnp.concatenate([v_s[c], u], axis=0)
            new.append(jnp.where(bd_mask, gc_s[d, c] * st + _dot(kbt_s[d, c], vu, HI), 0.0))
        return tuple(new)

    if latent:
        init = (st0_ref[0, 0, 0], st0_ref[0, 1, 0])
    else:
        init = (jnp.zeros((LANES, LANES), F32),) * 2
    fin = lax.fori_loop(0, nc, phase2, init)
    if sf_ref is not None:
        sf_ref[0, 0, 0] = fin[0]
        sf_ref[0, 1, 0] = fin[1]

    def phase3(c, carry):
        t0 = pl.multiple_of(c * c_len, c_len)
        of = of_s[0, pl.ds(t0, c_len), :] + of_s[1, pl.ds(t0, c_len), :]
        xc = of - _dot(of, mean_bd, HI)
        on = xc * lax.rsqrt(_dot(xc * xc, mean_bd, HI) + GN_EPS) * lng_ref[...] + lnb_ref[...]
        xg = lora_ref[0, pl.ds(t0, c_len), 2 * LORA_W + 2 * LORA_A:LORA_ALL]
        g_out = _dot(_sigmoid(xg).astype(BF16), g2_ref[...])
        o_ref[0, pl.ds(t0, c_len), :] = (on + bon_s[pl.ds(t0, c_len), :]) * g_out
        return carry

    lax.fori_loop(0, nc, phase3, 0)


def _wkv(rkv, lora, wts, st0):
    b, t, _ = rkv.shape
    latent = st0 is not None
    nc = t // WKV_C
    npair = RW // LANES
    kern = functools.partial(_wkv_kernel, t_seq=t, latent=latent)

    def seq_spec(col0):
        return pl.BlockSpec((1, t, LANES), lambda bi, p: (bi, 0, col0 + p))

    def row_spec(rows, col0=0):
        return pl.BlockSpec((rows, LANES), lambda bi, p: (0, col0 + p))

    state_spec = pl.BlockSpec((1, 2, 1, LANES, LANES), lambda bi, p: (bi, 0, p, 0, 0))
    in_specs = [
        seq_spec(0), seq_spec(npair), seq_spec(2 * npair),
        pl.BlockSpec((1, t, LORA_ALL), lambda bi, p: (bi, 0, 0)),
        row_spec(3, 0), row_spec(3, npair), row_spec(3, 2 * npair),
        row_spec(2), row_spec(2), row_spec(LANES), row_spec(LANES), row_spec(LANES),
        row_spec(1), row_spec(1), row_spec(1), row_spec(1), row_spec(1),
    ]
    args = [rkv, rkv, rkv, lora, wts["conv"], wts["conv"], wts["conv"], wts["w0"], wts["a0"], wts["w2"], wts["a2"],
            wts["g2"], wts["k_k"], wts["k_a"], wts["r_k"], wts["ln_g"], wts["ln_b"]]
    out_specs = [seq_spec(0)]
    out_shape = [jax.ShapeDtypeStruct((b, t, RW), F32)]
    if latent:
        in_specs.append(state_spec)
        args.append(st0)
    else:
        out_specs.append(state_spec)
        out_shape.append(jax.ShapeDtypeStruct((b, 2, npair, LANES, LANES), F32))
    chunk_rows = pltpu.VMEM((2, nc, WKV_C, LANES), F32)
    chunk_sq = pltpu.VMEM((2, nc, LANES, LANES), F32)
    res = pl.pallas_call(
        kern,
        grid=(b, npair),
        in_specs=in_specs,
        out_specs=out_specs,
        out_shape=out_shape,
        scratch_shapes=[chunk_rows, chunk_rows, chunk_rows, chunk_rows, chunk_sq, chunk_sq, chunk_sq,
                        pltpu.VMEM((nc, WKV_C, LANES), F32), pltpu.VMEM((t, LANES), F32), pltpu.VMEM((2, t, LANES), F32)],
        compiler_params=_cparams(("arbitrary", "arbitrary")),
        name="wkv_lat" if latent else "wkv_ctx",
    )(*args)
    return (res[0], None) if latent else (res[0], res[1])


def _merge_kernel(x_ref, oa_ref, os_ref, or_ref, gates_ref, gt_ref, pa_ref, ps_ref, pr_ref, wo_ref, o_ref,
                  *, row0, rstride, bps):
    row = row0 + (pl.program_id(0) // bps) * rstride if rstride else row0
    d = D_MODEL
    merged = (_sigmoid(gates_ref[:, 0:d]) * _dot(oa_ref[...].astype(BF16), pa_ref[...])
              + _sigmoid(gates_ref[:, d:2 * d]) * _dot(os_ref[...].astype(BF16), ps_ref[...])
              + _sigmoid(gates_ref[:, 2 * d:3 * d]) * _dot(or_ref[...].astype(BF16), pr_ref[...]))
    o_ref[...] = x_ref[...] + _mod_row(gt_ref, row) * _dot(merged.astype(BF16), wo_ref[...])


def _merge(x2, oa, osg, orw, gates, mod_l, pa, ps, pr, wo, row0, rstride, t_seq):
    m, d = x2.shape
    tm = 256
    kern = functools.partial(_merge_kernel, row0=row0, rstride=rstride, bps=t_seq // tm)
    tok = lambda w: pl.BlockSpec((tm, w), lambda i: (i, 0))
    full = lambda a: pl.BlockSpec(a.shape, lambda i: (0, 0))
    return pl.pallas_call(
        kern,
        grid=(m // tm,),
        in_specs=[tok(d), tok(ATTN_W), tok(SGU_W), tok(RW), tok(3 * d),
                  pl.BlockSpec((8, d), lambda i: (0, 2)),
                  full(pa), full(ps), full(pr), full(wo)],
        out_specs=tok(d),
        out_shape=jax.ShapeDtypeStruct((m, d), F32),
        compiler_params=_cparams(("arbitrary",)),
        name="merge",
    )(x2, oa, osg, orw, gates, mod_l, pa, ps, pr, wo)


def _moe_kernel(x_ref, sh_ref, sc_ref, gt_ref, g2_ref, rw_ref, rb_ref, shg_ref, shu_ref, shd_ref,
                wg_ref, wu_ref, wd_ref, o_ref, h_s, comb_s, acc_s, *, row0, rstride, bps):
    row = row0 + (pl.program_id(0) // bps) * rstride if rstride else row0
    e = pl.program_id(1)
    tm = x_ref.shape[0]
    lane = lax.broadcasted_iota(jnp.int32, (tm, LANES), 1).astype(F32)

    @pl.when(e == 0)
    def _():
        x = x_ref[...]
        h = x * lax.rsqrt(jnp.mean(x * x, axis=-1, keepdims=True) + EPS) * g2_ref[...]
        h = h * (1.0 + _mod_row(sc_ref, row)) + _mod_row(sh_ref, row)
        scores = _sigmoid(_dot(h, rw_ref[...], HI))
        valid = lane < N_EXPERTS
        sel = jnp.where(valid, scores + rb_ref[...], -jnp.inf)
        picked = jnp.zeros((tm, LANES), jnp.bool_)
        for _ in range(TOP_K):
            best = jnp.max(sel, axis=-1, keepdims=True)
            first = jnp.min(jnp.where(sel == best, lane, float(LANES)), axis=-1, keepdims=True)
            hit = lane == first
            picked = jnp.logical_or(picked, hit)
            sel = jnp.where(hit, -jnp.inf, sel)
        sw = jnp.where(picked, scores, 0.0)
        comb_s[...] = sw / jnp.sum(sw, axis=-1, keepdims=True) * ROUTED_SCALE
        hb = h.astype(BF16)
        h_s[...] = hb
        acc_s[...] = _dot((_silu(_dot(hb, shg_ref[...])) * _dot(hb, shu_ref[...])).astype(BF16), shd_ref[...])

    hb = h_s[...]
    n_hid = MOE_EB * D_EXPERT
    er = lax.broadcasted_iota(jnp.int32, (LANES, n_hid), 0)
    ec = lax.broadcasted_iota(jnp.int32, (LANES, n_hid), 1)
    expand = jnp.where(er == e * MOE_EB + ec // D_EXPERT, 1.0, 0.0).astype(BF16)
    comb = comb_s[...]
    c_hi = comb.astype(BF16)
    rem = comb - c_hi.astype(F32)
    c_mid = rem.astype(BF16)
    c_lo = (rem - c_mid.astype(F32)).astype(BF16)
    cw = _dot(c_hi, expand) + _dot(c_mid, expand) + _dot(c_lo, expand)
    hid = _silu(_dot(hb, wg_ref[...])) * _dot(hb, wu_ref[...]) * cw
    acc_s[...] += _dot(hid.astype(BF16), wd_ref[...])

    @pl.when(e == pl.num_programs(1) - 1)
    def _():
        o_ref[...] = x_ref[...] + _mod_row(gt_ref, row) * acc_s[...]


def _moe(x2, mod_l, g2, rw_p, rb_p, shg, shu, shd, wg, wu, wd, row0, rstride, t_seq):
    m, d = x2.shape
    tm = min(512, t_seq)
    n_hid = MOE_EB * D_EXPERT
    kern = functools.partial(_moe_kernel, row0=row0, rstride=rstride, bps=t_seq // tm)
    full = lambda a: pl.BlockSpec(a.shape, lambda i, e: (0, 0))
    return pl.pallas_call(
        kern,
        grid=(m // tm, N_EXPERTS // MOE_EB),
        in_specs=[
            pl.BlockSpec((tm, d), lambda i, e: (i, 0)),
            pl.BlockSpec((8, d), lambda i, e: (0, 3)),
            pl.BlockSpec((8, d), lambda i, e: (0, 4)),
            pl.BlockSpec((8, d), lambda i, e: (0, 5)),
            full(g2), full(rw_p), full(rb_p), full(shg), full(shu), full(shd),
            pl.BlockSpec((d, n_hid), lambda i, e: (0, e)),
            pl.BlockSpec((d, n_hid), lambda i, e: (0, e)),
            pl.BlockSpec((n_hid, d), lambda i, e: (e, 0)),
        ],
        out_specs=pl.BlockSpec((tm, d), lambda i, e: (i, 0)),
        out_shape=jax.ShapeDtypeStruct((m, d), F32),
        scratch_shapes=[pltpu.VMEM((tm, d), BF16), pltpu.VMEM((tm, LANES), F32), pltpu.VMEM((tm, d), F32)],
        compiler_params=_cparams(("arbitrary", "arbitrary")),
        name="moe",
    )(x2, mod_l, mod_l, mod_l, g2, rw_p, rb_p, shg, shu, shd, wg, wu, wd)


def _layer(x, mod_l, w, row0, rstride, ctx, rope_tabs):
    b, t, d = x.shape
    x2 = x.reshape(b * t, d)
    q, kv, su, sv, rkv, lora, gates = _inproj(x2, mod_l, w["norm1_g"], w["w_in"], row0, rstride, t)
    if ctx is None:
        o_attn, k_new = _attention(q.reshape(b, t, -1), kv.reshape(b, t, -1), w["q_norm"], w["k_norm"], None, None)
        st0 = None
    else:
        o_attn, k_new = _attention(q.reshape(b, t, -1), kv.reshape(b, t, -1), w["q_norm"], w["k_norm"],
                                   ctx[:2], rope_tabs)
        st0 = ctx[2]
    o_sgu = _sgu(su, sv, w["sgu_norm_g"], w["sgu_ws"], w["sgu_bs"])
    o_wkv, s_fin = _wkv(rkv.reshape(b, t, -1), lora.reshape(b, t, -1), w["wkv"], st0)
    x1 = _merge(x2, o_attn.reshape(b * t, -1), o_sgu, o_wkv.reshape(b * t, -1), gates, mod_l,
                w["proj_attn"], w["proj_sgu"], w["proj_rwkv"], w["w_out"], row0, rstride, t)
    x_out = _moe(x1, mod_l, w["norm2_g"], w["router_w"], w["router_bias"], w["sh_gate"], w["sh_up"], w["sh_down"],
                 w["exp_gate"], w["exp_up"], w["exp_down"], row0, rstride, t)
    return x_out.reshape(b, t, d), (k_new, kv, s_fin)


def _rope_tables(n_tok):
    t = jnp.arange(n_tok)
    row = (t // GRID_W).astype(F32)
    col = (t % GRID_W).astype(F32)
    inv = ROPE_THETA ** (-jnp.arange(N_FREQ, dtype=F32) / N_FREQ)
    ang = jnp.stack([row[:, None] * inv[None, :], col[:, None] * inv[None, :]], axis=1)
    cos, sin = jnp.cos(ang), jnp.sin(ang)
    cos64 = jnp.stack([cos, cos], axis=2).reshape(n_tok, HEAD_DIM)
    sin64 = jnp.stack([-sin, sin], axis=2).reshape(n_tok, HEAD_DIM)
    return jnp.tile(cos64, (1, LANES // HEAD_DIM)), jnp.tile(sin64, (1, LANES // HEAD_DIM))


def _layer_weights(l, mod_w, mod_b, norm1_g, norm2_g, w_in, q_norm, k_norm, sgu_norm_g, sgu_ws, sgu_bs, rwkv_conv,
                   rwkv_w0, rwkv_w2, rwkv_a0, rwkv_a2, rwkv_g2, rwkv_k_k, rwkv_k_a, rwkv_r_k, rwkv_ln_g, rwkv_ln_b,
                   proj_attn, proj_sgu, proj_rwkv, w_out, router_w, router_bias, exp_gate, exp_up, exp_down,
                   sh_gate, sh_up, sh_down):
    d = D_MODEL
    row = lambda a: a.reshape(1, -1)
    pad_e = LANES - N_EXPERTS
    return {
        "norm1_g": row(norm1_g[l]), "norm2_g": row(norm2_g[l]),
        "w_in": w_in[l].astype(BF16),
        "q_norm": jnp.tile(row(q_norm[l]), (1, LANES // HEAD_DIM)),
        "k_norm": jnp.tile(row(k_norm[l]), (1, LANES // HEAD_DIM)),
        "sgu_norm_g": row(sgu_norm_g[l]),
        "sgu_ws": sgu_ws[l].astype(BF16),
        "sgu_bs": jnp.broadcast_to(sgu_bs[l][:, :, None], (SGU_GROUPS, CHUNK, LANES)),
        "wkv": {
            "conv": rwkv_conv[l],
            "w0": rwkv_w0[l], "a0": rwkv_a0[l],
            "w2": rwkv_w2[l].reshape(2 * LORA_W, RW).astype(BF16),
            "a2": rwkv_a2[l].reshape(2 * LORA_A, RW).astype(BF16),
            "g2": rwkv_g2[l].astype(BF16),
            "k_k": row(rwkv_k_k[l]), "k_a": row(rwkv_k_a[l]), "r_k": row(rwkv_r_k[l]),
            "ln_g": row(rwkv_ln_g[l]), "ln_b": row(rwkv_ln_b[l]),
        },
        "proj_attn": proj_attn[l].astype(BF16), "proj_sgu": proj_sgu[l].astype(BF16),
        "proj_rwkv": proj_rwkv[l].astype(BF16), "w_out": w_out[l].astype(BF16),
        "router_w": jnp.pad(router_w[l], ((0, 0), (0, pad_e))),
        "router_bias": jnp.pad(row(router_bias[l]), ((0, 0), (0, pad_e))),
        "sh_gate": sh_gate[l].astype(BF16), "sh_up": sh_up[l].astype(BF16), "sh_down": sh_down[l].astype(BF16),
        "exp_gate": jnp.transpose(exp_gate[l].astype(BF16), (1, 0, 2)).reshape(d, N_EXPERTS * D_EXPERT),
        "exp_up": jnp.transpose(exp_up[l].astype(BF16), (1, 0, 2)).reshape(d, N_EXPERTS * D_EXPERT),
        "exp_down": exp_down[l].astype(BF16).reshape(N_EXPERTS * D_EXPERT, d),
    }


def _state_to_blockdiag_t(s):
    b = s.shape[0]
    st = jnp.swapaxes(s, -1, -2).reshape(b, 2, R_HEADS // 2, 2, R_HEAD, R_HEAD)
    bd = jnp.einsum("bdpjkv,ji->bdpjkiv", st, jnp.eye(2, dtype=s.dtype))
    return bd.reshape(b, 2, R_HEADS // 2, LANES, LANES)


def _blockdiag_t_to_state(sf):
    b = sf.shape[0]
    x = sf.reshape(b, 2, R_HEADS // 2, 2, R_HEAD, 2, R_HEAD)
    diag = jnp.stack([x[:, :, :, 0, :, 0, :], x[:, :, :, 1, :, 1, :]], axis=3)
    return jnp.swapaxes(diag, -1, -2).reshape(b, 2, R_HEADS, R_HEAD, R_HEAD)


def kernel(x_prompt, x_sample, cache_k, cache_v, state_wkv, c, c_ctx, mod_w, mod_b, norm1_g, norm2_g, w_in, q_norm, k_norm, sgu_norm_g, sgu_ws, sgu_bs, rwkv_conv, rwkv_w0, rwkv_w2, rwkv_a0, rwkv_a2, rwkv_g2, rwkv_k_k, rwkv_k_a, rwkv_r_k, rwkv_ln_g, rwkv_ln_b, proj_attn, proj_sgu, proj_rwkv, w_out, router_w, router_bias, exp_gate, exp_up, exp_down, sh_gate, sh_up, sh_down):
    params = (mod_w, mod_b, norm1_g, norm2_g, w_in, q_norm, k_norm, sgu_norm_g, sgu_ws, sgu_bs, rwkv_conv,
              rwkv_w0, rwkv_w2, rwkv_a0, rwkv_a2, rwkv_g2, rwkv_k_k, rwkv_k_a, rwkv_r_k, rwkv_ln_g, rwkv_ln_b,
              proj_attn, proj_sgu, proj_rwkv, w_out, router_w, router_bias, exp_gate, exp_up, exp_down,
              sh_gate, sh_up, sh_down)
    n_b, n_t, d = x_prompt.shape
    n_db, n_dt, _ = x_sample.shape
    assert n_db + 1 <= 8
    cmat = jnp.zeros((8, d), F32).at[0].set(c_ctx).at[1:1 + n_db].set(c)
    mod = _modulation(cmat, mod_w, mod_b)
    rope_tabs = _rope_tables(n_dt)
    xp, xs = x_prompt, x_sample
    ks, vs, ss = [], [], []
    for l in range(DEPTH):
        w = _layer_weights(l, *params)
        xp, (k_new, kv, s_fin) = _layer(xp, mod[l], w, 0, 0, None, None)
        ks.append(k_new.reshape(n_b, n_t, N_KV_HEADS, HEAD_DIM))
        vs.append(kv.reshape(n_b, n_t, 2 * KV_W)[:, :, KV_W:].reshape(n_b, n_t, N_KV_HEADS, HEAD_DIM))
        ss.append(_blockdiag_t_to_state(s_fin))
        ctx = (cache_k[:, l].reshape(n_db, -1, KV_W), cache_v[:, l].reshape(n_db, -1, KV_W),
               _state_to_blockdiag_t(state_wkv[:, l]))
        xs, _ = _layer(xs, mod[l], w, 1, 1, ctx, rope_tabs)
    return (xp, xs, jnp.stack(ks, axis=1), jnp.stack(vs, axis=1), jnp.stack(ss, axis=1))
```

```python
import functools

import jax
import jax.numpy as jnp
from jax import lax
from jax.experimental import pallas as pl
from jax.experimental.pallas import tpu as pltpu

F32 = jnp.float32
BF16 = jnp.bfloat16
HI = lax.Precision.HIGHEST

D_MODEL = 1024
DEPTH = 2
GRID_W = 64
N_HEADS = 8
N_KV_HEADS = 2
HEAD_DIM = 64
ATTN_W = N_HEADS * HEAD_DIM
KV_W = N_KV_HEADS * HEAD_DIM
N_FREQ = HEAD_DIM // 4
ROPE_THETA = 10000.0
CHUNK = 128
SGU_GROUPS = 4
SGU_W = 512
R_HEADS = 8
R_HEAD = 64
RW = R_HEADS * R_HEAD
LORA_W = 64
LORA_A = 64
LORA_G = 128
DECAY_SCALE = 0.606531
GN_EPS = 64e-5
N_EXPERTS = 64
TOP_K = 6
D_EXPERT = 128
D_SHARED = 256
ROUTED_SCALE = 2.5
EPS = 1e-6
LORA_ALL = 2 * LORA_W + 2 * LORA_A + LORA_G
N_IN = ATTN_W + 2 * KV_W + 2 * SGU_W + 3 * RW + LORA_ALL + 3 * D_MODEL

LANES = 128
WKV_C = 64
MOE_EB = 8
VMEM_LIMIT = 56 * 1024 * 1024


def _cparams(sem, vmem=VMEM_LIMIT):
    return pltpu.CompilerParams(dimension_semantics=sem, vmem_limit_bytes=vmem)


def _dot(a, b, prec=None):
    return jnp.dot(a, b, preferred_element_type=F32, precision=prec)


def _dot_nt(a, b, prec=None):
    return lax.dot_general(a, b, (((1,), (1,)), ((), ())), preferred_element_type=F32, precision=prec)


def _sigmoid(x):
    return 1.0 / (1.0 + jnp.exp(-x))


def _silu(x):
    return x / (1.0 + jnp.exp(-x))


def _head_block_diag(value):
    r = lax.broadcasted_iota(jnp.int32, (LANES, LANES), 0) // HEAD_DIM
    c = lax.broadcasted_iota(jnp.int32, (LANES, LANES), 1) // HEAD_DIM
    return jnp.where(r == c, value, 0.0).astype(F32)


def _mod_row(ref, row):
    return ref[pl.ds(row, 1), :]


def _mod_kernel(c_ref, w_ref, b_ref, o_ref):
    o_ref[0] = _dot(_silu(c_ref[...]), w_ref[0], HI) + b_ref[0]


def _modulation(cmat, mod_w, mod_b):
    n_l, d, n6 = mod_w.shape
    tn = n6 // 4
    return pl.pallas_call(
        _mod_kernel,
        grid=(n_l, n6 // tn),
        in_specs=[
            pl.BlockSpec((8, d), lambda l, j: (0, 0)),
            pl.BlockSpec((1, d, tn), lambda l, j: (l, 0, j)),
            pl.BlockSpec((1, 1, tn), lambda l, j: (l, 0, j)),
        ],
        out_specs=pl.BlockSpec((1, 8, tn), lambda l, j: (l, 0, j)),
        out_shape=jax.ShapeDtypeStruct((n_l, 8, n6), F32),
        compiler_params=_cparams(("arbitrary", "arbitrary")),
        name="modulation",
    )(cmat, mod_w, mod_b.reshape(n_l, 1, n6))


_IN_SPLIT = (ATTN_W, 2 * KV_W, SGU_W, SGU_W, 3 * RW, LORA_ALL, 3 * D_MODEL)


def _inproj_kernel(x_ref, sh_ref, sc_ref, g_ref, w_ref, *out_refs, row0, rstride, bps):
    row = row0 + (pl.program_id(0) // bps) * rstride if rstride else row0
    x = x_ref[...]
    h = x * lax.rsqrt(jnp.mean(x * x, axis=-1, keepdims=True) + EPS) * g_ref[...]
    h = (h * (1.0 + _mod_row(sc_ref, row)) + _mod_row(sh_ref, row)).astype(BF16)
    off = 0
    for ref, size in zip(out_refs, _IN_SPLIT):
        ref[...] = _dot(h, w_ref[:, off:off + size])
        off += size


def _inproj(x2, mod_l, g1, w_in_b, row0, rstride, t_seq):
    m, d = x2.shape
    tm = 256
    kern = functools.partial(_inproj_kernel, row0=row0, rstride=rstride, bps=t_seq // tm)
    return pl.pallas_call(
        kern,
        grid=(m // tm,),
        in_specs=[
            pl.BlockSpec((tm, d), lambda i: (i, 0)),
            pl.BlockSpec((8, d), lambda i: (0, 0)),
            pl.BlockSpec((8, d), lambda i: (0, 1)),
            pl.BlockSpec((1, d), lambda i: (0, 0)),
            pl.BlockSpec((d, N_IN), lambda i: (0, 0), pipeline_mode=pl.Buffered(1)),
        ],
        out_specs=[pl.BlockSpec((tm, s), lambda i: (i, 0)) for s in _IN_SPLIT],
        out_shape=[jax.ShapeDtypeStruct((m, s), F32) for s in _IN_SPLIT],
        compiler_params=_cparams(("arbitrary",)),
        name="inproj",
    )(x2, mod_l, mod_l, g1, w_in_b)


def _attn_kernel(*refs, t_seq, past, tq, latent):
    if latent:
        (q_ref, kv_ref, ck_ref, cv_ref, qg_ref, kg_ref, cq_ref, sq_ref, ckk_ref, skk_ref,
         o_ref, kdup, vdup) = refs
    else:
        q_ref, kv_ref, qg_ref, kg_ref, o_ref, kn_ref, kdup, vdup = refs
    lane = lax.broadcasted_iota(jnp.int32, (1, LANES), 1)
    lo = lane < HEAD_DIM
    first_half = ((lane // N_FREQ) % 2) == 0
    mean_bd = _head_block_diag(1.0 / HEAD_DIM)

    def head_norm(x, g):
        return x * lax.rsqrt(_dot(x * x, mean_bd, HI) + EPS) * g

    def rope(x, cos, sin_signed):
        rot = jnp.where(first_half, pltpu.roll(x, LANES - N_FREQ, 1), pltpu.roll(x, N_FREQ, 1))
        return x * cos + rot * sin_signed

    def store_dup(dst, x, r0, rows):
        sw = pltpu.roll(x, HEAD_DIM, 1)
        dst[0, pl.ds(r0, rows), :] = jnp.where(lo, x, sw).astype(BF16)
        dst[1, pl.ds(r0, rows), :] = jnp.where(lo, sw, x).astype(BF16)

    @pl.when(pl.program_id(1) == 0)
    def _():
        rb = 256

        def body(i, carry):
            r0 = pl.multiple_of(i * rb, rb)
            k = kv_ref[0, pl.ds(r0, rb), 0:KV_W]
            v = kv_ref[0, pl.ds(r0, rb), KV_W:2 * KV_W]
            kn = head_norm(k, kg_ref[...])
            if latent:
                kn = rope(kn, ckk_ref[pl.ds(r0, rb), :], skk_ref[pl.ds(r0, rb), :])
            else:
                kn_ref[0, pl.ds(r0, rb), :] = kn
            store_dup(kdup, kn, pl.multiple_of(past + r0, rb), rb)
            store_dup(vdup, v, pl.multiple_of(past + r0, rb), rb)
            return carry

        lax.fori_loop(0, t_seq // rb, body, 0)
        if latent:
            def cbody(i, carry):
                r0 = pl.multiple_of(i * rb, rb)
                store_dup(kdup, ck_ref[0, pl.ds(r0, rb), :], r0, rb)
                store_dup(vdup, cv_ref[0, pl.ds(r0, rb), :], r0, rb)
                return carry

            lax.fori_loop(0, past // rb, cbody, 0)

    for g in range(N_KV_HEADS):
        parts = []
        for j in (2 * g, 2 * g + 1):
            qn = head_norm(q_ref[0, :, LANES * j:LANES * (j + 1)], qg_ref[...])
            if latent:
                qn = rope(qn, cq_ref[...], sq_ref[...])
            qn = qn * (HEAD_DIM ** -0.5)
            parts += [jnp.where(lo, qn, 0.0).astype(BF16), jnp.where(lo, 0.0, qn).astype(BF16)]
        q4 = jnp.concatenate(parts, axis=0)
        s = _dot_nt(q4, kdup[g])
        p = jnp.exp(s - jnp.max(s, axis=-1, keepdims=True))
        den = jnp.sum(p, axis=-1, keepdims=True)
        o4 = _dot(p.astype(BF16), vdup[g]) / den
        o_ref[0, :, LANES * (2 * g):LANES * (2 * g + 1)] = jnp.where(lo, o4[0:tq], o4[tq:2 * tq])
        o_ref[0, :, LANES * (2 * g + 1):LANES * (2 * g + 2)] = jnp.where(lo, o4[2 * tq:3 * tq], o4[3 * tq:4 * tq])


def _attention(q, kv, qg, kg, ctx_kv, rope_tabs):
    b, t, _ = q.shape
    latent = ctx_kv is not None
    past = ctx_kv[0].shape[1] if latent else 0
    s_all = past + t
    tq = 128
    kern = functools.partial(_attn_kernel, t_seq=t, past=past, tq=tq, latent=latent)
    in_specs = [
        pl.BlockSpec((1, tq, ATTN_W), lambda bi, qi: (bi, qi, 0)),
        pl.BlockSpec((1, t, 2 * KV_W), lambda bi, qi: (bi, 0, 0)),
    ]
    args = [q, kv]
    if latent:
        in_specs += [pl.BlockSpec((1, past, KV_W), lambda bi, qi: (bi, 0, 0))] * 2
        args += list(ctx_kv)
    in_specs += [pl.BlockSpec((1, LANES), lambda bi, qi: (0, 0))] * 2
    args += [qg, kg]
    out_specs = [pl.BlockSpec((1, tq, ATTN_W), lambda bi, qi: (bi, qi, 0))]
    out_shape = [jax.ShapeDtypeStruct((b, t, ATTN_W), F32)]
    if latent:
        cos_t, sin_t = rope_tabs
        in_specs += [pl.BlockSpec((tq, LANES), lambda bi, qi: (qi, 0))] * 2
        in_specs += [pl.BlockSpec((t, LANES), lambda bi, qi: (0, 0))] * 2
        args += [cos_t, sin_t, cos_t, sin_t]
    else:
        out_specs.append(pl.BlockSpec((1, t, KV_W), lambda bi, qi: (bi, 0, 0)))
        out_shape.append(jax.ShapeDtypeStruct((b, t, KV_W), F32))
    res = pl.pallas_call(
        kern,
        grid=(b, t // tq),
        in_specs=in_specs,
        out_specs=out_specs,
        out_shape=out_shape,
        scratch_shapes=[pltpu.VMEM((N_KV_HEADS, s_all, LANES), BF16)] * 2,
        compiler_params=_cparams(("arbitrary", "arbitrary")),
        name="attention_lat" if latent else "attention_ctx",
    )(*args)
    return res if not latent else (res[0], None)


def _sgu_kernel(su_ref, sv_ref, g_ref, ws_ref, bs_ref, o_ref):
    v = sv_ref[...]
    vn = (v * lax.rsqrt(jnp.mean(v * v, axis=-1, keepdims=True) + EPS) * g_ref[...]).astype(BF16)
    for g in range(SGU_GROUPS):
        sl = slice(LANES * g, LANES * (g + 1))
        o_ref[:, sl] = su_ref[:, sl] * (_dot(ws_ref[g], vn[:, sl]) + bs_ref[g])


def _sgu(su, sv, g, ws_b, bs_b):
    m = su.shape[0]
    return pl.pallas_call(
        _sgu_kernel,
        grid=(m // CHUNK,),
        in_specs=[
            pl.BlockSpec((CHUNK, SGU_W), lambda i: (i, 0)),
            pl.BlockSpec((CHUNK, SGU_W), lambda i: (i, 0)),
            pl.BlockSpec((1, SGU_W), lambda i: (0, 0)),
            pl.BlockSpec((SGU_GROUPS, CHUNK, CHUNK), lambda i: (0, 0, 0)),
            pl.BlockSpec((SGU_GROUPS, CHUNK, LANES), lambda i: (0, 0, 0)),
        ],
        out_specs=pl.BlockSpec((CHUNK, SGU_W), lambda i: (i, 0)),
        out_shape=jax.ShapeDtypeStruct((m, SGU_W), F32),
        compiler_params=_cparams(("arbitrary",)),
        name="sgu",
    )(su, sv, g, ws_b, bs_b)


def _wkv_kernel(*refs, t_seq, latent):
    (r_ref, k_ref, v_ref, lora_ref, cwr_ref, cwk_ref, cwv_ref, w0_ref, a0_ref, w2_ref, a2_ref, g2_ref,
     kk_ref, ka_ref, rk_ref, lng_ref, lnb_ref) = refs[:17]
    rest = refs[17:]
    if latent:
        st0_ref, o_ref = rest[:2]
        scr = rest[2:]
        sf_ref = None
    else:
        o_ref, sf_ref = rest[:2]
        scr = rest[2:]
        st0_ref = None
    tar_s, tv_s, ov_s, arb_s, w_s, n_s, gc_s, st_s, bon_s = scr
    c_len = WKV_C
    c2 = 2 * c_len
    nc = t_seq // c_len

    lane = lax.broadcasted_iota(jnp.int32, (1, LANES), 1)
    lo = lane < R_HEAD
    sum_bd = _head_block_diag(1.0)
    mean_bd = _head_block_diag(1.0 / R_HEAD)
    row_c = lax.broadcasted_iota(jnp.int32, (c_len, LANES), 0)
    ri = lax.broadcasted_iota(jnp.int32, (c2, c2), 0)
    ci = lax.broadcasted_iota(jnp.int32, (c2, c2), 1)
    same_head = (ri // c_len) == (ci // c_len)
    tt, ss = ri % c_len, ci % c_len
    eye2 = jnp.where(ri == ci, 1.0, 0.0).astype(F32)
    strict = (same_head & (ss < tt), same_head & (ss > tt))
    incl = (same_head & (ss <= tt), same_head & (ss >= tt))
    ti = lax.broadcasted_iota(jnp.int32, (c_len, c_len), 0)
    si = lax.broadcasted_iota(jnp.int32, (c_len, c_len), 1)
    cum_incl = (jnp.where(si <= ti, 1.0, 0.0).astype(BF16), jnp.where(si >= ti, 1.0, 0.0).astype(BF16))
    sum_bd_b = sum_bd.astype(BF16)
    mean_bd_b = mean_bd.astype(BF16)

    def bdot(a, b):
        return _dot(a.astype(BF16), b.astype(BF16))

    def split2(x):
        hi = x.astype(BF16)
        return hi, (x - hi.astype(F32)).astype(BF16)

    def head_reduce(x, bd_b):
        hi, lo = split2(x)
        return _dot(hi, bd_b) + _dot(lo, bd_b)

    def stack_heads(x):
        return jnp.concatenate([jnp.where(lo, x, 0.0), jnp.where(lo, 0.0, x)], axis=0)

    def fold_heads(x):
        return x[0:c_len] + x[c_len:c2]

    def conv_tokens(ref, w_ref, c, t0):
        x = ref[0, pl.ds(t0, c_len), :]
        prev = ref[0, pl.ds(jnp.maximum(t0 - 1, 0), 1), :]
        nxt = ref[0, pl.ds(jnp.minimum(t0 + c_len, t_seq - 1), 1), :]
        prev = jnp.where(c > 0, prev, 0.0)
        nxt = jnp.where(c < nc - 1, nxt, 0.0)
        xm = jnp.where(row_c == 0, prev, pltpu.roll(x, 1, 0))
        xp = jnp.where(row_c == c_len - 1, nxt, pltpu.roll(x, c_len - 1, 0))
        return xm * w_ref[0:1, :] + x * w_ref[1:2, :] + xp * w_ref[2:3, :]

    def chunk_group(chunks):
        n = len(chunks)
        t0s = [pl.multiple_of(c * c_len, c_len) for c in chunks]
        rc = [conv_tokens(r_ref, cwr_ref, c, t0) for c, t0 in zip(chunks, t0s)]
        kc = [conv_tokens(k_ref, cwk_ref, c, t0) for c, t0 in zip(chunks, t0s)]
        vc = [conv_tokens(v_ref, cwv_ref, c, t0) for c, t0 in zip(chunks, t0s)]
        kkr = [k * kk_ref[...] for k in kc]
        kk = [x * lax.rsqrt(head_reduce(x * x, sum_bd_b) + 1e-12) for x in kkr]
        tw = [jnp.tanh(lora_ref[0, pl.ds(t0, c_len), 0:2 * LORA_W]) for t0 in t0s]
        xa = [lora_ref[0, pl.ds(t0, c_len), 2 * LORA_W:2 * LORA_W + 2 * LORA_A] for t0 in t0s]
        v_hs = [stack_heads(v).astype(BF16) for v in vc]
        prob =[(i, d) for i in range(n) for d in range(2)]
        dsel = (lo, jnp.logical_not(lo))
        lw = [-DECAY_SCALE * _sigmoid(w0_ref[d:d + 1, :] + _dot(jnp.where(dsel[d], tw[i], 0.0).astype(BF16), w2_ref[...]))
              for i, d in prob]
        ag = [_sigmoid(a0_ref[d:d + 1, :] + _dot(jnp.where(dsel[d], xa[i], 0.0).astype(BF16), a2_ref[...]))
              for i, d in prob]
        kd = [kc[i] * (1.0 + (ag[p] - 1.0) * ka_ref[...]) for p, (i, d) in enumerate(prob)]
        kb = [kk[i] * ag[p] for p, (i, d) in enumerate(prob)]
        bon = [head_reduce(rc[i] * kd[p] * rk_ref[...], sum_bd_b) * vc[i] for p, (i, d) in enumerate(prob)]
        for i in range(n):
            bon_s[pl.ds(t0s[i], c_len), :] = bon[2 * i] + bon[2 * i + 1]
        lw_sp = [split2(x) for x in lw]
        cs_in = [_dot(cum_incl[d], lw_sp[p][0]) + _dot(cum_incl[d], lw_sp[p][1]) for p, (i, d) in enumerate(prob)]
        tot = [cs_in[p][c_len - 1:c_len, :] if d == 0 else cs_in[p][0:1, :] for p, (i, d) in enumerate(prob)]
        e_neg = [jnp.exp(-x) for x in cs_in]
        e_rem = [jnp.exp(tot[p] - cs_in[p]) for p in range(len(prob))]
        a_t = [-kk[i] * jnp.exp(cs_in[p] - (lw_sp[p][0].astype(F32) + lw_sp[p][1].astype(F32)))
               for p, (i, d) in enumerate(prob)]
        r_t = [rc[i] * jnp.exp(cs_in[p]) for p, (i, d) in enumerate(prob)]
        a_hs = [stack_heads(x).astype(BF16) for x in a_t]
        lhs = [jnp.concatenate([a_hs[p], stack_heads(r_t[p]).astype(BF16)], axis=0) for p in range(len(prob))]
        rhs = [jnp.concatenate([stack_heads(kd[p] * e_neg[p]), stack_heads(kb[p] * e_neg[p])], axis=0).astype(BF16)
               for p in range(len(prob))]
        m4 = [_dot_nt(a, b) for a, b in zip(lhs, rhs)]
        a_ak = [jnp.where(strict[d], m4[p][0:c2, 0:c2], 0.0).astype(BF16) for p, (i, d) in enumerate(prob)]
        l_ab = [jnp.where(strict[d], m4[p][0:c2, c2:2 * c2], 0.0) for p, (i, d) in enumerate(prob)]
        a_rk = [jnp.where(incl[d], m4[p][c2:2 * c2, 0:c2], 0.0).astype(BF16) for p, (i, d) in enumerate(prob)]
        for p, (i, d) in enumerate(prob):
            arb_s[d, chunks[i]] = jnp.where(incl[d], m4[p][c2:2 * c2, c2:2 * c2], 0.0).astype(BF16)
        t_inv = [eye2 + x for x in l_ab]
        pw_b = [x.astype(BF16) for x in l_ab]
        pw_b = [_dot(x, x).astype(BF16) for x in pw_b]
        for _ in range(4):
            res = [_dot(x, jnp.concatenate([x, t.astype(BF16)], axis=1)) for x, t in zip(pw_b, t_inv)]
            pw_b = [r[:, 0:c2].astype(BF16) for r in res]
            t_inv = [t + r[:, c2:2 * c2] for t, r in zip(t_inv, res)]
        t_inv = [t + _dot(x, t.astype(BF16)) for x, t in zip(pw_b, t_inv)]
        t_b = [t.astype(BF16) for t in t_inv]
        av = [_dot(a_ak[p], v_hs[i]).astype(BF16) for p, (i, d) in enumerate(prob)]
        ov = [fold_heads(_dot(a_rk[p], v_hs[i])) for p, (i, d) in enumerate(prob)]
        tt = [_dot(t, jnp.concatenate([a, x], axis=1)) for t, a, x in zip(t_b, a_hs, av)]
        ta = [fold_heads(r[:, 0:LANES]) for r in tt]
        tv = [fold_heads(r[:, LANES:2 * LANES]) for r in tt]
        kbt = [jnp.concatenate([kd[p] * e_rem[p], kb[p] * e_rem[p]], axis=0).T.astype(BF16) for p in range(len(prob))]
        zero_c = jnp.zeros((c_len, LANES), BF16)
        upd_rhs = [jnp.concatenate([jnp.concatenate([vc[i].astype(BF16), zero_c], axis=1),
                                    jnp.concatenate([tv[p].astype(BF16), ta[p].astype(BF16)], axis=1)], axis=0)
                   for p, (i, d) in enumerate(prob)]
        nw = [_dot(a, b) for a, b in zip(kbt, upd_rhs)]
        for p, (i, d) in enumerate(prob):
            c = chunks[i]
            tar_s[d, c] = jnp.concatenate([ta[p], r_t[p]], axis=0).astype(BF16)
            tv_s[d, c] = tv[p]
            ov_s[d, c] = ov[p]
            n_s[d, c] = jnp.where(bd_mask, nw[p][:, 0:LANES], 0.0)
            w_s[d, c] = jnp.where(bd_mask, nw[p][:, LANES:2 * LANES], 0.0).astype(BF16)
            gc_s[d, c] = jnp.broadcast_to(jnp.exp(tot[p]), (LANES, LANES)).T

    bd_mask = sum_bd > 0.5
    unroll1 = 4

    def phase1(i, carry):
        chunk_group([i * unroll1 + j for j in range(unroll1)])
        return carry

    lax.fori_loop(0, nc // unroll1, phase1, 0)

    def phase2(i, carry):
        cs = (i, nc - 1 - i)
        st_b = [carry[d].astype(BF16) for d in range(2)]
        for d in range(2):
            st_s[d, cs[d]] = st_b[d]
        prod = [_dot(w_s[d, cs[d]], st_b[d]) for d in range(2)]
        return tuple(gc_s[d, cs[d]] * carry[d] + prod[d] + n_s[d, cs[d]] for d in range(2))

    if latent:
        init = (st0_ref[0, 0, 0], st0_ref[0, 1, 0])
    else:
        init = (jnp.zeros((LANES, LANES), F32),) * 2
    fin = lax.fori_loop(0, nc, phase2, init)
    if sf_ref is not None:
        sf_ref[0, 0, 0] = fin[0]
        sf_ref[0, 1, 0] = fin[1]

    def out_group(chunks):
        prob = [(c, d) for c in chunks for d in range(2)]
        both = [_dot(tar_s[d, c], st_s[d, c]) for c, d in prob]
        u = [both[p][0:c_len] + tv_s[d, c] for p, (c, d) in enumerate(prob)]
        ou = [_dot(arb_s[d, c], stack_heads(u[p]).astype(BF16)) for p, (c, d) in enumerate(prob)]
        o = [ov_s[d, c] + both[p][c_len:c2] + fold_heads(ou[p]) for p, (c, d) in enumerate(prob)]
        of = [o[2 * j] + o[2 * j + 1] for j in range(len(chunks))]
        xc = [x - head_reduce(x, mean_bd_b) for x in of]
        var = [head_reduce(x * x, mean_bd_b) for x in xc]
        for j, c in enumerate(chunks):
            t0 = pl.multiple_of(c * c_len, c_len)
            on = xc[j] * lax.rsqrt(var[j] + GN_EPS) * lng_ref[...] + lnb_ref[...]
            xg = lora_ref[0, pl.ds(t0, c_len), 2 * LORA_W + 2 * LORA_A:LORA_ALL]
            g_out = _dot(_sigmoid(xg).astype(BF16), g2_ref[...])
            o_ref[0, pl.ds(t0, c_len), :] = (on + bon_s[pl.ds(t0, c_len), :]) * g_out

    unroll3 = 2

    def phase3(i, carry):
        out_group([i * unroll3 + j for j in range(unroll3)])
        return carry

    lax.fori_loop(0, nc // unroll3, phase3, 0)


def _wkv(rkv, lora, wts, st0):
    b, t, _ = rkv.shape
    latent = st0 is not None
    nc = t // WKV_C
    npair = RW // LANES
    kern = functools.partial(_wkv_kernel, t_seq=t, latent=latent)

    def seq_spec(col0):
        return pl.BlockSpec((1, t, LANES), lambda bi, p: (bi, 0, col0 + p))

    def row_spec(rows, col0=0):
        return pl.BlockSpec((rows, LANES), lambda bi, p: (0, col0 + p))

    state_spec = pl.BlockSpec((1, 2, 1, LANES, LANES), lambda bi, p: (bi, 0, p, 0, 0))
    in_specs = [
        seq_spec(0), seq_spec(npair), seq_spec(2 * npair),
        pl.BlockSpec((1, t, LORA_ALL), lambda bi, p: (bi, 0, 0)),
        row_spec(3, 0), row_spec(3, npair), row_spec(3, 2 * npair),
        row_spec(2), row_spec(2), row_spec(LANES), row_spec(LANES), row_spec(LANES),
        row_spec(1), row_spec(1), row_spec(1), row_spec(1), row_spec(1),
    ]
    args = [rkv, rkv, rkv, lora, wts["conv"], wts["conv"], wts["conv"], wts["w0"], wts["a0"], wts["w2"], wts["a2"],
            wts["g2"], wts["k_k"], wts["k_a"], wts["r_k"], wts["ln_g"], wts["ln_b"]]
    out_specs = [seq_spec(0)]
    out_shape = [jax.ShapeDtypeStruct((b, t, RW), F32)]
    if latent:
        in_specs.append(state_spec)
        args.append(st0)
    else:
        out_specs.append(state_spec)
        out_shape.append(jax.ShapeDtypeStruct((b, 2, npair, LANES, LANES), F32))
    chunk_rows = pltpu.VMEM((2, nc, WKV_C, LANES), F32)
    chunk_sq = pltpu.VMEM((2, nc, LANES, LANES), F32)
    chunk_sq_b = pltpu.VMEM((2, nc, LANES, LANES), BF16)
    res = pl.pallas_call(
        kern,
        grid=(b, npair),
        in_specs=in_specs,
        out_specs=out_specs,
        out_shape=out_shape,
        scratch_shapes=[chunk_sq_b, chunk_rows, chunk_rows, chunk_sq_b, chunk_sq_b, chunk_sq, chunk_sq, chunk_sq_b,
                        pltpu.VMEM((t, LANES), F32)],
        compiler_params=_cparams(("arbitrary", "arbitrary")),
        name="wkv_lat" if latent else "wkv_ctx",
    )(*args)
    return (res[0], None) if latent else (res[0], res[1])


def _merge_kernel(x_ref, oa_ref, os_ref, or_ref, gates_ref, gt_ref, pa_ref, ps_ref, pr_ref, wo_ref, o_ref,
                  *, row0, rstride, bps):
    row = row0 + (pl.program_id(0) // bps) * rstride if rstride else row0
    d = D_MODEL
    merged = (_sigmoid(gates_ref[:, 0:d]) * _dot(oa_ref[...].astype(BF16), pa_ref[...])
              + _sigmoid(gates_ref[:, d:2 * d]) * _dot(os_ref[...].astype(BF16), ps_ref[...])
              + _sigmoid(gates_ref[:, 2 * d:3 * d]) * _dot(or_ref[...].astype(BF16), pr_ref[...]))
    o_ref[...] = x_ref[...] + _mod_row(gt_ref, row) * _dot(merged.astype(BF16), wo_ref[...])


def _merge(x2, oa, osg, orw, gates, mod_l, pa, ps, pr, wo, row0, rstride, t_seq):
    m, d = x2.shape
    tm = 256
    kern = functools.partial(_merge_kernel, row0=row0, rstride=rstride, bps=t_seq // tm)
    tok = lambda w: pl.BlockSpec((tm, w), lambda i: (i, 0))
    full = lambda a: pl.BlockSpec(a.shape, lambda i: (0, 0))
    return pl.pallas_call(
        kern,
        grid=(m // tm,),
        in_specs=[tok(d), tok(ATTN_W), tok(SGU_W), tok(RW), tok(3 * d),
                  pl.BlockSpec((8, d), lambda i: (0, 2)),
                  full(pa), full(ps), full(pr), full(wo)],
        out_specs=tok(d),
        out_shape=jax.ShapeDtypeStruct((m, d), F32),
        compiler_params=_cparams(("arbitrary",)),
        name="merge",
    )(x2, oa, osg, orw, gates, mod_l, pa, ps, pr, wo)


def _moe_kernel(x_ref, sh_ref, sc_ref, gt_ref, g2_ref, rw_ref, rb_ref, shg_ref, shu_ref, shd_ref,
                wg_ref, wu_ref, wd_ref, o_ref, h_s, comb_s, acc_s, *, row0, rstride, bps):
    row = row0 + (pl.program_id(0) // bps) * rstride if rstride else row0
    e = pl.program_id(1)
    tm = x_ref.shape[0]
    lane = lax.broadcasted_iota(jnp.int32, (tm, LANES), 1).astype(F32)

    @pl.when(e == 0)
    def _():
        x = x_ref[...]
        h = x * lax.rsqrt(jnp.mean(x * x, axis=-1, keepdims=True) + EPS) * g2_ref[...]
        h = h * (1.0 + _mod_row(sc_ref, row)) + _mod_row(sh_ref, row)
        scores = _sigmoid(_dot(h, rw_ref[...], HI))
        valid = lane < N_EXPERTS
        sel = jnp.where(valid, scores + rb_ref[...], -jnp.inf)
        picked = jnp.zeros((tm, LANES), jnp.bool_)
        for _ in range(TOP_K):
            best = jnp.max(sel, axis=-1, keepdims=True)
            first = jnp.min(jnp.where(sel == best, lane, float(LANES)), axis=-1, keepdims=True)
            hit = lane == first
            picked = jnp.logical_or(picked, hit)
            sel = jnp.where(hit, -jnp.inf, sel)
        sw = jnp.where(picked, scores, 0.0)
        comb = sw / jnp.sum(sw, axis=-1, keepdims=True) * ROUTED_SCALE
        for b in range(N_EXPERTS // MOE_EB):
            comb_s[b] = comb if b == 0 else pltpu.roll(comb, LANES - MOE_EB * b, 1)
        hb = h.astype(BF16)
        h_s[...] = hb
        acc_s[...] = _dot((_silu(_dot(hb, shg_ref[...])) * _dot(hb, shu_ref[...])).astype(BF16), shd_ref[...])

    hb = h_s[...]
    comb = comb_s[e]
    gate = _dot(hb, wg_ref[...])
    up = _dot(hb, wu_ref[...])
    hid = []
    for j in range(MOE_EB):
        sl = slice(D_EXPERT * j, D_EXPERT * (j + 1))
        cw = jnp.broadcast_to(comb[:, j:j + 1], (tm, D_EXPERT))
        hid.append((_silu(gate[:, sl]) * up[:, sl] * cw).astype(BF16))
    acc_s[...] += _dot(jnp.concatenate(hid, axis=1), wd_ref[...])

    @pl.when(e == pl.num_programs(1) - 1)
    def _():
        o_ref[...] = x_ref[...] + _mod_row(gt_ref, row) * acc_s[...]


def _moe(x2, mod_l, g2, rw_p, rb_p, shg, shu, shd, wg, wu, wd, row0, rstride, t_seq):
    m, d = x2.shape
    tm = min(512, t_seq)
    n_hid = MOE_EB * D_EXPERT
    kern = functools.partial(_moe_kernel, row0=row0, rstride=rstride, bps=t_seq // tm)
    full = lambda a: pl.BlockSpec(a.shape, lambda i, e: (0, 0))
    return pl.pallas_call(
        kern,
        grid=(m // tm, N_EXPERTS // MOE_EB),
        in_specs=[
            pl.BlockSpec((tm, d), lambda i, e: (i, 0)),
            pl.BlockSpec((8, d), lambda i, e: (0, 3)),
            pl.BlockSpec((8, d), lambda i, e: (0, 4)),
            pl.BlockSpec((8, d), lambda i, e: (0, 5)),
            full(g2), full(rw_p), full(rb_p), full(shg), full(shu), full(shd),
            pl.BlockSpec((d, n_hid), lambda i, e: (0, e)),
            pl.BlockSpec((d, n_hid), lambda i, e: (0, e)),
            pl.BlockSpec((n_hid, d), lambda i, e: (e, 0)),
        ],
        out_specs=pl.BlockSpec((tm, d), lambda i, e: (i, 0)),
        out_shape=jax.ShapeDtypeStruct((m, d), F32),
        scratch_shapes=[pltpu.VMEM((tm, d), BF16), pltpu.VMEM((N_EXPERTS // MOE_EB, tm, LANES), F32),
                        pltpu.VMEM((tm, d), F32)],
        compiler_params=_cparams(("arbitrary", "arbitrary")),
        name="moe",
    )(x2, mod_l, mod_l, mod_l, g2, rw_p, rb_p, shg, shu, shd, wg, wu, wd)


def _layer(x, mod_l, w, row0, rstride, ctx, rope_tabs):
    b, t, d = x.shape
    x2 = x.reshape(b * t, d)
    q, kv, su, sv, rkv, lora, gates = _inproj(x2, mod_l, w["norm1_g"], w["w_in"], row0, rstride, t)
    if ctx is None:
        o_attn, k_new = _attention(q.reshape(b, t, -1), kv.reshape(b, t, -1), w["q_norm"], w["k_norm"], None, None)
        st0 = None
    else:
        o_attn, k_new = _attention(q.reshape(b, t, -1), kv.reshape(b, t, -1), w["q_norm"], w["k_norm"],
                                   ctx[:2], rope_tabs)
        st0 = ctx[2]
    o_sgu = _sgu(su, sv, w["sgu_norm_g"], w["sgu_ws"], w["sgu_bs"])
    o_wkv, s_fin = _wkv(rkv.reshape(b, t, -1), lora.reshape(b, t, -1), w["wkv"], st0)
    x1 = _merge(x2, o_attn.reshape(b * t, -1), o_sgu, o_wkv.reshape(b * t, -1), gates, mod_l,
                w["proj_attn"], w["proj_sgu"], w["proj_rwkv"], w["w_out"], row0, rstride, t)
    x_out = _moe(x1, mod_l, w["norm2_g"], w["router_w"], w["router_bias"], w["sh_gate"], w["sh_up"], w["sh_down"],
                 w["exp_gate"], w["exp_up"], w["exp_down"], row0, rstride, t)
    return x_out.reshape(b, t, d), (k_new, kv, s_fin)


def _rope_tables(n_tok):
    t = jnp.arange(n_tok)
    row = (t // GRID_W).astype(F32)
    col = (t % GRID_W).astype(F32)
    inv = ROPE_THETA ** (-jnp.arange(N_FREQ, dtype=F32) / N_FREQ)
    ang = jnp.stack([row[:, None] * inv[None, :], col[:, None] * inv[None, :]], axis=1)
    cos, sin = jnp.cos(ang), jnp.sin(ang)
    cos64 = jnp.stack([cos, cos], axis=2).reshape(n_tok, HEAD_DIM)
    sin64 = jnp.stack([-sin, sin], axis=2).reshape(n_tok, HEAD_DIM)
    return jnp.tile(cos64, (1, LANES // HEAD_DIM)), jnp.tile(sin64, (1, LANES // HEAD_DIM))


def _layer_weights(l, mod_w, mod_b, norm1_g, norm2_g, w_in, q_norm, k_norm, sgu_norm_g, sgu_ws, sgu_bs, rwkv_conv,
                   rwkv_w0, rwkv_w2, rwkv_a0, rwkv_a2, rwkv_g2, rwkv_k_k, rwkv_k_a, rwkv_r_k, rwkv_ln_g, rwkv_ln_b,
                   proj_attn, proj_sgu, proj_rwkv, w_out, router_w, router_bias, exp_gate, exp_up, exp_down,
                   sh_gate, sh_up, sh_down):
    d = D_MODEL
    row = lambda a: a.reshape(1, -1)
    pad_e = LANES - N_EXPERTS
    return {
        "norm1_g": row(norm1_g[l]), "norm2_g": row(norm2_g[l]),
        "w_in": w_in[l].astype(BF16),
        "q_norm": jnp.tile(row(q_norm[l]), (1, LANES // HEAD_DIM)),
        "k_norm": jnp.tile(row(k_norm[l]), (1, LANES // HEAD_DIM)),
        "sgu_norm_g": row(sgu_norm_g[l]),
        "sgu_ws": sgu_ws[l].astype(BF16),
        "sgu_bs": jnp.broadcast_to(sgu_bs[l][:, :, None], (SGU_GROUPS, CHUNK, LANES)),
        "wkv": {
            "conv": rwkv_conv[l],
            "w0": rwkv_w0[l], "a0": rwkv_a0[l],
            "w2": rwkv_w2[l].reshape(2 * LORA_W, RW).astype(BF16),
            "a2": rwkv_a2[l].reshape(2 * LORA_A, RW).astype(BF16),
            "g2": rwkv_g2[l].astype(BF16),
            "k_k": row(rwkv_k_k[l]), "k_a": row(rwkv_k_a[l]), "r_k": row(rwkv_r_k[l]),
            "ln_g": row(rwkv_ln_g[l]), "ln_b": row(rwkv_ln_b[l]),
        },
        "proj_attn": proj_attn[l].astype(BF16), "proj_sgu": proj_sgu[l].astype(BF16),
        "proj_rwkv": proj_rwkv[l].astype(BF16), "w_out": w_out[l].astype(BF16),
        "router_w": jnp.pad(router_w[l], ((0, 0), (0, pad_e))),
        "router_bias": jnp.pad(row(router_bias[l]), ((0, 0), (0, pad_e))),
        "sh_gate": sh_gate[l].astype(BF16), "sh_up": sh_up[l].astype(BF16), "sh_down": sh_down[l].astype(BF16),
        "exp_gate": jnp.transpose(exp_gate[l].astype(BF16), (1, 0, 2)).reshape(d, N_EXPERTS * D_EXPERT),
        "exp_up": jnp.transpose(exp_up[l].astype(BF16), (1, 0, 2)).reshape(d, N_EXPERTS * D_EXPERT),
        "exp_down": exp_down[l].astype(BF16).reshape(N_EXPERTS * D_EXPERT, d),
    }


def _state_to_blockdiag_t(s):
    b = s.shape[0]
    st = jnp.swapaxes(s, -1, -2).reshape(b, 2, R_HEADS // 2, 2, R_HEAD, R_HEAD)
    bd = jnp.einsum("bdpjkv,ji->bdpjkiv", st, jnp.eye(2, dtype=s.dtype))
    return bd.reshape(b, 2, R_HEADS // 2, LANES, LANES)


def _blockdiag_t_to_state(sf):
    b = sf.shape[0]
    x = sf.reshape(b, 2, R_HEADS // 2, 2, R_HEAD, 2, R_HEAD)
    diag = jnp.stack([x[:, :, :, 0, :, 0, :], x[:, :, :, 1, :, 1, :]], axis=3)
    return jnp.swapaxes(diag, -1, -2).reshape(b, 2, R_HEADS, R_HEAD, R_HEAD)


def kernel(x_prompt, x_sample, cache_k, cache_v, state_wkv, c, c_ctx, mod_w, mod_b, norm1_g, norm2_g, w_in, q_norm, k_norm, sgu_norm_g, sgu_ws, sgu_bs, rwkv_conv, rwkv_w0, rwkv_w2, rwkv_a0, rwkv_a2, rwkv_g2, rwkv_k_k, rwkv_k_a, rwkv_r_k, rwkv_ln_g, rwkv_ln_b, proj_attn, proj_sgu, proj_rwkv, w_out, router_w, router_bias, exp_gate, exp_up, exp_down, sh_gate, sh_up, sh_down):
    params = (mod_w, mod_b, norm1_g, norm2_g, w_in, q_norm, k_norm, sgu_norm_g, sgu_ws, sgu_bs, rwkv_conv,
              rwkv_w0, rwkv_w2, rwkv_a0, rwkv_a2, rwkv_g2, rwkv_k_k, rwkv_k_a, rwkv_r_k, rwkv_ln_g, rwkv_ln_b,
              proj_attn, proj_sgu, proj_rwkv, w_out, router_w, router_bias, exp_gate, exp_up, exp_down,
              sh_gate, sh_up, sh_down)
    n_b, n_t, d = x_prompt.shape
    n_db, n_dt, _ = x_sample.shape
    assert n_db + 1 <= 8
    cmat = jnp.zeros((8, d), F32).at[0].set(c_ctx).at[1:1 + n_db].set(c)
    mod = _modulation(cmat, mod_w, mod_b)
    rope_tabs = _rope_tables(n_dt)
    xp, xs = x_prompt, x_sample
    ks, vs, ss = [], [], []
    for l in range(DEPTH):
        w = _layer_weights(l, *params)
        xp, (k_new, kv, s_fin) = _layer(xp, mod[l], w, 0, 0, None, None)
        ks.append(k_new.reshape(n_b, n_t, N_KV_HEADS, HEAD_DIM))
        vs.append(kv.reshape(n_b, n_t, 2 * KV_W)[:, :, KV_W:].reshape(n_b, n_t, N_KV_HEADS, HEAD_DIM))
        ss.append(_blockdiag_t_to_state(s_fin))
        ctx = (cache_k[:, l].reshape(n_db, -1, KV_W), cache_v[:, l].reshape(n_db, -1, KV_W),
               _state_to_blockdiag_t(state_wkv[:, l]))
        xs, _ = _layer(xs, mod[l], w, 1, 1, ctx, rope_tabs)
    return (xp, xs, jnp.stack(ks, axis=1), jnp.stack(vs, axis=1), jnp.stack(ss, axis=1))
```

```python
import functools

import jax
import jax.numpy as jnp
from jax import lax
from jax.experimental import pallas as pl
from jax.experimental.pallas import tpu as pltpu

F32 = jnp.float32
BF16 = jnp.bfloat16
HI = lax.Precision.HIGHEST

D_MODEL = 1024
DEPTH = 2
GRID_W = 64
N_HEADS = 8
N_KV_HEADS = 2
HEAD_DIM = 64
ATTN_W = N_HEADS * HEAD_DIM
KV_W = N_KV_HEADS * HEAD_DIM
N_FREQ = HEAD_DIM // 4
ROPE_THETA = 10000.0
CHUNK = 128
SGU_GROUPS = 4
SGU_W = 512
R_HEADS = 8
R_HEAD = 64
RW = R_HEADS * R_HEAD
LORA_W = 64
LORA_A = 64
LORA_G = 128
DECAY_SCALE = 0.606531
GN_EPS = 64e-5
N_EXPERTS = 64
TOP_K = 6
D_EXPERT = 128
D_SHARED = 256
ROUTED_SCALE = 2.5
EPS = 1e-6
LORA_ALL = 2 * LORA_W + 2 * LORA_A + LORA_G
N_IN = ATTN_W + 2 * KV_W + 2 * SGU_W + 3 * RW + LORA_ALL + 3 * D_MODEL

LANES = 128
WKV_C = 64
MOE_EB = 8
VMEM_LIMIT = 56 * 1024 * 1024


def _cparams(sem, vmem=VMEM_LIMIT):
    return pltpu.CompilerParams(dimension_semantics=sem, vmem_limit_bytes=vmem)


def _dot(a, b, prec=None):
    return jnp.dot(a, b, preferred_element_type=F32, precision=prec)


def _dot_nt(a, b, prec=None):
    return lax.dot_general(a, b, (((1,), (1,)), ((), ())), preferred_element_type=F32, precision=prec)


def _sigmoid(x):
    return 1.0 / (1.0 + jnp.exp(-x))


def _silu(x):
    return x / (1.0 + jnp.exp(-x))


def _head_block_diag(value):
    r = lax.broadcasted_iota(jnp.int32, (LANES, LANES), 0) // HEAD_DIM
    c = lax.broadcasted_iota(jnp.int32, (LANES, LANES), 1) // HEAD_DIM
    return jnp.where(r == c, value, 0.0).astype(F32)


def _mod_row(ref, row):
    return ref[pl.ds(row, 1), :]


def _mod_kernel(c_ref, w_ref, b_ref, o_ref):
    o_ref[0] = _dot(_silu(c_ref[...]), w_ref[0], HI) + b_ref[0]


def _modulation(cmat, mod_w, mod_b):
    n_l, d, n6 = mod_w.shape
    tn = n6 // 4
    return pl.pallas_call(
        _mod_kernel,
        grid=(n_l, n6 // tn),
        in_specs=[
            pl.BlockSpec((8, d), lambda l, j: (0, 0)),
            pl.BlockSpec((1, d, tn), lambda l, j: (l, 0, j)),
            pl.BlockSpec((1, 1, tn), lambda l, j: (l, 0, j)),
        ],
        out_specs=pl.BlockSpec((1, 8, tn), lambda l, j: (l, 0, j)),
        out_shape=jax.ShapeDtypeStruct((n_l, 8, n6), F32),
        compiler_params=_cparams(("arbitrary", "arbitrary")),
        name="modulation",
    )(cmat, mod_w, mod_b.reshape(n_l, 1, n6))


_IN_SPLIT = (ATTN_W, 2 * KV_W, SGU_W, SGU_W, 3 * RW, LORA_ALL, 3 * D_MODEL)


def _inproj_kernel(x_ref, sh_ref, sc_ref, g_ref, w_ref, *out_refs, row0, rstride, bps):
    row = row0 + (pl.program_id(0) // bps) * rstride if rstride else row0
    x = x_ref[...]
    h = x * lax.rsqrt(jnp.mean(x * x, axis=-1, keepdims=True) + EPS) * g_ref[...]
    h = (h * (1.0 + _mod_row(sc_ref, row)) + _mod_row(sh_ref, row)).astype(BF16)
    off = 0
    for ref, size in zip(out_refs, _IN_SPLIT):
        ref[...] = _dot(h, w_ref[:, off:off + size])
        off += size


def _inproj(x2, mod_l, g1, w_in_b, row0, rstride, t_seq):
    m, d = x2.shape
    tm = 256
    kern = functools.partial(_inproj_kernel, row0=row0, rstride=rstride, bps=t_seq // tm)
    return pl.pallas_call(
        kern,
        grid=(m // tm,),
        in_specs=[
            pl.BlockSpec((tm, d), lambda i: (i, 0)),
            pl.BlockSpec((8, d), lambda i: (0, 0)),
            pl.BlockSpec((8, d), lambda i: (0, 1)),
            pl.BlockSpec((1, d), lambda i: (0, 0)),
            pl.BlockSpec((d, N_IN), lambda i: (0, 0), pipeline_mode=pl.Buffered(1)),
        ],
        out_specs=[pl.BlockSpec((tm, s), lambda i: (i, 0)) for s in _IN_SPLIT],
        out_shape=[jax.ShapeDtypeStruct((m, s), F32) for s in _IN_SPLIT],
        compiler_params=_cparams(("arbitrary",)),
        name="inproj",
    )(x2, mod_l, mod_l, g1, w_in_b)


def _attn_kernel(*refs, t_seq, past, tq, latent):
    if latent:
        (q_ref, kv_ref, ck_ref, cv_ref, qg_ref, kg_ref, cq_ref, sq_ref, ckk_ref, skk_ref,
         o_ref, k_s, vt_s) = refs
    else:
        q_ref, kv_ref, qg_ref, kg_ref, o_ref, kn_ref, k_s, vt_s = refs
    lane = lax.broadcasted_iota(jnp.int32, (1, LANES), 1)
    first_half = ((lane // N_FREQ) % 2) == 0
    mean_bd = _head_block_diag(1.0 / HEAD_DIM).astype(BF16)
    q_per_kv = N_HEADS // N_KV_HEADS

    def head_norm(x, g):
        sq = x * x
        hi = sq.astype(BF16)
        lo = (sq - hi.astype(F32)).astype(BF16)
        return x * lax.rsqrt(_dot(hi, mean_bd) + _dot(lo, mean_bd) + EPS) * g

    def rope(x, cos, sin_signed):
        rot = jnp.where(first_half, pltpu.roll(x, LANES - N_FREQ, 1), pltpu.roll(x, N_FREQ, 1))
        return x * cos + rot * sin_signed

    @pl.when(pl.program_id(1) == 0)
    def _():
        rb = 256
        for i in range(past // rb):
            r0 = i * rb
            k_s[r0:r0 + rb, :] = ck_ref[0, r0:r0 + rb, :].astype(BF16)
            vt_s[:, r0:r0 + rb] = cv_ref[0, r0:r0 + rb, :].T.astype(BF16)
        for i in range(t_seq // rb):
            r0 = i * rb
            kn = head_norm(kv_ref[0, r0:r0 + rb, 0:KV_W], kg_ref[...])
            if latent:
                kn = rope(kn, ckk_ref[r0:r0 + rb, :], skk_ref[r0:r0 + rb, :])
            else:
                kn_ref[0, r0:r0 + rb, :] = kn
            k_s[past + r0:past + r0 + rb, :] = kn.astype(BF16)
            vt_s[:, past + r0:past + r0 + rb] = kv_ref[0, r0:r0 + rb, KV_W:2 * KV_W].T.astype(BF16)

    slabs = []
    for j in range(ATTN_W // LANES):
        qn = head_norm(q_ref[0, :, LANES * j:LANES * (j + 1)], qg_ref[...])
        if latent:
            qn = rope(qn, cq_ref[...], sq_ref[...])
        slabs.append(qn * (HEAD_DIM ** -0.5))
    q_t = jnp.concatenate(slabs, axis=1).T.astype(BF16)
    zero_h = jnp.zeros((HEAD_DIM, tq), BF16)
    outs = []
    for g in range(N_KV_HEADS):
        cols = []
        for h in range(q_per_kv * g, q_per_kv * (g + 1)):
            q_h = q_t[HEAD_DIM * h:HEAD_DIM * (h + 1), :]
            cols.append(jnp.concatenate([q_h, zero_h] if g == 0 else [zero_h, q_h], axis=0))
        qz = jnp.concatenate(cols, axis=1)
        s_t = _dot(k_s[...], qz)
        p_t = jnp.exp(s_t - jnp.max(s_t, axis=0, keepdims=True))
        den = jnp.sum(p_t, axis=0, keepdims=True)
        o_t = _dot(vt_s[HEAD_DIM * g:HEAD_DIM * (g + 1), :], p_t.astype(BF16)) / den
        outs += [o_t[:, tq * h:tq * (h + 1)] for h in range(q_per_kv)]
    o_ref[0] = jnp.concatenate(outs, axis=0).T


def _attention(q, kv, qg, kg, ctx_kv, rope_tabs):
    b, t, _ = q.shape
    latent = ctx_kv is not None
    past = ctx_kv[0].shape[1] if latent else 0
    s_all = past + t
    tq = 128
    kern = functools.partial(_attn_kernel, t_seq=t, past=past, tq=tq, latent=latent)
    in_specs = [
        pl.BlockSpec((1, tq, ATTN_W), lambda bi, qi: (bi, qi, 0)),
        pl.BlockSpec((1, t, 2 * KV_W), lambda bi, qi: (bi, 0, 0)),
    ]
    args = [q, kv]
    if latent:
        in_specs += [pl.BlockSpec((1, past, KV_W), lambda bi, qi: (bi, 0, 0))] * 2
        args += list(ctx_kv)
    in_specs += [pl.BlockSpec((1, LANES), lambda bi, qi: (0, 0))] * 2
    args += [qg, kg]
    out_specs = [pl.BlockSpec((1, tq, ATTN_W), lambda bi, qi: (bi, qi, 0))]
    out_shape = [jax.ShapeDtypeStruct((b, t, ATTN_W), F32)]
    if latent:
        cos_t, sin_t = rope_tabs
        in_specs += [pl.BlockSpec((tq, LANES), lambda bi, qi: (qi, 0))] * 2
        in_specs += [pl.BlockSpec((t, LANES), lambda bi, qi: (0, 0))] * 2
        args += [cos_t, sin_t, cos_t, sin_t]
    else:
        out_specs.append(pl.BlockSpec((1, t, KV_W), lambda bi, qi: (bi, 0, 0)))
        out_shape.append(jax.ShapeDtypeStruct((b, t, KV_W), F32))
    res = pl.pallas_call(
        kern,
        grid=(b, t // tq),
        in_specs=in_specs,
        out_specs=out_specs,
        out_shape=out_shape,
        scratch_shapes=[pltpu.VMEM((s_all, LANES), BF16), pltpu.VMEM((LANES, s_all), BF16)],
        compiler_params=_cparams(("arbitrary", "arbitrary")),
        name="attention_lat" if latent else "attention_ctx",
    )(*args)
    return res if not latent else (res[0], None)


def _sgu_kernel(su_ref, sv_ref, g_ref, ws_ref, bs_ref, o_ref):
    v = sv_ref[...]
    vn = (v * lax.rsqrt(jnp.mean(v * v, axis=-1, keepdims=True) + EPS) * g_ref[...]).astype(BF16)
    for g in range(SGU_GROUPS):
        sl = slice(LANES * g, LANES * (g + 1))
        o_ref[:, sl] = su_ref[:, sl] * (_dot(ws_ref[g], vn[:, sl]) + bs_ref[g])


def _sgu(su, sv, g, ws_b, bs_b):
    m = su.shape[0]
    return pl.pallas_call(
        _sgu_kernel,
        grid=(m // CHUNK,),
        in_specs=[
            pl.BlockSpec((CHUNK, SGU_W), lambda i: (i, 0)),
            pl.BlockSpec((CHUNK, SGU_W), lambda i: (i, 0)),
            pl.BlockSpec((1, SGU_W), lambda i: (0, 0)),
            pl.BlockSpec((SGU_GROUPS, CHUNK, CHUNK), lambda i: (0, 0, 0)),
            pl.BlockSpec((SGU_GROUPS, CHUNK, LANES), lambda i: (0, 0, 0)),
        ],
        out_specs=pl.BlockSpec((CHUNK, SGU_W), lambda i: (i, 0)),
        out_shape=jax.ShapeDtypeStruct((m, SGU_W), F32),
        compiler_params=_cparams(("arbitrary",)),
        name="sgu",
    )(su, sv, g, ws_b, bs_b)


def _wkv_kernel(*refs, t_seq, latent):
    (r_ref, k_ref, v_ref, lora_ref, cwr_ref, cwk_ref, cwv_ref, w0_ref, a0_ref, w2_ref, a2_ref, g2_ref,
     kk_ref, ka_ref, rk_ref, lng_ref, lnb_ref) = refs[:17]
    rest = refs[17:]
    if latent:
        st0_ref, o_ref = rest[:2]
        scr = rest[2:]
        sf_ref = None
    else:
        o_ref, sf_ref = rest[:2]
        scr = rest[2:]
        st0_ref = None
    tar_s, tv_s, ov_s, arb_s, w_s, n_s, gc_s, st_s, bon_s = scr
    c_len = WKV_C
    c2 = 2 * c_len
    nc = t_seq // c_len

    lane = lax.broadcasted_iota(jnp.int32, (1, LANES), 1)
    lo = lane < R_HEAD
    sum_bd = _head_block_diag(1.0)
    mean_bd = _head_block_diag(1.0 / R_HEAD)
    row_c = lax.broadcasted_iota(jnp.int32, (c_len, LANES), 0)
    ri = lax.broadcasted_iota(jnp.int32, (c2, c2), 0)
    ci = lax.broadcasted_iota(jnp.int32, (c2, c2), 1)
    same_head = (ri // c_len) == (ci // c_len)
    tt, ss = ri % c_len, ci % c_len
    eye2 = jnp.where(ri == ci, 1.0, 0.0).astype(F32)
    strict = (same_head & (ss < tt), same_head & (ss > tt))
    incl = (same_head & (ss <= tt), same_head & (ss >= tt))
    ti = lax.broadcasted_iota(jnp.int32, (c_len, c_len), 0)
    si = lax.broadcasted_iota(jnp.int32, (c_len, c_len), 1)
    cum_incl = (jnp.where(si <= ti, 1.0, 0.0).astype(BF16), jnp.where(si >= ti, 1.0, 0.0).astype(BF16))
    sum_bd_b = sum_bd.astype(BF16)
    mean_bd_b = mean_bd.astype(BF16)

    def bdot(a, b):
        return _dot(a.astype(BF16), b.astype(BF16))

    def split2(x):
        hi = x.astype(BF16)
        return hi, (x - hi.astype(F32)).astype(BF16)

    def head_reduce(x, bd_b):
        hi, lo = split2(x)
        return _dot(hi, bd_b) + _dot(lo, bd_b)

    def stack_heads(x):
        return jnp.concatenate([jnp.where(lo, x, 0.0), jnp.where(lo, 0.0, x)], axis=0)

    def fold_heads(x):
        return x[0:c_len] + x[c_len:c2]

    def conv_tokens(ref, w_ref, c, t0):
        x = ref[0, pl.ds(t0, c_len), :]
        prev = ref[0, pl.ds(jnp.maximum(t0 - 1, 0), 1), :]
        nxt = ref[0, pl.ds(jnp.minimum(t0 + c_len, t_seq - 1), 1), :]
        prev = jnp.where(c > 0, prev, 0.0)
        nxt = jnp.where(c < nc - 1, nxt, 0.0)
        xm = jnp.where(row_c == 0, prev, pltpu.roll(x, 1, 0))
        xp = jnp.where(row_c == c_len - 1, nxt, pltpu.roll(x, c_len - 1, 0))
        return xm * w_ref[0:1, :] + x * w_ref[1:2, :] + xp * w_ref[2:3, :]

    def chunk_group(chunks):
        n = len(chunks)
        t0s = [pl.multiple_of(c * c_len, c_len) for c in chunks]
        rc = [conv_tokens(r_ref, cwr_ref, c, t0) for c, t0 in zip(chunks, t0s)]
        kc = [conv_tokens(k_ref, cwk_ref, c, t0) for c, t0 in zip(chunks, t0s)]
        vc = [conv_tokens(v_ref, cwv_ref, c, t0) for c, t0 in zip(chunks, t0s)]
        kkr = [k * kk_ref[...] for k in kc]
        kk = [x * lax.rsqrt(head_reduce(x * x, sum_bd_b) + 1e-12) for x in kkr]
        tw = [jnp.tanh(lora_ref[0, pl.ds(t0, c_len), 0:2 * LORA_W]) for t0 in t0s]
        xa = [lora_ref[0, pl.ds(t0, c_len), 2 * LORA_W:2 * LORA_W + 2 * LORA_A] for t0 in t0s]
        v_hs = [stack_heads(v).astype(BF16) for v in vc]
        prob =[(i, d) for i in range(n) for d in range(2)]
        dsel = (lo, jnp.logical_not(lo))
        lw = [-DECAY_SCALE * _sigmoid(w0_ref[d:d + 1, :] + _dot(jnp.where(dsel[d], tw[i], 0.0).astype(BF16), w2_ref[...]))
              for i, d in prob]
        ag = [_sigmoid(a0_ref[d:d + 1, :] + _dot(jnp.where(dsel[d], xa[i], 0.0).astype(BF16), a2_ref[...]))
              for i, d in prob]
        kd = [kc[i] * (1.0 + (ag[p] - 1.0) * ka_ref[...]) for p, (i, d) in enumerate(prob)]
        kb = [kk[i] * ag[p] for p, (i, d) in enumerate(prob)]
        bon = [head_reduce(rc[i] * kd[p] * rk_ref[...], sum_bd_b) * vc[i] for p, (i, d) in enumerate(prob)]
        for i in range(n):
            bon_s[pl.ds(t0s[i], c_len), :] = bon[2 * i] + bon[2 * i + 1]
        lw_sp = [split2(x) for x in lw]
        cs_in = [_dot(cum_incl[d], lw_sp[p][0]) + _dot(cum_incl[d], lw_sp[p][1]) for p, (i, d) in enumerate(prob)]
        tot = [cs_in[p][c_len - 1:c_len, :] if d == 0 else cs_in[p][0:1, :] for p, (i, d) in enumerate(prob)]
        e_neg = [jnp.exp(-x) for x in cs_in]
        e_rem = [jnp.exp(tot[p] - cs_in[p]) for p in range(len(prob))]
        a_t = [-kk[i] * jnp.exp(cs_in[p] - (lw_sp[p][0].astype(F32) + lw_sp[p][1].astype(F32)))
               for p, (i, d) in enumerate(prob)]
        r_t = [rc[i] * jnp.exp(cs_in[p]) for p, (i, d) in enumerate(prob)]
        a_hs = [stack_heads(x).astype(BF16) for x in a_t]
        lhs = [jnp.concatenate([a_hs[p], stack_heads(r_t[p]).astype(BF16)], axis=0) for p in range(len(prob))]
        rhs = [jnp.concatenate([stack_heads(kd[p] * e_neg[p]), stack_heads(kb[p] * e_neg[p])], axis=0).astype(BF16)
               for p in range(len(prob))]
        m4 = [_dot_nt(a, b) for a, b in zip(lhs, rhs)]
        a_ak = [jnp.where(strict[d], m4[p][0:c2, 0:c2], 0.0).astype(BF16) for p, (i, d) in enumerate(prob)]
        l_ab = [jnp.where(strict[d], m4[p][0:c2, c2:2 * c2], 0.0) for p, (i, d) in enumerate(prob)]
        a_rk = [jnp.where(incl[d], m4[p][c2:2 * c2, 0:c2], 0.0).astype(BF16) for p, (i, d) in enumerate(prob)]
        for p, (i, d) in enumerate(prob):
            arb_s[d, chunks[i]] = jnp.where(incl[d], m4[p][c2:2 * c2, c2:2 * c2], 0.0).astype(BF16)
        t_inv = [eye2 + x for x in l_ab]
        pw_b = [x.astype(BF16) for x in l_ab]
        pw_b = [_dot(x, x).astype(BF16) for x in pw_b]
        for _ in range(4):
            res = [_dot(x, jnp.concatenate([x, t.astype(BF16)], axis=1)) for x, t in zip(pw_b, t_inv)]
            pw_b = [r[:, 0:c2].astype(BF16) for r in res]
            t_inv = [t + r[:, c2:2 * c2] for t, r in zip(t_inv, res)]
        t_inv = [t + _dot(x, t.astype(BF16)) for x, t in zip(pw_b, t_inv)]
        t_b = [t.astype(BF16) for t in t_inv]
        av = [_dot(a_ak[p], v_hs[i]).astype(BF16) for p, (i, d) in enumerate(prob)]
        ov = [fold_heads(_dot(a_rk[p], v_hs[i])) for p, (i, d) in enumerate(prob)]
        tt = [_dot(t, jnp.concatenate([a, x], axis=1)) for t, a, x in zip(t_b, a_hs, av)]
        ta = [fold_heads(r[:, 0:LANES]) for r in tt]
        tv = [fold_heads(r[:, LANES:2 * LANES]) for r in tt]
        kbt = [jnp.concatenate([kd[p] * e_rem[p], kb[p] * e_rem[p]], axis=0).T.astype(BF16) for p in range(len(prob))]
        zero_c = jnp.zeros((c_len, LANES), BF16)
        upd_rhs = [jnp.concatenate([jnp.concatenate([vc[i].astype(BF16), zero_c], axis=1),
                                    jnp.concatenate([tv[p].astype(BF16), ta[p].astype(BF16)], axis=1)], axis=0)
                   for p, (i, d) in enumerate(prob)]
        nw = [_dot(a, b) for a, b in zip(kbt, upd_rhs)]
        for p, (i, d) in enumerate(prob):
            c = chunks[i]
            tar_s[d, c] = jnp.concatenate([ta[p], r_t[p]], axis=0).astype(BF16)
            tv_s[d, c] = tv[p]
            ov_s[d, c] = ov[p]
            n_s[d, c] = jnp.where(bd_mask, nw[p][:, 0:LANES], 0.0)
            w_s[d, c] = jnp.where(bd_mask, nw[p][:, LANES:2 * LANES], 0.0).astype(BF16)
            gc_s[d, c] = jnp.broadcast_to(jnp.exp(tot[p]), (LANES, LANES)).T

    bd_mask = sum_bd > 0.5
    unroll1 = min(8, nc)

    def phase1(i, carry):
        chunk_group([i * unroll1 + j for j in range(unroll1)])
        return carry

    lax.fori_loop(0, nc // unroll1, phase1, 0)

    def phase2(i, carry):
        cs = (i, nc - 1 - i)
        st_b = [carry[d].astype(BF16) for d in range(2)]
        for d in range(2):
            st_s[d, cs[d]] = st_b[d]
        prod = [_dot(w_s[d, cs[d]], st_b[d]) for d in range(2)]
        return tuple(gc_s[d, cs[d]] * carry[d] + prod[d] + n_s[d, cs[d]] for d in range(2))

    if latent:
        init = (st0_ref[0, 0, 0], st0_ref[0, 1, 0])
    else:
        init = (jnp.zeros((LANES, LANES), F32),) * 2
    fin = lax.fori_loop(0, nc, phase2, init)
    if sf_ref is not None:
        sf_ref[0, 0, 0] = fin[0]
        sf_ref[0, 1, 0] = fin[1]

    def out_group(chunks):
        prob = [(c, d) for c in chunks for d in range(2)]
        both = [_dot(tar_s[d, c], st_s[d, c]) for c, d in prob]
        u = [both[p][0:c_len] + tv_s[d, c] for p, (c, d) in enumerate(prob)]
        ou = [_dot(arb_s[d, c], stack_heads(u[p]).astype(BF16)) for p, (c, d) in enumerate(prob)]
        o = [ov_s[d, c] + both[p][c_len:c2] + fold_heads(ou[p]) for p, (c, d) in enumerate(prob)]
        of = [o[2 * j] + o[2 * j + 1] for j in range(len(chunks))]
        xc = [x - head_reduce(x, mean_bd_b) for x in of]
        var = [head_reduce(x * x, mean_bd_b) for x in xc]
        for j, c in enumerate(chunks):
            t0 = pl.multiple_of(c * c_len, c_len)
            on = xc[j] * lax.rsqrt(var[j] + GN_EPS) * lng_ref[...] + lnb_ref[...]
            xg = lora_ref[0, pl.ds(t0, c_len), 2 * LORA_W + 2 * LORA_A:LORA_ALL]
            g_out = _dot(_sigmoid(xg).astype(BF16), g2_ref[...])
            o_ref[0, pl.ds(t0, c_len), :] = (on + bon_s[pl.ds(t0, c_len), :]) * g_out

    unroll3 = 4

    def phase3(i, carry):
        out_group([i * unroll3 + j for j in range(unroll3)])
        return carry

    lax.fori_loop(0, nc // unroll3, phase3, 0)


def _wkv(rkv, lora, wts, st0):
    b, t, _ = rkv.shape
    latent = st0 is not None
    nc = t // WKV_C
    npair = RW // LANES
    kern = functools.partial(_wkv_kernel, t_seq=t, latent=latent)

    def seq_spec(col0):
        return pl.BlockSpec((1, t, LANES), lambda bi, p: (bi, 0, col0 + p))

    def row_spec(rows, col0=0):
        return pl.BlockSpec((rows, LANES), lambda bi, p: (0, col0 + p))

    state_spec = pl.BlockSpec((1, 2, 1, LANES, LANES), lambda bi, p: (bi, 0, p, 0, 0))
    in_specs = [
        seq_spec(0), seq_spec(npair), seq_spec(2 * npair),
        pl.BlockSpec((1, t, LORA_ALL), lambda bi, p: (bi, 0, 0)),
        row_spec(3, 0), row_spec(3, npair), row_spec(3, 2 * npair),
        row_spec(2), row_spec(2), row_spec(LANES), row_spec(LANES), row_spec(LANES),
        row_spec(1), row_spec(1), row_spec(1), row_spec(1), row_spec(1),
    ]
    args = [rkv, rkv, rkv, lora, wts["conv"], wts["conv"], wts["conv"], wts["w0"], wts["a0"], wts["w2"], wts["a2"],
            wts["g2"], wts["k_k"], wts["k_a"], wts["r_k"], wts["ln_g"], wts["ln_b"]]
    out_specs = [seq_spec(0)]
    out_shape = [jax.ShapeDtypeStruct((b, t, RW), F32)]
    if latent:
        in_specs.append(state_spec)
        args.append(st0)
    else:
        out_specs.append(state_spec)
        out_shape.append(jax.ShapeDtypeStruct((b, 2, npair, LANES, LANES), F32))
    chunk_rows = pltpu.VMEM((2, nc, WKV_C, LANES), F32)
    chunk_sq = pltpu.VMEM((2, nc, LANES, LANES), F32)
    chunk_sq_b = pltpu.VMEM((2, nc, LANES, LANES), BF16)
    res = pl.pallas_call(
        kern,
        grid=(b, npair),
        in_specs=in_specs,
        out_specs=out_specs,
        out_shape=out_shape,
        scratch_shapes=[chunk_sq_b, chunk_rows, chunk_rows, chunk_sq_b, chunk_sq_b, chunk_sq, chunk_sq, chunk_sq_b,
                        pltpu.VMEM((t, LANES), F32)],
        compiler_params=_cparams(("arbitrary", "arbitrary")),
        name="wkv_lat" if latent else "wkv_ctx",
    )(*args)
    return (res[0], None) if latent else (res[0], res[1])


def _merge_kernel(x_ref, oa_ref, os_ref, or_ref, gates_ref, gt_ref, pa_ref, ps_ref, pr_ref, wo_ref, o_ref,
                  *, row0, rstride, bps):
    row = row0 + (pl.program_id(0) // bps) * rstride if rstride else row0
    d = D_MODEL
    merged = (_sigmoid(gates_ref[:, 0:d]) * _dot(oa_ref[...].astype(BF16), pa_ref[...])
              + _sigmoid(gates_ref[:, d:2 * d]) * _dot(os_ref[...].astype(BF16), ps_ref[...])
              + _sigmoid(gates_ref[:, 2 * d:3 * d]) * _dot(or_ref[...].astype(BF16), pr_ref[...]))
    o_ref[...] = x_ref[...] + _mod_row(gt_ref, row) * _dot(merged.astype(BF16), wo_ref[...])


def _merge(x2, oa, osg, orw, gates, mod_l, pa, ps, pr, wo, row0, rstride, t_seq):
    m, d = x2.shape
    tm = 256
    kern = functools.partial(_merge_kernel, row0=row0, rstride=rstride, bps=t_seq // tm)
    tok = lambda w: pl.BlockSpec((tm, w), lambda i: (i, 0))
    full = lambda a: pl.BlockSpec(a.shape, lambda i: (0, 0))
    return pl.pallas_call(
        kern,
        grid=(m // tm,),
        in_specs=[tok(d), tok(ATTN_W), tok(SGU_W), tok(RW), tok(3 * d),
                  pl.BlockSpec((8, d), lambda i: (0, 2)),
                  full(pa), full(ps), full(pr), full(wo)],
        out_specs=tok(d),
        out_shape=jax.ShapeDtypeStruct((m, d), F32),
        compiler_params=_cparams(("arbitrary",)),
        name="merge",
    )(x2, oa, osg, orw, gates, mod_l, pa, ps, pr, wo)


def _moe_kernel(x_ref, sh_ref, sc_ref, gt_ref, g2_ref, rw_ref, rb_ref, shg_ref, shu_ref, shd_ref,
                wgu_ref, wd_ref, o_ref, h_s, comb_s, acc_s, *, row0, rstride, bps):
    row = row0 + (pl.program_id(0) // bps) * rstride if rstride else row0
    e = pl.program_id(1)
    tm = x_ref.shape[0]
    lane = lax.broadcasted_iota(jnp.int32, (tm, LANES), 1).astype(F32)

    @pl.when(e == 0)
    def _():
        x = x_ref[...]
        h = x * lax.rsqrt(jnp.mean(x * x, axis=-1, keepdims=True) + EPS) * g2_ref[...]
        h = h * (1.0 + _mod_row(sc_ref, row)) + _mod_row(sh_ref, row)
        scores = _sigmoid(_dot(h, rw_ref[...], HI))
        valid = lane < N_EXPERTS
        sel = jnp.where(valid, scores + rb_ref[...], -jnp.inf)
        picked = jnp.zeros((tm, LANES), jnp.bool_)
        for _ in range(TOP_K):
            best = jnp.max(sel, axis=-1, keepdims=True)
            first = jnp.min(jnp.where(sel == best, lane, float(LANES)), axis=-1, keepdims=True)
            hit = lane == first
            picked = jnp.logical_or(picked, hit)
            sel = jnp.where(hit, -jnp.inf, sel)
        sw = jnp.where(picked, scores, 0.0)
        comb = sw / jnp.sum(sw, axis=-1, keepdims=True) * ROUTED_SCALE
        for b in range(N_EXPERTS // MOE_EB):
            comb_s[b] = comb if b == 0 else pltpu.roll(comb, LANES - MOE_EB * b, 1)
        hb = h.astype(BF16)
        h_s[...] = hb
        acc_s[...] = _dot((_silu(_dot(hb, shg_ref[...])) * _dot(hb, shu_ref[...])).astype(BF16), shd_ref[...])

    hb = h_s[...]
    comb = comb_s[e]
    hid = []
    for j in range(MOE_EB):
        gu = _dot(hb, wgu_ref[j])
        cw = jnp.broadcast_to(comb[:, j:j + 1], (tm, D_EXPERT))
        hid.append((_silu(gu[:, 0:D_EXPERT]) * gu[:, D_EXPERT:2 * D_EXPERT] * cw).astype(BF16))
    acc_s[...] += _dot(jnp.concatenate(hid, axis=1), wd_ref[...])

    @pl.when(e == pl.num_programs(1) - 1)
    def _():
        o_ref[...] = x_ref[...] + _mod_row(gt_ref, row) * acc_s[...]


def _moe(x2, mod_l, g2, rw_p, rb_p, shg, shu, shd, wgu, wd, row0, rstride, t_seq):
    m, d = x2.shape
    tm = min(1024, t_seq if rstride else m)
    n_hid = MOE_EB * D_EXPERT
    kern = functools.partial(_moe_kernel, row0=row0, rstride=rstride, bps=max(t_seq // tm, 1))
    full = lambda a: pl.BlockSpec(a.shape, lambda i, e: (0, 0))
    return pl.pallas_call(
        kern,
        grid=(m // tm, N_EXPERTS // MOE_EB),
        in_specs=[
            pl.BlockSpec((tm, d), lambda i, e: (i, 0)),
            pl.BlockSpec((8, d), lambda i, e: (0, 3)),
            pl.BlockSpec((8, d), lambda i, e: (0, 4)),
            pl.BlockSpec((8, d), lambda i, e: (0, 5)),
            full(g2), full(rw_p), full(rb_p), full(shg), full(shu), full(shd),
            pl.BlockSpec((MOE_EB, d, 2 * D_EXPERT), lambda i, e: (e, 0, 0)),
            pl.BlockSpec((n_hid, d), lambda i, e: (e, 0)),
        ],
        out_specs=pl.BlockSpec((tm, d), lambda i, e: (i, 0)),
        out_shape=jax.ShapeDtypeStruct((m, d), F32),
        scratch_shapes=[pltpu.VMEM((tm, d), BF16), pltpu.VMEM((N_EXPERTS // MOE_EB, tm, LANES), F32),
                        pltpu.VMEM((tm, d), F32)],
        compiler_params=_cparams(("arbitrary", "arbitrary")),
        name="moe",
    )(x2, mod_l, mod_l, mod_l, g2, rw_p, rb_p, shg, shu, shd, wgu, wd)


def _layer(x, mod_l, w, row0, rstride, ctx, rope_tabs):
    b, t, d = x.shape
    x2 = x.reshape(b * t, d)
    q, kv, su, sv, rkv, lora, gates = _inproj(x2, mod_l, w["norm1_g"], w["w_in"], row0, rstride, t)
    if ctx is None:
        o_attn, k_new = _attention(q.reshape(b, t, -1), kv.reshape(b, t, -1), w["q_norm"], w["k_norm"], None, None)
        st0 = None
    else:
        o_attn, k_new = _attention(q.reshape(b, t, -1), kv.reshape(b, t, -1), w["q_norm"], w["k_norm"],
                                   ctx[:2], rope_tabs)
        st0 = ctx[2]
    o_sgu = _sgu(su, sv, w["sgu_norm_g"], w["sgu_ws"], w["sgu_bs"])
    o_wkv, s_fin = _wkv(rkv.reshape(b, t, -1), lora.reshape(b, t, -1), w["wkv"], st0)
    x1 = _merge(x2, o_attn.reshape(b * t, -1), o_sgu, o_wkv.reshape(b * t, -1), gates, mod_l,
                w["proj_attn"], w["proj_sgu"], w["proj_rwkv"], w["w_out"], row0, rstride, t)
    x_out = _moe(x1, mod_l, w["norm2_g"], w["router_w"], w["router_bias"], w["sh_gate"], w["sh_up"], w["sh_down"],
                 w["exp_gu"], w["exp_down"], row0, rstride, t)
    return x_out.reshape(b, t, d), (k_new, kv, s_fin)


def _rope_tables(n_tok):
    t = jnp.arange(n_tok)
    row = (t // GRID_W).astype(F32)
    col = (t % GRID_W).astype(F32)
    inv = ROPE_THETA ** (-jnp.arange(N_FREQ, dtype=F32) / N_FREQ)
    ang = jnp.stack([row[:, None] * inv[None, :], col[:, None] * inv[None, :]], axis=1)
    cos, sin = jnp.cos(ang), jnp.sin(ang)
    cos64 = jnp.stack([cos, cos], axis=2).reshape(n_tok, HEAD_DIM)
    sin64 = jnp.stack([-sin, sin], axis=2).reshape(n_tok, HEAD_DIM)
    return jnp.tile(cos64, (1, LANES // HEAD_DIM)), jnp.tile(sin64, (1, LANES // HEAD_DIM))


def _layer_weights(l, mod_w, mod_b, norm1_g, norm2_g, w_in, q_norm, k_norm, sgu_norm_g, sgu_ws, sgu_bs, rwkv_conv,
                   rwkv_w0, rwkv_w2, rwkv_a0, rwkv_a2, rwkv_g2, rwkv_k_k, rwkv_k_a, rwkv_r_k, rwkv_ln_g, rwkv_ln_b,
                   proj_attn, proj_sgu, proj_rwkv, w_out, router_w, router_bias, exp_gate, exp_up, exp_down,
                   sh_gate, sh_up, sh_down):
    d = D_MODEL
    row = lambda a: a.reshape(1, -1)
    pad_e = LANES - N_EXPERTS
    return {
        "norm1_g": row(norm1_g[l]), "norm2_g": row(norm2_g[l]),
        "w_in": w_in[l].astype(BF16),
        "q_norm": jnp.tile(row(q_norm[l]), (1, LANES // HEAD_DIM)),
        "k_norm": jnp.tile(row(k_norm[l]), (1, LANES // HEAD_DIM)),
        "sgu_norm_g": row(sgu_norm_g[l]),
        "sgu_ws": sgu_ws[l].astype(BF16),
        "sgu_bs": jnp.broadcast_to(sgu_bs[l][:, :, None], (SGU_GROUPS, CHUNK, LANES)),
        "wkv": {
            "conv": rwkv_conv[l],
            "w0": rwkv_w0[l], "a0": rwkv_a0[l],
            "w2": rwkv_w2[l].reshape(2 * LORA_W, RW).astype(BF16),
            "a2": rwkv_a2[l].reshape(2 * LORA_A, RW).astype(BF16),
            "g2": rwkv_g2[l].astype(BF16),
            "k_k": row(rwkv_k_k[l]), "k_a": row(rwkv_k_a[l]), "r_k": row(rwkv_r_k[l]),
            "ln_g": row(rwkv_ln_g[l]), "ln_b": row(rwkv_ln_b[l]),
        },
        "proj_attn": proj_attn[l].astype(BF16), "proj_sgu": proj_sgu[l].astype(BF16),
        "proj_rwkv": proj_rwkv[l].astype(BF16), "w_out": w_out[l].astype(BF16),
        "router_w": jnp.pad(router_w[l], ((0, 0), (0, pad_e))),
        "router_bias": jnp.pad(row(router_bias[l]), ((0, 0), (0, pad_e))),
        "sh_gate": sh_gate[l].astype(BF16), "sh_up": sh_up[l].astype(BF16), "sh_down": sh_down[l].astype(BF16),
        "exp_gu": jnp.concatenate([exp_gate[l], exp_up[l]], axis=-1).astype(BF16),
        "exp_down": exp_down[l].astype(BF16).reshape(N_EXPERTS * D_EXPERT, d),
    }


def _state_to_blockdiag_t(s):
    b = s.shape[0]
    st = jnp.swapaxes(s, -1, -2).reshape(b, 2, R_HEADS // 2, 2, R_HEAD, R_HEAD)
    bd = jnp.einsum("bdpjkv,ji->bdpjkiv", st, jnp.eye(2, dtype=s.dtype))
    return bd.reshape(b, 2, R_HEADS // 2, LANES, LANES)


def _blockdiag_t_to_state(sf):
    b = sf.shape[0]
    x = sf.reshape(b, 2, R_HEADS // 2, 2, R_HEAD, 2, R_HEAD)
    diag = jnp.stack([x[:, :, :, 0, :, 0, :], x[:, :, :, 1, :, 1, :]], axis=3)
    return jnp.swapaxes(diag, -1, -2).reshape(b, 2, R_HEADS, R_HEAD, R_HEAD)


def kernel(x_prompt, x_sample, cache_k, cache_v, state_wkv, c, c_ctx, mod_w, mod_b, norm1_g, norm2_g, w_in, q_norm, k_norm, sgu_norm_g, sgu_ws, sgu_bs, rwkv_conv, rwkv_w0, rwkv_w2, rwkv_a0, rwkv_a2, rwkv_g2, rwkv_k_k, rwkv_k_a, rwkv_r_k, rwkv_ln_g, rwkv_ln_b, proj_attn, proj_sgu, proj_rwkv, w_out, router_w, router_bias, exp_gate, exp_up, exp_down, sh_gate, sh_up, sh_down):
    params = (mod_w, mod_b, norm1_g, norm2_g, w_in, q_norm, k_norm, sgu_norm_g, sgu_ws, sgu_bs, rwkv_conv,
              rwkv_w0, rwkv_w2, rwkv_a0, rwkv_a2, rwkv_g2, rwkv_k_k, rwkv_k_a, rwkv_r_k, rwkv_ln_g, rwkv_ln_b,
              proj_attn, proj_sgu, proj_rwkv, w_out, router_w, router_bias, exp_gate, exp_up, exp_down,
              sh_gate, sh_up, sh_down)
    n_b, n_t, d = x_prompt.shape
    n_db, n_dt, _ = x_sample.shape
    assert n_db + 1 <= 8
    cmat = jnp.zeros((8, d), F32).at[0].set(c_ctx).at[1:1 + n_db].set(c)
    mod = _modulation(cmat, mod_w, mod_b)
    rope_tabs = _rope_tables(n_dt)
    xp, xs = x_prompt, x_sample
    ks, vs, ss = [], [], []
    for l in range(DEPTH):
        w = _layer_weights(l, *params)
        xp, (k_new, kv, s_fin) = _layer(xp, mod[l], w, 0, 0, None, None)
        ks.append(k_new.reshape(n_b, n_t, N_KV_HEADS, HEAD_DIM))
        vs.append(kv.reshape(n_b, n_t, 2 * KV_W)[:, :, KV_W:].reshape(n_b, n_t, N_KV_HEADS, HEAD_DIM))
        ss.append(_blockdiag_t_to_state(s_fin))
        ctx = (cache_k[:, l].reshape(n_db, -1, KV_W), cache_v[:, l].reshape(n_db, -1, KV_W),
               _state_to_blockdiag_t(state_wkv[:, l]))
        xs, _ = _layer(xs, mod[l], w, 1, 1, ctx, rope_tabs)
    return (xp, xs, jnp.stack(ks, axis=1), jnp.stack(vs, axis=1), jnp.stack(ss, axis=1))
```

```python
import functools

import jax
import jax.numpy as jnp
from jax import lax
from jax.experimental import pallas as pl
from jax.experimental.pallas import tpu as pltpu

F32 = jnp.float32
BF16 = jnp.bfloat16
HI = lax.Precision.HIGHEST

D_MODEL = 1024
DEPTH = 2
GRID_W = 64
N_HEADS = 8
N_KV_HEADS = 2
HEAD_DIM = 64
ATTN_W = N_HEADS * HEAD_DIM
KV_W = N_KV_HEADS * HEAD_DIM
N_FREQ = HEAD_DIM // 4
ROPE_THETA = 10000.0
CHUNK = 128
SGU_GROUPS = 4
SGU_W = 512
R_HEADS = 8
R_HEAD = 64
RW = R_HEADS * R_HEAD
LORA_W = 64
LORA_A = 64
LORA_G = 128
DECAY_SCALE = 0.606531
GN_EPS = 64e-5
N_EXPERTS = 64
TOP_K = 6
D_EXPERT = 128
D_SHARED = 256
ROUTED_SCALE = 2.5
EPS = 1e-6
LORA_ALL = 2 * LORA_W + 2 * LORA_A + LORA_G
N_IN = ATTN_W + 2 * KV_W + 2 * SGU_W + 3 * RW + LORA_ALL + 3 * D_MODEL

LANES = 128
WKV_C = 64
MOE_EB = 8
VMEM_LIMIT = 56 * 1024 * 1024


def _cparams(sem, vmem=VMEM_LIMIT):
    return pltpu.CompilerParams(dimension_semantics=sem, vmem_limit_bytes=vmem)


def _dot(a, b, prec=None):
    return jnp.dot(a, b, preferred_element_type=F32, precision=prec)


def _dot_nt(a, b, prec=None):
    return lax.dot_general(a, b, (((1,), (1,)), ((), ())), preferred_element_type=F32, precision=prec)


def _sigmoid(x):
    return 1.0 / (1.0 + jnp.exp(-x))


def _silu(x):
    return x / (1.0 + jnp.exp(-x))


def _head_block_diag(value):
    r = lax.broadcasted_iota(jnp.int32, (LANES, LANES), 0) // HEAD_DIM
    c = lax.broadcasted_iota(jnp.int32, (LANES, LANES), 1) // HEAD_DIM
    return jnp.where(r == c, value, 0.0).astype(F32)


def _mod_row(ref, row):
    return ref[pl.ds(row, 1), :]


def _mod_kernel(c_ref, w_ref, b_ref, o_ref):
    o_ref[0] = _dot(_silu(c_ref[...]), w_ref[0], HI) + b_ref[0]


def _modulation(cmat, mod_w, mod_b):
    n_l, d, n6 = mod_w.shape
    tn = n6 // 4
    return pl.pallas_call(
        _mod_kernel,
        grid=(n_l, n6 // tn),
        in_specs=[
            pl.BlockSpec((8, d), lambda l, j: (0, 0)),
            pl.BlockSpec((1, d, tn), lambda l, j: (l, 0, j)),
            pl.BlockSpec((1, 1, tn), lambda l, j: (l, 0, j)),
        ],
        out_specs=pl.BlockSpec((1, 8, tn), lambda l, j: (l, 0, j)),
        out_shape=jax.ShapeDtypeStruct((n_l, 8, n6), F32),
        compiler_params=_cparams(("arbitrary", "arbitrary")),
        name="modulation",
    )(cmat, mod_w, mod_b.reshape(n_l, 1, n6))


_IN_SPLIT = (ATTN_W, 2 * KV_W, SGU_W, SGU_W, 3 * RW, LORA_ALL, 3 * D_MODEL)


def _inproj_kernel(x_ref, sh_ref, sc_ref, g_ref, w_ref, *out_refs, row0, rstride, bps):
    row = row0 + (pl.program_id(0) // bps) * rstride if rstride else row0
    x = x_ref[...]
    h = x * lax.rsqrt(jnp.mean(x * x, axis=-1, keepdims=True) + EPS) * g_ref[...]
    h = (h * (1.0 + _mod_row(sc_ref, row)) + _mod_row(sh_ref, row)).astype(BF16)
    off = 0
    for ref, size in zip(out_refs, _IN_SPLIT):
        ref[...] = _dot(h, w_ref[:, off:off + size]).astype(ref.dtype)
        off += size


def _inproj(x2, mod_l, g1, w_in_b, row0, rstride, t_seq):
    m, d = x2.shape
    tm = 256
    kern = functools.partial(_inproj_kernel, row0=row0, rstride=rstride, bps=t_seq // tm)
    return pl.pallas_call(
        kern,
        grid=(m // tm,),
        in_specs=[
            pl.BlockSpec((tm, d), lambda i: (i, 0)),
            pl.BlockSpec((8, d), lambda i: (0, 0)),
            pl.BlockSpec((8, d), lambda i: (0, 1)),
            pl.BlockSpec((1, d), lambda i: (0, 0)),
            pl.BlockSpec((d, N_IN), lambda i: (0, 0), pipeline_mode=pl.Buffered(1)),
        ],
        out_specs=[pl.BlockSpec((tm, s), lambda i: (i, 0)) for s in _IN_SPLIT],
        out_shape=[jax.ShapeDtypeStruct((m, s), BF16 if s == 3 * D_MODEL else F32) for s in _IN_SPLIT],
        compiler_params=_cparams(("arbitrary",)),
        name="inproj",
    )(x2, mod_l, mod_l, g1, w_in_b)


def _attn_kernel(*refs, t_seq, past, tq, latent):
    if latent:
        (q_ref, kv_ref, ck_ref, cv_ref, qg_ref, kg_ref, cq_ref, sq_ref, ckk_ref, skk_ref,
         o_ref, k_s, vt_s) = refs
    else:
        q_ref, kv_ref, qg_ref, kg_ref, o_ref, kn_ref, k_s, vt_s = refs
    lane = lax.broadcasted_iota(jnp.int32, (1, LANES), 1)
    first_half = ((lane // N_FREQ) % 2) == 0
    mean_bd = _head_block_diag(1.0 / HEAD_DIM).astype(BF16)
    q_per_kv = N_HEADS // N_KV_HEADS

    def head_norm(x, g):
        sq = x * x
        hi = sq.astype(BF16)
        lo = (sq - hi.astype(F32)).astype(BF16)
        return x * lax.rsqrt(_dot(hi, mean_bd) + _dot(lo, mean_bd) + EPS) * g

    def rope(x, cos, sin_signed):
        rot = jnp.where(first_half, pltpu.roll(x, LANES - N_FREQ, 1), pltpu.roll(x, N_FREQ, 1))
        return x * cos + rot * sin_signed

    @pl.when(pl.program_id(1) == 0)
    def _():
        rb = 256
        for i in range(past // rb):
            r0 = i * rb
            k_s[r0:r0 + rb, :] = ck_ref[0, r0:r0 + rb, :].astype(BF16)
            vt_s[:, r0:r0 + rb] = cv_ref[0, r0:r0 + rb, :].T.astype(BF16)
        for i in range(t_seq // rb):
            r0 = i * rb
            kn = head_norm(kv_ref[0, r0:r0 + rb, 0:KV_W], kg_ref[...])
            if latent:
                kn = rope(kn, ckk_ref[r0:r0 + rb, :], skk_ref[r0:r0 + rb, :])
            else:
                kn_ref[0, r0:r0 + rb, :] = kn
            k_s[past + r0:past + r0 + rb, :] = kn.astype(BF16)
            vt_s[:, past + r0:past + r0 + rb] = kv_ref[0, r0:r0 + rb, KV_W:2 * KV_W].T.astype(BF16)

    slabs = []
    for j in range(ATTN_W // LANES):
        qn = head_norm(q_ref[0, :, LANES * j:LANES * (j + 1)], qg_ref[...])
        if latent:
            qn = rope(qn, cq_ref[...], sq_ref[...])
        slabs.append(qn * (HEAD_DIM ** -0.5))
    q_t = jnp.concatenate(slabs, axis=1).T.astype(BF16)
    zero_h = jnp.zeros((HEAD_DIM, tq), BF16)
    qz = []
    for g in range(N_KV_HEADS):
        cols = []
        for h in range(q_per_kv * g, q_per_kv * (g + 1)):
            q_h = q_t[HEAD_DIM * h:HEAD_DIM * (h + 1), :]
            cols.append(jnp.concatenate([q_h, zero_h] if g == 0 else [zero_h, q_h], axis=0))
        qz.append(jnp.concatenate(cols, axis=1))
    s_t = [_dot(k_s[...], x) for x in qz]
    p_t = [jnp.exp(x - jnp.max(x, axis=0, keepdims=True)) for x in s_t]
    den = [jnp.sum(x, axis=0, keepdims=True) for x in p_t]
    o_t = [_dot(vt_s[HEAD_DIM * g:HEAD_DIM * (g + 1), :], p_t[g].astype(BF16)) / den[g]
           for g in range(N_KV_HEADS)]
    outs = [o_t[g][:, tq * h:tq * (h + 1)] for g in range(N_KV_HEADS) for h in range(q_per_kv)]
    o_ref[0] = jnp.concatenate(outs, axis=0).T.astype(o_ref.dtype)


def _attention(q, kv, qg, kg, ctx_kv, rope_tabs):
    b, t, _ = q.shape
    latent = ctx_kv is not None
    past = ctx_kv[0].shape[1] if latent else 0
    s_all = past + t
    tq = 128
    kern = functools.partial(_attn_kernel, t_seq=t, past=past, tq=tq, latent=latent)
    in_specs = [
        pl.BlockSpec((1, tq, ATTN_W), lambda bi, qi: (bi, qi, 0)),
        pl.BlockSpec((1, t, 2 * KV_W), lambda bi, qi: (bi, 0, 0)),
    ]
    args = [q, kv]
    if latent:
        in_specs += [pl.BlockSpec((1, past, KV_W), lambda bi, qi: (bi, 0, 0))] * 2
        args += list(ctx_kv)
    in_specs += [pl.BlockSpec((1, LANES), lambda bi, qi: (0, 0))] * 2
    args += [qg, kg]
    out_specs = [pl.BlockSpec((1, tq, ATTN_W), lambda bi, qi: (bi, qi, 0))]
    out_shape = [jax.ShapeDtypeStruct((b, t, ATTN_W), BF16)]
    if latent:
        cos_t, sin_t = rope_tabs
        in_specs += [pl.BlockSpec((tq, LANES), lambda bi, qi: (qi, 0))] * 2
        in_specs += [pl.BlockSpec((t, LANES), lambda bi, qi: (0, 0))] * 2
        args += [cos_t, sin_t, cos_t, sin_t]
    else:
        out_specs.append(pl.BlockSpec((1, t, KV_W), lambda bi, qi: (bi, 0, 0)))
        out_shape.append(jax.ShapeDtypeStruct((b, t, KV_W), F32))
    res = pl.pallas_call(
        kern,
        grid=(b, t // tq),
        in_specs=in_specs,
        out_specs=out_specs,
        out_shape=out_shape,
        scratch_shapes=[pltpu.VMEM((s_all, LANES), BF16), pltpu.VMEM((LANES, s_all), BF16)],
        compiler_params=_cparams(("arbitrary", "arbitrary")),
        name="attention_lat" if latent else "attention_ctx",
    )(*args)
    return res if not latent else (res[0], None)


SGU_CHUNKS_PER_STEP = 4


def _sgu_kernel(su_ref, sv_ref, g_ref, ws_ref, bs_ref, o_ref):
    for c in range(su_ref.shape[0] // CHUNK):
        rows = slice(CHUNK * c, CHUNK * (c + 1))
        v = sv_ref[rows, :]
        vn = (v * lax.rsqrt(jnp.mean(v * v, axis=-1, keepdims=True) + EPS) * g_ref[...]).astype(BF16)
        for g in range(SGU_GROUPS):
            sl = slice(LANES * g, LANES * (g + 1))
            o_ref[rows, sl] = (su_ref[rows, sl] * (_dot(ws_ref[g], vn[:, sl]) + bs_ref[g])).astype(o_ref.dtype)


def _sgu(su, sv, g, ws_b, bs_b):
    m = su.shape[0]
    tm = min(SGU_CHUNKS_PER_STEP * CHUNK, m)
    return pl.pallas_call(
        _sgu_kernel,
        grid=(m // tm,),
        in_specs=[
            pl.BlockSpec((tm, SGU_W), lambda i: (i, 0)),
            pl.BlockSpec((tm, SGU_W), lambda i: (i, 0)),
            pl.BlockSpec((1, SGU_W), lambda i: (0, 0)),
            pl.BlockSpec((SGU_GROUPS, CHUNK, CHUNK), lambda i: (0, 0, 0)),
            pl.BlockSpec((SGU_GROUPS, CHUNK, LANES), lambda i: (0, 0, 0)),
        ],
        out_specs=pl.BlockSpec((tm, SGU_W), lambda i: (i, 0)),
        out_shape=jax.ShapeDtypeStruct((m, SGU_W), BF16),
        compiler_params=_cparams(("arbitrary",)),
        name="sgu",
    )(su, sv, g, ws_b, bs_b)


def _wkv_kernel(*refs, t_seq, latent, nb):
    (r_ref, k_ref, v_ref, lora_ref, cwr_ref, cwk_ref, cwv_ref, w0_ref, a0_ref, w2_ref, a2_ref, g2_ref,
     kk_ref, ka_ref, rk_ref, lng_ref, lnb_ref) = refs[:17]
    rest = refs[17:]
    if latent:
        st0_ref, o_ref = rest[:2]
        scr = rest[2:]
        sf_ref = None
    else:
        o_ref, sf_ref = rest[:2]
        scr = rest[2:]
        st0_ref = None
    tar_s, tv_s, ov_s, arb_s, w_s, n_s, gc_s, st_s, bon_s = scr
    c_len = WKV_C
    c2 = 2 * c_len
    nc = t_seq // c_len

    lane = lax.broadcasted_iota(jnp.int32, (1, LANES), 1)
    lo = lane < R_HEAD
    sum_bd = _head_block_diag(1.0)
    mean_bd = _head_block_diag(1.0 / R_HEAD)
    row_c = lax.broadcasted_iota(jnp.int32, (c_len, LANES), 0)
    ri = lax.broadcasted_iota(jnp.int32, (c2, c2), 0)
    ci = lax.broadcasted_iota(jnp.int32, (c2, c2), 1)
    same_head = (ri // c_len) == (ci // c_len)
    tt, ss = ri % c_len, ci % c_len
    eye2 = jnp.where(ri == ci, 1.0, 0.0).astype(F32)
    strict = (same_head & (ss < tt), same_head & (ss > tt))
    incl = (same_head & (ss <= tt), same_head & (ss >= tt))
    ti = lax.broadcasted_iota(jnp.int32, (c_len, c_len), 0)
    si = lax.broadcasted_iota(jnp.int32, (c_len, c_len), 1)
    cum_incl = (jnp.where(si <= ti, 1.0, 0.0).astype(BF16), jnp.where(si >= ti, 1.0, 0.0).astype(BF16))
    sum_bd_b = sum_bd.astype(BF16)
    mean_bd_b = mean_bd.astype(BF16)

    def bdot(a, b):
        return _dot(a.astype(BF16), b.astype(BF16))

    def split2(x):
        hi = x.astype(BF16)
        return hi, (x - hi.astype(F32)).astype(BF16)

    def head_reduce(x, bd_b):
        hi, lo = split2(x)
        return _dot(hi, bd_b) + _dot(lo, bd_b)

    def stack_heads(x):
        return jnp.concatenate([jnp.where(lo, x, 0.0), jnp.where(lo, 0.0, x)], axis=0)

    def fold_heads(x):
        return x[0:c_len] + x[c_len:c2]

    def slot(bi, c):
        return bi * nc + c

    def conv_tokens(ref, w_ref, bi, c, t0):
        x = ref[bi, pl.ds(t0, c_len), :]
        if isinstance(c, int):
            zero_row = jnp.zeros((1, LANES), F32)
            prev = ref[bi, pl.ds(t0 - 1, 1), :] if c > 0 else zero_row
            nxt = ref[bi, pl.ds(t0 + c_len, 1), :] if c < nc - 1 else zero_row
        else:
            prev = ref[bi, pl.ds(jnp.maximum(t0 - 1, 0), 1), :]
            nxt = ref[bi, pl.ds(jnp.minimum(t0 + c_len, t_seq - 1), 1), :]
            prev = jnp.where(c > 0, prev, 0.0)
            nxt = jnp.where(c < nc - 1, nxt, 0.0)
        xm = jnp.where(row_c == 0, prev, pltpu.roll(x, 1, 0))
        xp = jnp.where(row_c == c_len - 1, nxt, pltpu.roll(x, c_len - 1, 0))
        return xm * w_ref[0:1, :] + x * w_ref[1:2, :] + xp * w_ref[2:3, :]

    def chunk_group(chunks):
        n = len(chunks)
        t0s = [c * c_len if isinstance(c, int) else pl.multiple_of(c * c_len, c_len) for _, c in chunks]
        rc = [conv_tokens(r_ref, cwr_ref, bi, c, t0) for (bi, c), t0 in zip(chunks, t0s)]
        kc = [conv_tokens(k_ref, cwk_ref, bi, c, t0) for (bi, c), t0 in zip(chunks, t0s)]
        vc = [conv_tokens(v_ref, cwv_ref, bi, c, t0) for (bi, c), t0 in zip(chunks, t0s)]
        kkr = [k * kk_ref[...] for k in kc]
        kk = [x * lax.rsqrt(head_reduce(x * x, sum_bd_b) + 1e-12) for x in kkr]
        tw = [jnp.tanh(lora_ref[bi, pl.ds(t0, c_len), 0:2 * LORA_W]) for (bi, _), t0 in zip(chunks, t0s)]
        xa = [lora_ref[bi, pl.ds(t0, c_len), 2 * LORA_W:2 * LORA_W + 2 * LORA_A] for (bi, _), t0 in zip(chunks, t0s)]
        v_hs = [stack_heads(v).astype(BF16) for v in vc]
        prob =[(i, d) for i in range(n) for d in range(2)]
        dsel = (lo, jnp.logical_not(lo))
        lw = [-DECAY_SCALE * _sigmoid(w0_ref[d:d + 1, :] + _dot(jnp.where(dsel[d], tw[i], 0.0).astype(BF16), w2_ref[...]))
              for i, d in prob]
        ag = [_sigmoid(a0_ref[d:d + 1, :] + _dot(jnp.where(dsel[d], xa[i], 0.0).astype(BF16), a2_ref[...]))
              for i, d in prob]
        kd = [kc[i] * (1.0 + (ag[p] - 1.0) * ka_ref[...]) for p, (i, d) in enumerate(prob)]
        kb = [kk[i] * ag[p] for p, (i, d) in enumerate(prob)]
        bon = [head_reduce(rc[i] * kd[p] * rk_ref[...], sum_bd_b) * vc[i] for p, (i, d) in enumerate(prob)]
        for i in range(n):
            bon_s[chunks[i][0], pl.ds(t0s[i], c_len), :] = bon[2 * i] + bon[2 * i + 1]
        lw_sp = [split2(x) for x in lw]
        cs_in = [_dot(cum_incl[d], lw_sp[p][0]) + _dot(cum_incl[d], lw_sp[p][1]) for p, (i, d) in enumerate(prob)]
        tot = [cs_in[p][c_len - 1:c_len, :] if d == 0 else cs_in[p][0:1, :] for p, (i, d) in enumerate(prob)]
        e_neg = [jnp.exp(-x) for x in cs_in]
        e_rem = [jnp.exp(tot[p] - cs_in[p]) for p in range(len(prob))]
        a_t = [-kk[i] * jnp.exp(cs_in[p] - (lw_sp[p][0].astype(F32) + lw_sp[p][1].astype(F32)))
               for p, (i, d) in enumerate(prob)]
        r_t = [rc[i] * jnp.exp(cs_in[p]) for p, (i, d) in enumerate(prob)]
        a_hs = [stack_heads(x).astype(BF16) for x in a_t]
        lhs = [jnp.concatenate([a_hs[p], stack_heads(r_t[p]).astype(BF16)], axis=0) for p in range(len(prob))]
        rhs = [jnp.concatenate([stack_heads(kd[p] * e_neg[p]), stack_heads(kb[p] * e_neg[p])], axis=0).astype(BF16)
               for p in range(len(prob))]
        m4 = [_dot_nt(a, b) for a, b in zip(lhs, rhs)]
        a_ak = [jnp.where(strict[d], m4[p][0:c2, 0:c2], 0.0).astype(BF16) for p, (i, d) in enumerate(prob)]
        l_ab = [jnp.where(strict[d], m4[p][0:c2, c2:2 * c2], 0.0) for p, (i, d) in enumerate(prob)]
        a_rk = [jnp.where(incl[d], m4[p][c2:2 * c2, 0:c2], 0.0).astype(BF16) for p, (i, d) in enumerate(prob)]
        for p, (i, d) in enumerate(prob):
            arb_s[d, slot(*chunks[i])] = jnp.where(incl[d], m4[p][c2:2 * c2, c2:2 * c2], 0.0).astype(BF16)
        t_inv = [eye2 + x for x in l_ab]
        pw_b = [x.astype(BF16) for x in l_ab]
        pw_b = [_dot(x, x).astype(BF16) for x in pw_b]
        for _ in range(4):
            res = [_dot(x, jnp.concatenate([x, t.astype(BF16)], axis=1)) for x, t in zip(pw_b, t_inv)]
            pw_b = [r[:, 0:c2].astype(BF16) for r in res]
            t_inv = [t + r[:, c2:2 * c2] for t, r in zip(t_inv, res)]
        t_inv = [t + _dot(x, t.astype(BF16)) for x, t in zip(pw_b, t_inv)]
        t_b = [t.astype(BF16) for t in t_inv]
        av = [_dot(a_ak[p], v_hs[i]).astype(BF16) for p, (i, d) in enumerate(prob)]
        ov = [fold_heads(_dot(a_rk[p], v_hs[i])) for p, (i, d) in enumerate(prob)]
        tt = [_dot(t, jnp.concatenate([a, x], axis=1)) for t, a, x in zip(t_b, a_hs, av)]
        ta = [fold_heads(r[:, 0:LANES]) for r in tt]
        tv = [fold_heads(r[:, LANES:2 * LANES]) for r in tt]
        kbt = [jnp.concatenate([kd[p] * e_rem[p], kb[p] * e_rem[p]], axis=0).T.astype(BF16) for p in range(len(prob))]
        zero_c = jnp.zeros((c_len, LANES), BF16)
        upd_rhs = [jnp.concatenate([jnp.concatenate([vc[i].astype(BF16), zero_c], axis=1),
                                    jnp.concatenate([tv[p].astype(BF16), ta[p].astype(BF16)], axis=1)], axis=0)
                   for p, (i, d) in enumerate(prob)]
        nw = [_dot(a, b) for a, b in zip(kbt, upd_rhs)]
        for p, (i, d) in enumerate(prob):
            c = slot(*chunks[i])
            tar_s[d, c] = jnp.concatenate([ta[p], r_t[p]], axis=0).astype(BF16)
            tv_s[d, c] = tv[p]
            ov_s[d, c] = ov[p]
            n_s[d, c] = jnp.where(bd_mask, nw[p][:, 0:LANES], 0.0)
            w_s[d, c] = jnp.where(bd_mask, nw[p][:, LANES:2 * LANES], 0.0).astype(BF16)
            gc_s[d, c] = jnp.broadcast_to(jnp.exp(tot[p]), (LANES, LANES)).T

    bd_mask = sum_bd > 0.5
    group1 = 8
    if nb * nc <= group1:
        chunk_group([(bi, c) for bi in range(nb) for c in range(nc)])
    else:
        assert nb == 1 and nc % group1 == 0

        def phase1(i, carry):
            chunk_group([(0, i * group1 + j) for j in range(group1)])
            return carry

        lax.fori_loop(0, nc // group1, phase1, 0)

    chains = [(bi, d) for bi in range(nb) for d in range(2)]

    def phase2(i, carry):
        cs = [slot(bi, i if d == 0 else nc - 1 - i) for bi, d in chains]
        st_b = [x.astype(BF16) for x in carry]
        for (bi, d), c, x in zip(chains, cs, st_b):
            st_s[d, c] = x
        prod = [_dot(w_s[d, c], x) for (bi, d), c, x in zip(chains, cs, st_b)]
        return tuple(gc_s[d, c] * s + pr + n_s[d, c] for (bi, d), c, s, pr in zip(chains, cs, carry, prod))

    if latent:
        init = tuple(st0_ref[bi, d, 0] for bi, d in chains)
    else:
        init = (jnp.zeros((LANES, LANES), F32),) * len(chains)
    fin = lax.fori_loop(0, nc, phase2, init)
    if sf_ref is not None:
        for (bi, d), st in zip(chains, fin):
            s_vk = st.T
            sf_ref[bi, d, 0] = s_vk[0:R_HEAD, 0:R_HEAD]
            sf_ref[bi, d, 1] = pltpu.roll(s_vk[R_HEAD:2 * R_HEAD, :], R_HEAD, 1)[:, 0:R_HEAD]

    def out_group(chunks):
        prob = [(bi, c, slot(bi, c), d) for bi, c in chunks for d in range(2)]
        both = [_dot(tar_s[d, s], st_s[d, s]) for _, _, s, d in prob]
        u = [both[p][0:c_len] + tv_s[d, s] for p, (_, _, s, d) in enumerate(prob)]
        ou = [_dot(arb_s[d, s], stack_heads(u[p]).astype(BF16)) for p, (_, _, s, d) in enumerate(prob)]
        o = [ov_s[d, s] + both[p][c_len:c2] + fold_heads(ou[p]) for p, (_, _, s, d) in enumerate(prob)]
        of = [o[2 * j] + o[2 * j + 1] for j in range(len(chunks))]
        xc = [x - head_reduce(x, mean_bd_b) for x in of]
        var = [head_reduce(x * x, mean_bd_b) for x in xc]
        for j, (bi, c) in enumerate(chunks):
            t0 = c * c_len if isinstance(c, int) else pl.multiple_of(c * c_len, c_len)
            on = xc[j] * lax.rsqrt(var[j] + GN_EPS) * lng_ref[...] + lnb_ref[...]
            xg = lora_ref[bi, pl.ds(t0, c_len), 2 * LORA_W + 2 * LORA_A:LORA_ALL]
            g_out = _dot(_sigmoid(xg).astype(BF16), g2_ref[...])
            o_ref[bi, pl.ds(t0, c_len), :] = ((on + bon_s[bi, pl.ds(t0, c_len), :]) * g_out).astype(o_ref.dtype)

    group3 = 4
    if nc <= group3:
        for bi in range(nb):
            out_group([(bi, c) for c in range(nc)])
    else:
        assert nb == 1 and nc % group3 == 0

        def phase3(i, carry):
            out_group([(0, i * group3 + j) for j in range(group3)])
            return carry

        lax.fori_loop(0, nc // group3, phase3, 0)


def _wkv(rkv, lora, wts, st0):
    b, t, _ = rkv.shape
    latent = st0 is not None
    nc = t // WKV_C
    npair = RW // LANES
    nb = 2 if (2 * nc <= 8 and b % 2 == 0) else 1
    kern = functools.partial(_wkv_kernel, t_seq=t, latent=latent, nb=nb)

    def seq_spec(col0):
        return pl.BlockSpec((nb, t, LANES), lambda bi, p: (bi, 0, col0 + p))

    def row_spec(rows, col0=0):
        return pl.BlockSpec((rows, LANES), lambda bi, p: (0, col0 + p))

    in_specs = [
        seq_spec(0), seq_spec(npair), seq_spec(2 * npair),
        pl.BlockSpec((nb, t, LORA_ALL), lambda bi, p: (bi, 0, 0)),
        row_spec(3, 0), row_spec(3, npair), row_spec(3, 2 * npair),
        row_spec(2), row_spec(2), row_spec(LANES), row_spec(LANES), row_spec(LANES),
        row_spec(1), row_spec(1), row_spec(1), row_spec(1), row_spec(1),
    ]
    args = [rkv, rkv, rkv, lora, wts["conv"], wts["conv"], wts["conv"], wts["w0"], wts["a0"], wts["w2"], wts["a2"],
            wts["g2"], wts["k_k"], wts["k_a"], wts["r_k"], wts["ln_g"], wts["ln_b"]]
    out_specs = [seq_spec(0)]
    out_shape = [jax.ShapeDtypeStruct((b, t, RW), BF16)]
    if latent:
        in_specs.append(pl.BlockSpec((nb, 2, 1, LANES, LANES), lambda bi, p: (bi, 0, p, 0, 0)))
        args.append(st0)
    else:
        out_specs.append(pl.BlockSpec((nb, 2, 2, R_HEAD, R_HEAD), lambda bi, p: (bi, 0, p, 0, 0)))
        out_shape.append(jax.ShapeDtypeStruct((b, 2, R_HEADS, R_HEAD, R_HEAD), F32))
    chunk_rows = pltpu.VMEM((2, nb * nc, WKV_C, LANES), F32)
    chunk_sq = pltpu.VMEM((2, nb * nc, LANES, LANES), F32)
    chunk_sq_b = pltpu.VMEM((2, nb * nc, LANES, LANES), BF16)
    res = pl.pallas_call(
        kern,
        grid=(b // nb, npair),
        in_specs=in_specs,
        out_specs=out_specs,
        out_shape=out_shape,
        scratch_shapes=[chunk_sq_b, chunk_rows, chunk_rows, chunk_sq_b, chunk_sq_b, chunk_sq, chunk_sq, chunk_sq_b,
                        pltpu.VMEM((nb, t, LANES), F32)],
        compiler_params=_cparams(("arbitrary", "arbitrary")),
        name="wkv_lat" if latent else "wkv_ctx",
    )(*args)
    return (res[0], None) if latent else (res[0], res[1])


def _merge_kernel(x_ref, oa_ref, os_ref, or_ref, gates_ref, gt_ref, pa_ref, ps_ref, pr_ref, wo_ref, o_ref,
                  *, row0, rstride, bps):
    row = row0 + (pl.program_id(0) // bps) * rstride if rstride else row0
    d = D_MODEL
    gate = lambda j: _sigmoid(gates_ref[:, j * d:(j + 1) * d].astype(F32))
    merged = (gate(0) * _dot(oa_ref[...].astype(BF16), pa_ref[...])
              + gate(1) * _dot(os_ref[...].astype(BF16), ps_ref[...])
              + gate(2) * _dot(or_ref[...].astype(BF16), pr_ref[...]))
    o_ref[...] = x_ref[...] + _mod_row(gt_ref, row) * _dot(merged.astype(BF16), wo_ref[...])


def _merge(x2, oa, osg, orw, gates, mod_l, pa, ps, pr, wo, row0, rstride, t_seq):
    m, d = x2.shape
    tm = 256
    kern = functools.partial(_merge_kernel, row0=row0, rstride=rstride, bps=t_seq // tm)
    tok = lambda w: pl.BlockSpec((tm, w), lambda i: (i, 0))
    full = lambda a: pl.BlockSpec(a.shape, lambda i: (0, 0))
    return pl.pallas_call(
        kern,
        grid=(m // tm,),
        in_specs=[tok(d), tok(ATTN_W), tok(SGU_W), tok(RW), tok(3 * d),
                  pl.BlockSpec((8, d), lambda i: (0, 2)),
                  full(pa), full(ps), full(pr), full(wo)],
        out_specs=tok(d),
        out_shape=jax.ShapeDtypeStruct((m, d), F32),
        compiler_params=_cparams(("arbitrary",)),
        name="merge",
    )(x2, oa, osg, orw, gates, mod_l, pa, ps, pr, wo)


def _moe_kernel(x_ref, sh_ref, sc_ref, gt_ref, g2_ref, rw_ref, rb_ref, shg_ref, shu_ref, shd_ref,
                wg_ref, wu_ref, wd_ref, o_ref, h_s, comb_s, acc_s, *, row0, rstride, bps):
    row = row0 + (pl.program_id(0) // bps) * rstride if rstride else row0
    e = pl.program_id(1)
    tm = x_ref.shape[0]
    lane = lax.broadcasted_iota(jnp.int32, (tm, LANES), 1).astype(F32)

    @pl.when(e == 0)
    def _():
        x = x_ref[...]
        h = x * lax.rsqrt(jnp.mean(x * x, axis=-1, keepdims=True) + EPS) * g2_ref[...]
        h = h * (1.0 + _mod_row(sc_ref, row)) + _mod_row(sh_ref, row)
        scores = _sigmoid(_dot(h, rw_ref[...], HI))
        valid = lane < N_EXPERTS
        sel = jnp.where(valid, scores + rb_ref[...], -jnp.inf)
        picked = jnp.zeros((tm, LANES), jnp.bool_)
        for _ in range(TOP_K):
            best = jnp.max(sel, axis=-1, keepdims=True)
            first = jnp.min(jnp.where(sel == best, lane, float(LANES)), axis=-1, keepdims=True)
            hit = lane == first
            picked = jnp.logical_or(picked, hit)
            sel = jnp.where(hit, -jnp.inf, sel)
        sw = jnp.where(picked, scores, 0.0)
        comb = sw / jnp.sum(sw, axis=-1, keepdims=True) * ROUTED_SCALE
        for b in range(N_EXPERTS // MOE_EB):
            comb_s[b] = comb if b == 0 else pltpu.roll(comb, LANES - MOE_EB * b, 1)
        hb = h.astype(BF16)
        h_s[...] = hb
        acc_s[...] = _dot((_silu(_dot(hb, shg_ref[...])) * _dot(hb, shu_ref[...])).astype(BF16), shd_ref[...])

    hb = h_s[...]
    comb = comb_s[e]
    hid = []
    for j in range(MOE_EB):
        gu = _dot(hb, jnp.concatenate([wg_ref[j], wu_ref[j]], axis=1))
        cw = jnp.broadcast_to(comb[:, j:j + 1], (tm, D_EXPERT))
        hid.append((_silu(gu[:, 0:D_EXPERT]) * gu[:, D_EXPERT:2 * D_EXPERT] * cw).astype(BF16))
    acc_s[...] += _dot(jnp.concatenate(hid, axis=1), wd_ref[...])

    @pl.when(e == pl.num_programs(1) - 1)
    def _():
        o_ref[...] = x_ref[...] + _mod_row(gt_ref, row) * acc_s[...]


def _moe(x2, mod_l, g2, rw_p, rb_p, shg, shu, shd, wg, wu, wd, row0, rstride, t_seq):
    m, d = x2.shape
    tm = min(1024, t_seq if rstride else m)
    n_hid = MOE_EB * D_EXPERT
    kern = functools.partial(_moe_kernel, row0=row0, rstride=rstride, bps=max(t_seq // tm, 1))
    full = lambda a: pl.BlockSpec(a.shape, lambda i, e: (0, 0))
    return pl.pallas_call(
        kern,
        grid=(m // tm, N_EXPERTS // MOE_EB),
        in_specs=[
            pl.BlockSpec((tm, d), lambda i, e: (i, 0)),
            pl.BlockSpec((8, d), lambda i, e: (0, 3)),
            pl.BlockSpec((8, d), lambda i, e: (0, 4)),
            pl.BlockSpec((8, d), lambda i, e: (0, 5)),
            full(g2), full(rw_p), full(rb_p), full(shg), full(shu), full(shd),
            pl.BlockSpec((MOE_EB, d, D_EXPERT), lambda i, e: (e, 0, 0)),
            pl.BlockSpec((MOE_EB, d, D_EXPERT), lambda i, e: (e, 0, 0)),
            pl.BlockSpec((n_hid, d), lambda i, e: (e, 0)),
        ],
        out_specs=pl.BlockSpec((tm, d), lambda i, e: (i, 0)),
        out_shape=jax.ShapeDtypeStruct((m, d), F32),
        scratch_shapes=[pltpu.VMEM((tm, d), BF16), pltpu.VMEM((N_EXPERTS // MOE_EB, tm, LANES), F32),
                        pltpu.VMEM((tm, d), F32)],
        compiler_params=_cparams(("arbitrary", "arbitrary")),
        name="moe",
    )(x2, mod_l, mod_l, mod_l, g2, rw_p, rb_p, shg, shu, shd, wg, wu, wd)


def _layer(x, mod_l, w, row0, rstride, ctx, rope_tabs):
    b, t, d = x.shape
    x2 = x.reshape(b * t, d)
    q, kv, su, sv, rkv, lora, gates = _inproj(x2, mod_l, w["norm1_g"], w["w_in"], row0, rstride, t)
    if ctx is None:
        o_attn, k_new = _attention(q.reshape(b, t, -1), kv.reshape(b, t, -1), w["q_norm"], w["k_norm"], None, None)
        st0 = None
    else:
        o_attn, k_new = _attention(q.reshape(b, t, -1), kv.reshape(b, t, -1), w["q_norm"], w["k_norm"],
                                   ctx[:2], rope_tabs)
        st0 = ctx[2]
    o_sgu = _sgu(su, sv, w["sgu_norm_g"], w["sgu_ws"], w["sgu_bs"])
    o_wkv, s_fin = _wkv(rkv.reshape(b, t, -1), lora.reshape(b, t, -1), w["wkv"], st0)
    x1 = _merge(x2, o_attn.reshape(b * t, -1), o_sgu, o_wkv.reshape(b * t, -1), gates, mod_l,
                w["proj_attn"], w["proj_sgu"], w["proj_rwkv"], w["w_out"], row0, rstride, t)
    x_out = _moe(x1, mod_l, w["norm2_g"], w["router_w"], w["router_bias"], w["sh_gate"], w["sh_up"], w["sh_down"],
                 w["exp_gate"], w["exp_up"], w["exp_down"], row0, rstride, t)
    return x_out.reshape(b, t, d), (k_new, kv, s_fin)


def _rope_tables(n_tok):
    t = jnp.arange(n_tok)
    row = (t // GRID_W).astype(F32)
    col = (t % GRID_W).astype(F32)
    inv = ROPE_THETA ** (-jnp.arange(N_FREQ, dtype=F32) / N_FREQ)
    ang = jnp.stack([row[:, None] * inv[None, :], col[:, None] * inv[None, :]], axis=1)
    cos, sin = jnp.cos(ang), jnp.sin(ang)
    cos64 = jnp.stack([cos, cos], axis=2).reshape(n_tok, HEAD_DIM)
    sin64 = jnp.stack([-sin, sin], axis=2).reshape(n_tok, HEAD_DIM)
    return jnp.tile(cos64, (1, LANES // HEAD_DIM)), jnp.tile(sin64, (1, LANES // HEAD_DIM))


def _layer_weights(l, mod_w, mod_b, norm1_g, norm2_g, w_in, q_norm, k_norm, sgu_norm_g, sgu_ws, sgu_bs, rwkv_conv,
                   rwkv_w0, rwkv_w2, rwkv_a0, rwkv_a2, rwkv_g2, rwkv_k_k, rwkv_k_a, rwkv_r_k, rwkv_ln_g, rwkv_ln_b,
                   proj_attn, proj_sgu, proj_rwkv, w_out, router_w, router_bias, exp_gate, exp_up, exp_down,
                   sh_gate, sh_up, sh_down):
    d = D_MODEL
    row = lambda a: a.reshape(1, -1)
    pad_e = LANES - N_EXPERTS
    return {
        "norm1_g": row(norm1_g[l]), "norm2_g": row(norm2_g[l]),
        "w_in": w_in[l].astype(BF16),
        "q_norm": jnp.tile(row(q_norm[l]), (1, LANES // HEAD_DIM)),
        "k_norm": jnp.tile(row(k_norm[l]), (1, LANES // HEAD_DIM)),
        "sgu_norm_g": row(sgu_norm_g[l]),
        "sgu_ws": sgu_ws[l].astype(BF16),
        "sgu_bs": jnp.broadcast_to(sgu_bs[l][:, :, None], (SGU_GROUPS, CHUNK, LANES)),
        "wkv": {
            "conv": rwkv_conv[l],
            "w0": rwkv_w0[l], "a0": rwkv_a0[l],
            "w2": rwkv_w2[l].reshape(2 * LORA_W, RW).astype(BF16),
            "a2": rwkv_a2[l].reshape(2 * LORA_A, RW).astype(BF16),
            "g2": rwkv_g2[l].astype(BF16),
            "k_k": row(rwkv_k_k[l]), "k_a": row(rwkv_k_a[l]), "r_k": row(rwkv_r_k[l]),
            "ln_g": row(rwkv_ln_g[l]), "ln_b": row(rwkv_ln_b[l]),
        },
        "proj_attn": proj_attn[l].astype(BF16), "proj_sgu": proj_sgu[l].astype(BF16),
        "proj_rwkv": proj_rwkv[l].astype(BF16), "w_out": w_out[l].astype(BF16),
        "router_w": jnp.pad(router_w[l], ((0, 0), (0, pad_e))),
        "router_bias": jnp.pad(row(router_bias[l]), ((0, 0), (0, pad_e))),
        "sh_gate": sh_gate[l].astype(BF16), "sh_up": sh_up[l].astype(BF16), "sh_down": sh_down[l].astype(BF16),
        "exp_gate": exp_gate[l].astype(BF16), "exp_up": exp_up[l].astype(BF16),
        "exp_down": exp_down[l].astype(BF16).reshape(N_EXPERTS * D_EXPERT, d),
    }


def _state_to_blockdiag_t(s):
    b = s.shape[0]
    st = jnp.swapaxes(s, -1, -2).reshape(b, 2, R_HEADS // 2, 2, R_HEAD, R_HEAD)
    bd = jnp.einsum("bdpjkv,ji->bdpjkiv", st, jnp.eye(2, dtype=s.dtype))
    return bd.reshape(b, 2, R_HEADS // 2, LANES, LANES)


def kernel(x_prompt, x_sample, cache_k, cache_v, state_wkv, c, c_ctx, mod_w, mod_b, norm1_g, norm2_g, w_in, q_norm, k_norm, sgu_norm_g, sgu_ws, sgu_bs, rwkv_conv, rwkv_w0, rwkv_w2, rwkv_a0, rwkv_a2, rwkv_g2, rwkv_k_k, rwkv_k_a, rwkv_r_k, rwkv_ln_g, rwkv_ln_b, proj_attn, proj_sgu, proj_rwkv, w_out, router_w, router_bias, exp_gate, exp_up, exp_down, sh_gate, sh_up, sh_down):
    params = (mod_w, mod_b, norm1_g, norm2_g, w_in, q_norm, k_norm, sgu_norm_g, sgu_ws, sgu_bs, rwkv_conv,
              rwkv_w0, rwkv_w2, rwkv_a0, rwkv_a2, rwkv_g2, rwkv_k_k, rwkv_k_a, rwkv_r_k, rwkv_ln_g, rwkv_ln_b,
              proj_attn, proj_sgu, proj_rwkv, w_out, router_w, router_bias, exp_gate, exp_up, exp_down,
              sh_gate, sh_up, sh_down)
    n_b, n_t, d = x_prompt.shape
    n_db, n_dt, _ = x_sample.shape
    assert n_db + 1 <= 8
    cmat = jnp.zeros((8, d), F32).at[0].set(c_ctx).at[1:1 + n_db].set(c)
    mod = _modulation(cmat, mod_w, mod_b)
    rope_tabs = _rope_tables(n_dt)
    xp, xs = x_prompt, x_sample
    ks, vs, ss = [], [], []
    for l in range(DEPTH):
        w = _layer_weights(l, *params)
        xp, (k_new, kv, s_fin) = _layer(xp, mod[l], w, 0, 0, None, None)
        ks.append(k_new.reshape(n_b, n_t, N_KV_HEADS, HEAD_DIM))
        vs.append(kv.reshape(n_b, n_t, 2 * KV_W)[:, :, KV_W:].reshape(n_b, n_t, N_KV_HEADS, HEAD_DIM))
        ss.append(s_fin)
        ctx = (cache_k[:, l].reshape(n_db, -1, KV_W), cache_v[:, l].reshape(n_db, -1, KV_W),
               _state_to_blockdiag_t(state_wkv[:, l]))
        xs, _ = _layer(xs, mod[l], w, 1, 1, ctx, rope_tabs)
    return (xp, xs, jnp.stack(ks, axis=1), jnp.stack(vs, axis=1), jnp.stack(ss, axis=1))
```

```python
import functools

import jax
import jax.numpy as jnp
from jax import lax
from jax.experimental import pallas as pl
from jax.experimental.pallas import tpu as pltpu

F32 = jnp.float32
BF16 = jnp.bfloat16
HI = lax.Precision.HIGHEST

D_MODEL = 1024
DEPTH = 2
GRID_W = 64
N_HEADS = 8
N_KV_HEADS = 2
HEAD_DIM = 64
ATTN_W = N_HEADS * HEAD_DIM
KV_W = N_KV_HEADS * HEAD_DIM
N_FREQ = HEAD_DIM // 4
ROPE_THETA = 10000.0
CHUNK = 128
SGU_GROUPS = 4
SGU_W = 512
R_HEADS = 8
R_HEAD = 64
RW = R_HEADS * R_HEAD
LORA_W = 64
LORA_A = 64
LORA_G = 128
DECAY_SCALE = 0.606531
GN_EPS = 64e-5
N_EXPERTS = 64
TOP_K = 6
D_EXPERT = 128
D_SHARED = 256
ROUTED_SCALE = 2.5
EPS = 1e-6
LORA_ALL = 2 * LORA_W + 2 * LORA_A + LORA_G
N_IN = ATTN_W + 2 * KV_W + 2 * SGU_W + 3 * RW + LORA_ALL + 3 * D_MODEL

LANES = 128
WKV_C = 64
MOE_EB = 8
VMEM_LIMIT = 56 * 1024 * 1024


def _cparams(sem, vmem=VMEM_LIMIT):
    return pltpu.CompilerParams(dimension_semantics=sem, vmem_limit_bytes=vmem)


def _dot(a, b, prec=None):
    return jnp.dot(a, b, preferred_element_type=F32, precision=prec)


def _dot_nt(a, b, prec=None):
    return lax.dot_general(a, b, (((1,), (1,)), ((), ())), preferred_element_type=F32, precision=prec)


def _sigmoid(x):
    return 1.0 / (1.0 + jnp.exp(-x))


def _silu(x):
    return x / (1.0 + jnp.exp(-x))


def _head_block_diag(value):
    r = lax.broadcasted_iota(jnp.int32, (LANES, LANES), 0) // HEAD_DIM
    c = lax.broadcasted_iota(jnp.int32, (LANES, LANES), 1) // HEAD_DIM
    return jnp.where(r == c, value, 0.0).astype(F32)


def _mod_row(ref, row):
    return ref[pl.ds(row, 1), :]


def _mod_kernel(c_ref, w_ref, b_ref, o_ref):
    o_ref[0] = _dot(_silu(c_ref[...]), w_ref[0], HI) + b_ref[0]


def _modulation(cmat, mod_w, mod_b):
    n_l, d, n6 = mod_w.shape
    tn = n6 // 4
    return pl.pallas_call(
        _mod_kernel,
        grid=(n_l, n6 // tn),
        in_specs=[
            pl.BlockSpec((8, d), lambda l, j: (0, 0)),
            pl.BlockSpec((1, d, tn), lambda l, j: (l, 0, j)),
            pl.BlockSpec((1, 1, tn), lambda l, j: (l, 0, j)),
        ],
        out_specs=pl.BlockSpec((1, 8, tn), lambda l, j: (l, 0, j)),
        out_shape=jax.ShapeDtypeStruct((n_l, 8, n6), F32),
        compiler_params=_cparams(("arbitrary", "arbitrary")),
        name="modulation",
    )(cmat, mod_w, mod_b.reshape(n_l, 1, n6))


_IN_SPLIT = (ATTN_W, 2 * KV_W, SGU_W, SGU_W, 3 * RW, LORA_ALL, 3 * D_MODEL)


def _inproj_kernel(x_ref, sh_ref, sc_ref, g_ref, w_ref, *out_refs, row0, rstride, bps):
    row = row0 + (pl.program_id(0) // bps) * rstride if rstride else row0
    x = x_ref[...]
    h = x * lax.rsqrt(jnp.mean(x * x, axis=-1, keepdims=True) + EPS) * g_ref[...]
    h = (h * (1.0 + _mod_row(sc_ref, row)) + _mod_row(sh_ref, row)).astype(BF16)
    off = 0
    for ref, size in zip(out_refs, _IN_SPLIT):
        ref[...] = _dot(h, w_ref[:, off:off + size]).astype(ref.dtype)
        off += size


def _inproj(x2, mod_l, g1, w_in_layers, row0, rstride, t_seq):
    m, d = x2.shape
    w_in_b, layer = w_in_layers
    tm = 256
    kern = functools.partial(_inproj_kernel, row0=row0, rstride=rstride, bps=t_seq // tm)
    return pl.pallas_call(
        kern,
        grid=(m // tm,),
        in_specs=[
            pl.BlockSpec((tm, d), lambda i: (i, 0)),
            pl.BlockSpec((8, d), lambda i: (0, 0)),
            pl.BlockSpec((8, d), lambda i: (0, 1)),
            pl.BlockSpec((1, d), lambda i: (0, 0)),
            pl.BlockSpec((None, d, N_IN), lambda i: (layer, 0, 0), pipeline_mode=pl.Buffered(1)),
        ],
        out_specs=[pl.BlockSpec((tm, s), lambda i: (i, 0)) for s in _IN_SPLIT],
        out_shape=[jax.ShapeDtypeStruct((m, s), BF16 if s == 3 * D_MODEL else F32) for s in _IN_SPLIT],
        compiler_params=_cparams(("arbitrary",)),
        name="inproj",
    )(x2, mod_l, mod_l, g1, w_in_b)


def _attn_kernel(*refs, t_seq, past, tq, latent):
    if latent:
        (q_ref, kv_ref, ck_ref, cv_ref, qg_ref, kg_ref, cq_ref, sq_ref, ckk_ref, skk_ref,
         o_ref, k_s, vt_s) = refs
    else:
        q_ref, kv_ref, qg_ref, kg_ref, o_ref, kn_ref, k_s, vt_s = refs
    lane = lax.broadcasted_iota(jnp.int32, (1, LANES), 1)
    first_half = ((lane // N_FREQ) % 2) == 0
    mean_bd = _head_block_diag(1.0 / HEAD_DIM).astype(BF16)
    q_per_kv = N_HEADS // N_KV_HEADS

    def head_norm(x, g):
        sq = x * x
        hi = sq.astype(BF16)
        lo = (sq - hi.astype(F32)).astype(BF16)
        return x * lax.rsqrt(_dot(hi, mean_bd) + _dot(lo, mean_bd) + EPS) * g

    def rope(x, cos, sin_signed):
        rot = jnp.where(first_half, pltpu.roll(x, LANES - N_FREQ, 1), pltpu.roll(x, N_FREQ, 1))
        return x * cos + rot * sin_signed

    @pl.when(pl.program_id(1) == 0)
    def _():
        rb = 256
        for i in range(past // rb):
            r0 = i * rb
            k_s[r0:r0 + rb, :] = ck_ref[0, r0:r0 + rb, :].astype(BF16)
            vt_s[:, r0:r0 + rb] = cv_ref[0, r0:r0 + rb, :].T.astype(BF16)
        for i in range(t_seq // rb):
            r0 = i * rb
            kn = head_norm(kv_ref[0, r0:r0 + rb, 0:KV_W], kg_ref[...])
            if latent:
                kn = rope(kn, ckk_ref[r0:r0 + rb, :], skk_ref[r0:r0 + rb, :])
            else:
                kn_ref[0, r0:r0 + rb, :] = kn
            k_s[past + r0:past + r0 + rb, :] = kn.astype(BF16)
            vt_s[:, past + r0:past + r0 + rb] = kv_ref[0, r0:r0 + rb, KV_W:2 * KV_W].T.astype(BF16)

    slabs = []
    for j in range(ATTN_W // LANES):
        qn = head_norm(q_ref[0, :, LANES * j:LANES * (j + 1)], qg_ref[...])
        if latent:
            qn = rope(qn, cq_ref[...], sq_ref[...])
        slabs.append(qn * (HEAD_DIM ** -0.5))
    q_t = jnp.concatenate(slabs, axis=1).T.astype(BF16)
    zero_h = jnp.zeros((HEAD_DIM, tq), BF16)
    qz = []
    for g in range(N_KV_HEADS):
        cols = []
        for h in range(q_per_kv * g, q_per_kv * (g + 1)):
            q_h = q_t[HEAD_DIM * h:HEAD_DIM * (h + 1), :]
            cols.append(jnp.concatenate([q_h, zero_h] if g == 0 else [zero_h, q_h], axis=0))
        qz.append(jnp.concatenate(cols, axis=1))
    s_t = [_dot(k_s[...], x) for x in qz]
    p_t = [jnp.exp(x - jnp.max(x, axis=0, keepdims=True)) for x in s_t]
    den = [jnp.sum(x, axis=0, keepdims=True) for x in p_t]
    o_t = [_dot(vt_s[HEAD_DIM * g:HEAD_DIM * (g + 1), :], p_t[g].astype(BF16)) / den[g]
           for g in range(N_KV_HEADS)]
    outs =[o_t[g][:, tq * h:tq * (h + 1)] for g in range(N_KV_HEADS) for h in range(q_per_kv)]
    o_ref[0] = jnp.concatenate(outs, axis=0).T.astype(o_ref.dtype)


def _attention(q, kv, qg, kg, ctx_kv, rope_tabs):
    b, t, _ = q.shape
    latent = ctx_kv is not None
    past = ctx_kv[0].shape[1] if latent else 0
    s_all = past + t
    tq = 128
    kern = functools.partial(_attn_kernel, t_seq=t, past=past, tq=tq, latent=latent)
    in_specs = [
        pl.BlockSpec((1, tq, ATTN_W), lambda bi, qi: (bi, qi, 0)),
        pl.BlockSpec((1, t, 2 * KV_W), lambda bi, qi: (bi, 0, 0)),
    ]
    args = [q, kv]
    if latent:
        in_specs += [pl.BlockSpec((1, past, KV_W), lambda bi, qi: (bi, 0, 0))] * 2
        args += list(ctx_kv)
    in_specs += [pl.BlockSpec((1, LANES), lambda bi, qi: (0, 0))] * 2
    args += [qg, kg]
    out_specs = [pl.BlockSpec((1, tq, ATTN_W), lambda bi, qi: (bi, qi, 0))]
    out_shape = [jax.ShapeDtypeStruct((b, t, ATTN_W), BF16)]
    if latent:
        cos_t, sin_t = rope_tabs
        in_specs += [pl.BlockSpec((tq, LANES), lambda bi, qi: (qi, 0))] * 2
        in_specs += [pl.BlockSpec((t, LANES), lambda bi, qi: (0, 0))] * 2
        args += [cos_t, sin_t, cos_t, sin_t]
    else:
        out_specs.append(pl.BlockSpec((1, t, KV_W), lambda bi, qi: (bi, 0, 0)))
        out_shape.append(jax.ShapeDtypeStruct((b, t, KV_W), F32))
    res = pl.pallas_call(
        kern,
        grid=(b, t // tq),
        in_specs=in_specs,
        out_specs=out_specs,
        out_shape=out_shape,
        scratch_shapes=[pltpu.VMEM((s_all, LANES), BF16), pltpu.VMEM((LANES, s_all), BF16)],
        compiler_params=_cparams(("arbitrary", "arbitrary")),
        name="attention_lat" if latent else "attention_ctx",
    )(*args)
    return res if not latent else (res[0], None)


SGU_CHUNKS_PER_STEP = 4


def _sgu_kernel(su_ref, sv_ref, g_ref, ws_ref, bs_ref, o_ref):
    for c in range(su_ref.shape[0] // CHUNK):
        rows = slice(CHUNK * c, CHUNK * (c + 1))
        v = sv_ref[rows, :]
        vn = (v * lax.rsqrt(jnp.mean(v * v, axis=-1, keepdims=True) + EPS) * g_ref[...]).astype(BF16)
        for g in range(SGU_GROUPS):
            sl = slice(LANES * g, LANES * (g + 1))
            o_ref[rows, sl] = (su_ref[rows, sl] * (_dot(ws_ref[g], vn[:, sl]) + bs_ref[g])).astype(o_ref.dtype)


def _sgu(su, sv, g, ws_b, bs_b):
    m = su.shape[0]
    tm = min(SGU_CHUNKS_PER_STEP * CHUNK, m)
    return pl.pallas_call(
        _sgu_kernel,
        grid=(m // tm,),
        in_specs=[
            pl.BlockSpec((tm, SGU_W), lambda i: (i, 0)),
            pl.BlockSpec((tm, SGU_W), lambda i: (i, 0)),
            pl.BlockSpec((1, SGU_W), lambda i: (0, 0)),
            pl.BlockSpec((SGU_GROUPS, CHUNK, CHUNK), lambda i: (0, 0, 0)),
            pl.BlockSpec((SGU_GROUPS, CHUNK, LANES), lambda i: (0, 0, 0)),
        ],
        out_specs=pl.BlockSpec((tm, SGU_W), lambda i: (i, 0)),
        out_shape=jax.ShapeDtypeStruct((m, SGU_W), BF16),
        compiler_params=_cparams(("arbitrary",)),
        name="sgu",
    )(su, sv, g, ws_b, bs_b)


def _wkv_kernel(*refs, t_seq, latent, nb):
    (r_ref, k_ref, v_ref, lora_ref, cwr_ref, cwk_ref, cwv_ref, w0_ref, a0_ref, w2_ref, a2_ref, g2_ref,
     kk_ref, ka_ref, rk_ref, lng_ref, lnb_ref) = refs[:17]
    rest = refs[17:]
    if latent:
        st0_ref, o_ref = rest[:2]
        scr = rest[2:]
        sf_ref = None
    else:
        o_ref, sf_ref = rest[:2]
        scr = rest[2:]
        st0_ref = None
    tar_s, tv_s, ov_s, arb_s, w_s, n_s, gc_s, st_s, bon_s = scr
    c_len = WKV_C
    c2 = 2 * c_len
    nc = t_seq // c_len

    lane = lax.broadcasted_iota(jnp.int32, (1, LANES), 1)
    lo = lane < R_HEAD
    sum_bd = _head_block_diag(1.0)
    mean_bd = _head_block_diag(1.0 / R_HEAD)
    row_c = lax.broadcasted_iota(jnp.int32, (c_len, LANES), 0)
    ri = lax.broadcasted_iota(jnp.int32, (c2, c2), 0)
    ci = lax.broadcasted_iota(jnp.int32, (c2, c2), 1)
    same_head = (ri // c_len) == (ci // c_len)
    tt, ss = ri % c_len, ci % c_len
    eye2 = jnp.where(ri == ci, 1.0, 0.0).astype(F32)
    strict = (same_head & (ss < tt), same_head & (ss > tt))
    incl = (same_head & (ss <= tt), same_head & (ss >= tt))
    tj = lax.broadcasted_iota(jnp.int32, (c_len, c2), 0)
    sj = lax.broadcasted_iota(jnp.int32, (c_len, c2), 1)
    cum_both = jnp.where(((sj < c_len) & (sj <= tj)) | ((sj >= c_len) & (sj - c_len >= tj)), 1.0, 0.0).astype(BF16)
    sum_bd_b = sum_bd.astype(BF16)
    mean_bd_b = mean_bd.astype(BF16)

    def bdot(a, b):
        return _dot(a.astype(BF16), b.astype(BF16))

    def split2(x):
        hi = x.astype(BF16)
        return hi, (x - hi.astype(F32)).astype(BF16)

    def head_reduce(x, bd_b):
        hi, lo = split2(x)
        return _dot(hi, bd_b) + _dot(lo, bd_b)

    def stack_heads(x):
        return jnp.concatenate([jnp.where(lo, x, 0.0), jnp.where(lo, 0.0, x)], axis=0)

    def fold_heads(x):
        return x[0:c_len] + x[c_len:c2]

    def slot(bi, c):
        return bi * nc + c

    def conv_tokens(ref, w_ref, bi, c, t0):
        x = ref[bi, pl.ds(t0, c_len), :]
        if isinstance(c, int):
            zero_row = jnp.zeros((1, LANES), F32)
            prev = ref[bi, pl.ds(t0 - 1, 1), :] if c > 0 else zero_row
            nxt = ref[bi, pl.ds(t0 + c_len, 1), :] if c < nc - 1 else zero_row
        else:
            prev = ref[bi, pl.ds(jnp.maximum(t0 - 1, 0), 1), :]
            nxt = ref[bi, pl.ds(jnp.minimum(t0 + c_len, t_seq - 1), 1), :]
            prev = jnp.where(c > 0, prev, 0.0)
            nxt = jnp.where(c < nc - 1, nxt, 0.0)
        xm = jnp.where(row_c == 0, prev, pltpu.roll(x, 1, 0))
        xp = jnp.where(row_c == c_len - 1, nxt, pltpu.roll(x, c_len - 1, 0))
        return xm * w_ref[0:1, :] + x * w_ref[1:2, :] + xp * w_ref[2:3, :]

    def chunk_group(chunks):
        n = len(chunks)
        t0s = [c * c_len if isinstance(c, int) else pl.multiple_of(c * c_len, c_len) for _, c in chunks]
        rc = [conv_tokens(r_ref, cwr_ref, bi, c, t0) for (bi, c), t0 in zip(chunks, t0s)]
        kc = [conv_tokens(k_ref, cwk_ref, bi, c, t0) for (bi, c), t0 in zip(chunks, t0s)]
        vc = [conv_tokens(v_ref, cwv_ref, bi, c, t0) for (bi, c), t0 in zip(chunks, t0s)]
        kkr = [k * kk_ref[...] for k in kc]
        kk = [x * lax.rsqrt(head_reduce(x * x, sum_bd_b) + 1e-12) for x in kkr]
        tw = [jnp.tanh(lora_ref[bi, pl.ds(t0, c_len), 0:2 * LORA_W]) for (bi, _), t0 in zip(chunks, t0s)]
        xa = [lora_ref[bi, pl.ds(t0, c_len), 2 * LORA_W:2 * LORA_W + 2 * LORA_A] for (bi, _), t0 in zip(chunks, t0s)]
        v_hs = [stack_heads(v).astype(BF16) for v in vc]
        prob =[(i, d) for i in range(n) for d in range(2)]
        pre_w = [_dot(x.astype(BF16), w2_ref[...]) for x in tw]
        pre_a = [_dot(x.astype(BF16), a2_ref[...]) for x in xa]
        lw = [-DECAY_SCALE * _sigmoid(w0_ref[d:d + 1, :] + pre_w[i][:, LANES * d:LANES * (d + 1)]) for i, d in prob]
        ag = [_sigmoid(a0_ref[d:d + 1, :] + pre_a[i][:, LANES * d:LANES * (d + 1)]) for i, d in prob]
        kd = [kc[i] * (1.0 + (ag[p] - 1.0) * ka_ref[...]) for p, (i, d) in enumerate(prob)]
        kb = [kk[i] * ag[p] for p, (i, d) in enumerate(prob)]
        for i in range(n):
            bon = head_reduce(rc[i] * (kd[2 * i] + kd[2 * i + 1]) * rk_ref[...], sum_bd_b) * vc[i]
            bon_s[chunks[i][0], pl.ds(t0s[i], c_len), :] = bon
        lw_sp = [split2(x) for x in lw]
        zc = jnp.zeros((c_len, LANES), BF16)
        cum_res = [_dot(cum_both, jnp.concatenate(
            [jnp.concatenate([lw_sp[2 * i][0], lw_sp[2 * i][1], zc, zc], axis=1),
             jnp.concatenate([zc, zc, lw_sp[2 * i + 1][0], lw_sp[2 * i + 1][1]], axis=1)], axis=0)) for i in range(n)]
        cs_in = [cum_res[i][:, 2 * LANES * d:2 * LANES * d + LANES] + cum_res[i][:, 2 * LANES * d + LANES:2 * LANES * (d + 1)]
                 for i, d in prob]
        tot = [cs_in[p][c_len - 1:c_len, :] if d == 0 else cs_in[p][0:1, :] for p, (i, d) in enumerate(prob)]
        e_neg = [jnp.exp(-x) for x in cs_in]
        e_rem = [jnp.exp(tot[p] - cs_in[p]) for p in range(len(prob))]
        a_t = [-kk[i] * jnp.exp(cs_in[p] - (lw_sp[p][0].astype(F32) + lw_sp[p][1].astype(F32)))
               for p, (i, d) in enumerate(prob)]
        r_t = [rc[i] * jnp.exp(cs_in[p]) for p, (i, d) in enumerate(prob)]
        a_hs = [stack_heads(x).astype(BF16) for x in a_t]
        lhs = [jnp.concatenate([a_hs[p], stack_heads(r_t[p]).astype(BF16)], axis=0) for p in range(len(prob))]
        rhs = [jnp.concatenate([stack_heads(kd[p] * e_neg[p]), stack_heads(kb[p] * e_neg[p])], axis=0).astype(BF16)
               for p in range(len(prob))]
        m4 = [_dot_nt(a, b) for a, b in zip(lhs, rhs)]
        a_ak = [jnp.where(strict[d], m4[p][0:c2, 0:c2], 0.0).astype(BF16) for p, (i, d) in enumerate(prob)]
        l_ab = [jnp.where(strict[d], m4[p][0:c2, c2:2 * c2], 0.0) for p, (i, d) in enumerate(prob)]
        a_rk = [jnp.where(incl[d], m4[p][c2:2 * c2, 0:c2], 0.0).astype(BF16) for p, (i, d) in enumerate(prob)]
        for p, (i, d) in enumerate(prob):
            arb_s[d, slot(*chunks[i])] = jnp.where(incl[d], m4[p][c2:2 * c2, c2:2 * c2], 0.0).astype(BF16)
        t_inv = [eye2 + x for x in l_ab]
        pw_b = [x.astype(BF16) for x in l_ab]
        pw_b = [_dot(x, x).astype(BF16) for x in pw_b]
        for _ in range(4):
            res = [_dot(x, jnp.concatenate([x, t.astype(BF16)], axis=1)) for x, t in zip(pw_b, t_inv)]
            pw_b = [r[:, 0:c2].astype(BF16) for r in res]
            t_inv = [t + r[:, c2:2 * c2] for t, r in zip(t_inv, res)]
        t_inv = [t + _dot(x, t.astype(BF16)) for x, t in zip(pw_b, t_inv)]
        t_b = [t.astype(BF16) for t in t_inv]
        av = [_dot(a_ak[p], v_hs[i]).astype(BF16) for p, (i, d) in enumerate(prob)]
        ov = [fold_heads(_dot(a_rk[p], v_hs[i])) for p, (i, d) in enumerate(prob)]
        tt = [_dot(t, jnp.concatenate([a, x], axis=1)) for t, a, x in zip(t_b, a_hs, av)]
        ta = [fold_heads(r[:, 0:LANES]) for r in tt]
        tv = [fold_heads(r[:, LANES:2 * LANES]) for r in tt]
        kbt = [jnp.concatenate([kd[p] * e_rem[p], kb[p] * e_rem[p]], axis=0).T.astype(BF16) for p in range(len(prob))]
        zero_c = jnp.zeros((c_len, LANES), BF16)
        upd_rhs = [jnp.concatenate([jnp.concatenate([vc[i].astype(BF16), zero_c], axis=1),
                                    jnp.concatenate([tv[p].astype(BF16), ta[p].astype(BF16)], axis=1)], axis=0)
                   for p, (i, d) in enumerate(prob)]
        nw = [_dot(a, b) for a, b in zip(kbt, upd_rhs)]
        for p, (i, d) in enumerate(prob):
            c = slot(*chunks[i])
            tar_s[d, c] = jnp.concatenate([ta[p], r_t[p]], axis=0).astype(BF16)
            tv_s[d, c] = tv[p]
            ov_s[d, c] = ov[p]
            n_s[d, c] = jnp.where(bd_mask, nw[p][:, 0:LANES], 0.0)
            w_s[d, c] = jnp.where(bd_mask, nw[p][:, LANES:2 * LANES], 0.0).astype(BF16)
            gc_s[d, c] = jnp.broadcast_to(jnp.exp(tot[p]), (LANES, LANES)).T

    bd_mask = sum_bd > 0.5
    group1 = 8
    if nb * nc <= group1:
        chunk_group([(bi, c) for bi in range(nb) for c in range(nc)])
    else:
        assert nb == 1 and nc % group1 == 0

        def phase1(i, carry):
            chunk_group([(0, i * group1 + j) for j in range(group1)])
            return carry

        lax.fori_loop(0, nc // group1, phase1, 0)

    chains = [(bi, d) for bi in range(nb) for d in range(2)]

    def phase2(i, carry):
        cs = [slot(bi, i if d == 0 else nc - 1 - i) for bi, d in chains]
        st_b = [x.astype(BF16) for x in carry]
        for (bi, d), c, x in zip(chains, cs, st_b):
            st_s[d, c] = x
        prod = [_dot(w_s[d, c], x) for (bi, d), c, x in zip(chains, cs, st_b)]
        return tuple(gc_s[d, c] * s + pr + n_s[d, c] for (bi, d), c, s, pr in zip(chains, cs, carry, prod))

    if latent:
        init = tuple(st0_ref[bi, d, 0] for bi, d in chains)
    else:
        init = (jnp.zeros((LANES, LANES), F32),) * len(chains)
    fin = lax.fori_loop(0, nc, phase2, init)
    if sf_ref is not None:
        for (bi, d), st in zip(chains, fin):
            s_vk = st.T
            sf_ref[bi, d, 0] = s_vk[0:R_HEAD, 0:R_HEAD]
            sf_ref[bi, d, 1] = pltpu.roll(s_vk[R_HEAD:2 * R_HEAD, :], R_HEAD, 1)[:, 0:R_HEAD]

    def out_group(chunks):
        prob = [(bi, c, slot(bi, c), d) for bi, c in chunks for d in range(2)]
        both = [_dot(tar_s[d, s], st_s[d, s]) for _, _, s, d in prob]
        u = [both[p][0:c_len] + tv_s[d, s] for p, (_, _, s, d) in enumerate(prob)]
        ou = [_dot(arb_s[d, s], stack_heads(u[p]).astype(BF16)) for p, (_, _, s, d) in enumerate(prob)]
        o = [ov_s[d, s] + both[p][c_len:c2] + fold_heads(ou[p]) for p, (_, _, s, d) in enumerate(prob)]
        of = [o[2 * j] + o[2 * j + 1] for j in range(len(chunks))]
        xc = [x - head_reduce(x, mean_bd_b) for x in of]
        var = [head_reduce(x * x, mean_bd_b) for x in xc]
        for j, (bi, c) in enumerate(chunks):
            t0 = c * c_len if isinstance(c, int) else pl.multiple_of(c * c_len, c_len)
            on = xc[j] * lax.rsqrt(var[j] + GN_EPS) * lng_ref[...] + lnb_ref[...]
            xg = lora_ref[bi, pl.ds(t0, c_len), 2 * LORA_W + 2 * LORA_A:LORA_ALL]
            g_out = _dot(_sigmoid(xg).astype(BF16), g2_ref[...])
            o_ref[bi, pl.ds(t0, c_len), :] = ((on + bon_s[bi, pl.ds(t0, c_len), :]) * g_out).astype(o_ref.dtype)

    group3 = 4
    if nc <= group3:
        for bi in range(nb):
            out_group([(bi, c) for c in range(nc)])
    else:
        assert nb == 1 and nc % group3 == 0

        def phase3(i, carry):
            out_group([(0, i * group3 + j) for j in range(group3)])
            return carry

        lax.fori_loop(0, nc // group3, phase3, 0)


def _wkv(rkv, lora, wts, st0):
    b, t, _ = rkv.shape
    latent = st0 is not None
    nc = t // WKV_C
    npair = RW // LANES
    nb = 2 if (2 * nc <= 8 and b % 2 == 0) else 1
    kern = functools.partial(_wkv_kernel, t_seq=t, latent=latent, nb=nb)

    def seq_spec(col0):
        return pl.BlockSpec((nb, t, LANES), lambda bi, p: (bi, 0, col0 + p))

    def row_spec(rows, col0=0):
        return pl.BlockSpec((rows, LANES), lambda bi, p: (0, col0 + p))

    in_specs = [
        seq_spec(0), seq_spec(npair), seq_spec(2 * npair),
        pl.BlockSpec((nb, t, LORA_ALL), lambda bi, p: (bi, 0, 0)),
        row_spec(3, 0), row_spec(3, npair), row_spec(3, 2 * npair),
        row_spec(2), row_spec(2),
        pl.BlockSpec((LANES, 2 * LANES), lambda bi, p: (0, p)), pl.BlockSpec((LANES, 2 * LANES), lambda bi, p: (0, p)),
        row_spec(LANES),
        row_spec(1), row_spec(1), row_spec(1), row_spec(1), row_spec(1),
    ]
    args = [rkv, rkv, rkv, lora, wts["conv"], wts["conv"], wts["conv"], wts["w0"], wts["a0"], wts["w2"], wts["a2"],
            wts["g2"], wts["k_k"], wts["k_a"], wts["r_k"], wts["ln_g"], wts["ln_b"]]
    out_specs = [seq_spec(0)]
    out_shape = [jax.ShapeDtypeStruct((b, t, RW), BF16)]
    if latent:
        in_specs.append(pl.BlockSpec((nb, 2, 1, LANES, LANES), lambda bi, p: (bi, 0, p, 0, 0)))
        args.append(st0)
    else:
        out_specs.append(pl.BlockSpec((nb, 2, 2, R_HEAD, R_HEAD), lambda bi, p: (bi, 0, p, 0, 0)))
        out_shape.append(jax.ShapeDtypeStruct((b, 2, R_HEADS, R_HEAD, R_HEAD), F32))
    chunk_rows = pltpu.VMEM((2, nb * nc, WKV_C, LANES), F32)
    chunk_sq = pltpu.VMEM((2, nb * nc, LANES, LANES), F32)
    chunk_sq_b = pltpu.VMEM((2, nb * nc, LANES, LANES), BF16)
    res = pl.pallas_call(
        kern,
        grid=(b // nb, npair),
        in_specs=in_specs,
        out_specs=out_specs,
        out_shape=out_shape,
        scratch_shapes=[chunk_sq_b, chunk_rows, chunk_rows, chunk_sq_b, chunk_sq_b, chunk_sq, chunk_sq, chunk_sq_b,
                        pltpu.VMEM((nb, t, LANES), F32)],
        compiler_params=_cparams(("arbitrary", "arbitrary")),
        name="wkv_lat" if latent else "wkv_ctx",
    )(*args)
    return (res[0], None) if latent else (res[0], res[1])


def _merge_kernel(x_ref, oa_ref, os_ref, or_ref, gates_ref, gt_ref, pa_ref, ps_ref, pr_ref, wo_ref, o_ref,
                  *, row0, rstride, bps):
    row = row0 + (pl.program_id(0) // bps) * rstride if rstride else row0
    d = D_MODEL
    gate = lambda j: _sigmoid(gates_ref[:, j * d:(j + 1) * d].astype(F32))
    merged = (gate(0) * _dot(oa_ref[...].astype(BF16), pa_ref[...])
              + gate(1) * _dot(os_ref[...].astype(BF16), ps_ref[...])
              + gate(2) * _dot(or_ref[...].astype(BF16), pr_ref[...]))
    o_ref[...] = x_ref[...] + _mod_row(gt_ref, row) * _dot(merged.astype(BF16), wo_ref[...])


def _merge(x2, oa, osg, orw, gates, mod_l, pa, ps, pr, wo, row0, rstride, t_seq):
    m, d = x2.shape
    tm = 256
    kern = functools.partial(_merge_kernel, row0=row0, rstride=rstride, bps=t_seq // tm)
    tok = lambda w: pl.BlockSpec((tm, w), lambda i: (i, 0))
    full = lambda a: pl.BlockSpec(a.shape, lambda i: (0, 0))
    return pl.pallas_call(
        kern,
        grid=(m // tm,),
        in_specs=[tok(d), tok(ATTN_W), tok(SGU_W), tok(RW), tok(3 * d),
                  pl.BlockSpec((8, d), lambda i: (0, 2)),
                  full(pa), full(ps), full(pr), full(wo)],
        out_specs=tok(d),
        out_shape=jax.ShapeDtypeStruct((m, d), F32),
        compiler_params=_cparams(("arbitrary",)),
        name="merge",
    )(x2, oa, osg, orw, gates, mod_l, pa, ps, pr, wo)


def _moe_kernel(x_ref, sh_ref, sc_ref, gt_ref, g2_ref, rwh_ref, rwl_ref, rb_ref, shg_ref, shu_ref, shd_ref,
                wg_ref, wu_ref, wd_ref, o_ref, h_s, comb_s, acc_s, *, row0, rstride, bps):
    row = row0 + (pl.program_id(0) // bps) * rstride if rstride else row0
    e = pl.program_id(1)
    tm = x_ref.shape[0]

    @pl.when(e == 0)
    def _():
        x = x_ref[...]
        h = x * lax.rsqrt(jnp.mean(x * x, axis=-1, keepdims=True) + EPS) * g2_ref[...]
        h = h * (1.0 + _mod_row(sc_ref, row)) + _mod_row(sh_ref, row)
        hb = h.astype(BF16)
        h_lo = (h - hb.astype(F32)).astype(BF16)
        logits = _dot(hb, rwh_ref[...]) + _dot(hb, rwl_ref[...]) + _dot(h_lo, rwh_ref[...])
        scores = _sigmoid(logits)
        sc_t = scores.T[0:N_EXPERTS]
        sel = (scores + rb_ref[...]).T[0:N_EXPERTS]
        eid = lax.broadcasted_iota(jnp.int32, (N_EXPERTS, tm), 0).astype(F32)
        picked = jnp.zeros((N_EXPERTS, tm), jnp.bool_)
        for _ in range(TOP_K):
            best = jnp.max(sel, axis=0, keepdims=True)
            first = jnp.min(jnp.where(sel == best, eid, float(N_EXPERTS)), axis=0, keepdims=True)
            hit = eid == first
            picked = jnp.logical_or(picked, hit)
            sel = jnp.where(hit, -jnp.inf, sel)
        sw = jnp.where(picked, sc_t, 0.0)
        comb_t = sw / jnp.sum(sw, axis=0, keepdims=True) * ROUTED_SCALE
        comb = jnp.concatenate([comb_t, jnp.zeros((LANES - N_EXPERTS, tm), F32)], axis=0).T
        for b in range(N_EXPERTS // MOE_EB):
            comb_s[b] = comb if b == 0 else pltpu.roll(comb, LANES - MOE_EB * b, 1)
        h_s[...] = hb
        acc_s[...] = _dot((_silu(_dot(hb, shg_ref[...])) * _dot(hb, shu_ref[...])).astype(BF16), shd_ref[...])

    hb = h_s[...]
    comb = comb_s[e]
    hid = []
    for j in range(MOE_EB):
        gu = _dot(hb, jnp.concatenate([wg_ref[j], wu_ref[j]], axis=1))
        cw = jnp.broadcast_to(comb[:, j:j + 1], (tm, D_EXPERT))
        hid.append((_silu(gu[:, 0:D_EXPERT]) * gu[:, D_EXPERT:2 * D_EXPERT] * cw).astype(BF16))
    acc_s[...] += _dot(jnp.concatenate(hid, axis=1), wd_ref[...])

    @pl.when(e == pl.num_programs(1) - 1)
    def _():
        o_ref[...] = x_ref[...] + _mod_row(gt_ref, row) * acc_s[...]


def _moe(x2, mod_l, g2, rw_hi, rw_lo, rb_p, shg, shu, shd, wg_l, wu_l, wd_l, row0, rstride, t_seq):
    m, d = x2.shape
    (wg, layer), (wu, _), (wd, _) = wg_l, wu_l, wd_l
    tm = min(1024, t_seq if rstride else m)
    n_hid = MOE_EB * D_EXPERT
    kern = functools.partial(_moe_kernel, row0=row0, rstride=rstride, bps=max(t_seq // tm, 1))
    full = lambda a: pl.BlockSpec(a.shape, lambda i, e: (0, 0))
    return pl.pallas_call(
        kern,
        grid=(m // tm, N_EXPERTS // MOE_EB),
        in_specs=[
            pl.BlockSpec((tm, d), lambda i, e: (i, 0)),
            pl.BlockSpec((8, d), lambda i, e: (0, 3)),
            pl.BlockSpec((8, d), lambda i, e: (0, 4)),
            pl.BlockSpec((8, d), lambda i, e: (0, 5)),
            full(g2), full(rw_hi), full(rw_lo), full(rb_p), full(shg), full(shu), full(shd),
            pl.BlockSpec((None, MOE_EB, d, D_EXPERT), lambda i, e: (layer, e, 0, 0)),
            pl.BlockSpec((None, MOE_EB, d, D_EXPERT), lambda i, e: (layer, e, 0, 0)),
            pl.BlockSpec((None, n_hid, d), lambda i, e: (layer, e, 0)),
        ],
        out_specs=pl.BlockSpec((tm, d), lambda i, e: (i, 0)),
        out_shape=jax.ShapeDtypeStruct((m, d), F32),
        scratch_shapes=[pltpu.VMEM((tm, d), BF16), pltpu.VMEM((N_EXPERTS // MOE_EB, tm, LANES), F32),
                        pltpu.VMEM((tm, d), F32)],
        compiler_params=_cparams(("arbitrary", "arbitrary")),
        name="moe",
    )(x2, mod_l, mod_l, mod_l, g2, rw_hi, rw_lo, rb_p, shg, shu, shd, wg, wu, wd)


def _layer(x, mod_l, w, row0, rstride, ctx, rope_tabs):
    b, t, d = x.shape
    x2 = x.reshape(b * t, d)
    q, kv, su, sv, rkv, lora, gates = _inproj(x2, mod_l, w["norm1_g"], w["w_in"], row0, rstride, t)
    if ctx is None:
        o_attn, k_new = _attention(q.reshape(b, t, -1), kv.reshape(b, t, -1), w["q_norm"], w["k_norm"], None, None)
        st0 = None
    else:
        o_attn, k_new = _attention(q.reshape(b, t, -1), kv.reshape(b, t, -1), w["q_norm"], w["k_norm"],
                                   ctx[:2], rope_tabs)
        st0 = ctx[2]
    o_sgu = _sgu(su, sv, w["sgu_norm_g"], w["sgu_ws"], w["sgu_bs"])
    o_wkv, s_fin = _wkv(rkv.reshape(b, t, -1), lora.reshape(b, t, -1), w["wkv"], st0)
    x1 = _merge(x2, o_attn.reshape(b * t, -1), o_sgu, o_wkv.reshape(b * t, -1), gates, mod_l,
                w["proj_attn"], w["proj_sgu"], w["proj_rwkv"], w["w_out"], row0, rstride, t)
    x_out = _moe(x1, mod_l, w["norm2_g"], w["router_w_hi"], w["router_w_lo"], w["router_bias"],
                 w["sh_gate"], w["sh_up"], w["sh_down"],
                 w["exp_gate"], w["exp_up"], w["exp_down"], row0, rstride, t)
    return x_out.reshape(b, t, d), (k_new, kv, s_fin)


def _rope_tables(n_tok):
    t = jnp.arange(n_tok)
    row = (t // GRID_W).astype(F32)
    col = (t % GRID_W).astype(F32)
    inv = ROPE_THETA ** (-jnp.arange(N_FREQ, dtype=F32) / N_FREQ)
    ang = jnp.stack([row[:, None] * inv[None, :], col[:, None] * inv[None, :]], axis=1)
    cos, sin = jnp.cos(ang), jnp.sin(ang)
    cos64 = jnp.stack([cos, cos], axis=2).reshape(n_tok, HEAD_DIM)
    sin64 = jnp.stack([-sin, sin], axis=2).reshape(n_tok, HEAD_DIM)
    return jnp.tile(cos64, (1, LANES // HEAD_DIM)), jnp.tile(sin64, (1, LANES // HEAD_DIM))


def _lora_pair_layout(w):
    npair = RW // LANES
    w4 = w.reshape(2, w.shape[1], npair, LANES)
    bd = jnp.einsum("dkpc,de->dkpec", w4, jnp.eye(2, dtype=w.dtype))
    return bd.reshape(2 * w.shape[1], npair * 2 * LANES).astype(BF16)


def _layer_weights(l, mod_w, mod_b, norm1_g, norm2_g, w_in, q_norm, k_norm, sgu_norm_g, sgu_ws, sgu_bs, rwkv_conv,
                   rwkv_w0, rwkv_w2, rwkv_a0, rwkv_a2, rwkv_g2, rwkv_k_k, rwkv_k_a, rwkv_r_k, rwkv_ln_g, rwkv_ln_b,
                   proj_attn, proj_sgu, proj_rwkv, w_out, router_w, router_bias, exp_gate, exp_up, exp_down,
                   sh_gate, sh_up, sh_down):
    d = D_MODEL
    row = lambda a: a.reshape(1, -1)
    pad_e = LANES - N_EXPERTS
    rw_pad = jnp.pad(router_w[l], ((0, 0), (0, pad_e)))
    return {
        "norm1_g": row(norm1_g[l]), "norm2_g": row(norm2_g[l]),
        "w_in": (w_in.astype(BF16), l),
        "q_norm": jnp.tile(row(q_norm[l]), (1, LANES // HEAD_DIM)),
        "k_norm": jnp.tile(row(k_norm[l]), (1, LANES // HEAD_DIM)),
        "sgu_norm_g": row(sgu_norm_g[l]),
        "sgu_ws": sgu_ws[l].astype(BF16),
        "sgu_bs": jnp.broadcast_to(sgu_bs[l][:, :, None], (SGU_GROUPS, CHUNK, LANES)),
        "wkv": {
            "conv": rwkv_conv[l],
            "w0": rwkv_w0[l], "a0": rwkv_a0[l],
            "w2": _lora_pair_layout(rwkv_w2[l]), "a2": _lora_pair_layout(rwkv_a2[l]),
            "g2": rwkv_g2[l].astype(BF16),
            "k_k": row(rwkv_k_k[l]), "k_a": row(rwkv_k_a[l]), "r_k": row(rwkv_r_k[l]),
            "ln_g": row(rwkv_ln_g[l]), "ln_b": row(rwkv_ln_b[l]),
        },
        "proj_attn": proj_attn[l].astype(BF16), "proj_sgu": proj_sgu[l].astype(BF16),
        "proj_rwkv": proj_rwkv[l].astype(BF16), "w_out": w_out[l].astype(BF16),
        "router_w_hi": rw_pad.astype(BF16),
        "router_w_lo": (rw_pad - rw_pad.astype(BF16).astype(F32)).astype(BF16),
        "router_bias": jnp.pad(row(router_bias[l]), ((0, 0), (0, pad_e))),
        "sh_gate": sh_gate[l].astype(BF16), "sh_up": sh_up[l].astype(BF16), "sh_down": sh_down[l].astype(BF16),
        "exp_gate": (exp_gate.astype(BF16), l), "exp_up": (exp_up.astype(BF16), l),
        "exp_down": (exp_down.astype(BF16).reshape(-1, N_EXPERTS * D_EXPERT, d), l),
    }


def _state_to_blockdiag_t(s):
    b = s.shape[0]
    st = jnp.swapaxes(s, -1, -2).reshape(b, 2, R_HEADS // 2, 2, R_HEAD, R_HEAD)
    bd = jnp.einsum("bdpjkv,ji->bdpjkiv", st, jnp.eye(2, dtype=s.dtype))
    return bd.reshape(b, 2, R_HEADS // 2, LANES, LANES)


def kernel(x_prompt, x_sample, cache_k, cache_v, state_wkv, c, c_ctx, mod_w, mod_b, norm1_g, norm2_g, w_in, q_norm, k_norm, sgu_norm_g, sgu_ws, sgu_bs, rwkv_conv, rwkv_w0, rwkv_w2, rwkv_a0, rwkv_a2, rwkv_g2, rwkv_k_k, rwkv_k_a, rwkv_r_k, rwkv_ln_g, rwkv_ln_b, proj_attn, proj_sgu, proj_rwkv, w_out, router_w, router_bias, exp_gate, exp_up, exp_down, sh_gate, sh_up, sh_down):
    params = (mod_w, mod_b, norm1_g, norm2_g, w_in, q_norm, k_norm, sgu_norm_g, sgu_ws, sgu_bs, rwkv_conv,
              rwkv_w0, rwkv_w2, rwkv_a0, rwkv_a2, rwkv_g2, rwkv_k_k, rwkv_k_a, rwkv_r_k, rwkv_ln_g, rwkv_ln_b,
              proj_attn, proj_sgu, proj_rwkv, w_out, router_w, router_bias, exp_gate, exp_up, exp_down,
              sh_gate, sh_up, sh_down)
    n_b, n_t, d = x_prompt.shape
    n_db, n_dt, _ = x_sample.shape
    assert n_db + 1 <= 8
    cmat = jnp.zeros((8, d), F32).at[0].set(c_ctx).at[1:1 + n_db].set(c)
    mod = _modulation(cmat, mod_w, mod_b)
    rope_tabs = _rope_tables(n_dt)
    xp, xs = x_prompt, x_sample
    ks, vs, ss = [], [], []
    for l in range(DEPTH):
        w = _layer_weights(l, *params)
        xp, (k_new, kv, s_fin) = _layer(xp, mod[l], w, 0, 0, None, None)
        ks.append(k_new.reshape(n_b, n_t, N_KV_HEADS, HEAD_DIM))
        vs.append(kv.reshape(n_b, n_t, 2 * KV_W)[:, :, KV_W:].reshape(n_b, n_t, N_KV_HEADS, HEAD_DIM))
        ss.append(s_fin)
        ctx = (cache_k[:, l].reshape(n_db, -1, KV_W), cache_v[:, l].reshape(n_db, -1, KV_W),
               _state_to_blockdiag_t(state_wkv[:, l]))
        xs, _ = _layer(xs, mod[l], w, 1, 1, ctx, rope_tabs)
    return (xp, xs, jnp.stack(ks, axis=1), jnp.stack(vs, axis=1), jnp.stack(ss, axis=1))
```

```python
import functools

import jax
import jax.numpy as jnp
from jax import lax
from jax.experimental import pallas as pl
from jax.experimental.pallas import tpu as pltpu

F32 = jnp.float32
BF16 = jnp.bfloat16
HI = lax.Precision.HIGHEST

D_MODEL = 1024
DEPTH = 2
GRID_W = 64
N_HEADS = 8
N_KV_HEADS = 2
HEAD_DIM = 64
ATTN_W = N_HEADS * HEAD_DIM
KV_W = N_KV_HEADS * HEAD_DIM
N_FREQ = HEAD_DIM // 4
ROPE_THETA = 10000.0
CHUNK = 128
SGU_GROUPS = 4
SGU_W = 512
R_HEADS = 8
R_HEAD = 64
RW = R_HEADS * R_HEAD
LORA_W = 64
LORA_A = 64
LORA_G = 128
DECAY_SCALE = 0.606531
GN_EPS = 64e-5
N_EXPERTS = 64
TOP_K = 6
D_EXPERT = 128
D_SHARED = 256
ROUTED_SCALE = 2.5
EPS = 1e-6
LORA_ALL = 2 * LORA_W + 2 * LORA_A + LORA_G
N_IN = ATTN_W + 2 * KV_W + 2 * SGU_W + 3 * RW + LORA_ALL + 3 * D_MODEL

LOG2_E = 1.4426950408889634
TOKEN_TILE = 512
LANES = 128
WKV_C = 64
MOE_EB = 8
VMEM_LIMIT = 56 * 1024 * 1024


def _cparams(sem, vmem=VMEM_LIMIT):
    return pltpu.CompilerParams(dimension_semantics=sem, vmem_limit_bytes=vmem)


def _dot(a, b, prec=None):
    return jnp.dot(a, b, preferred_element_type=F32, precision=prec)


def _dot_nt(a, b, prec=None):
    return lax.dot_general(a, b, (((1,), (1,)), ((), ())), preferred_element_type=F32, precision=prec)


def _sigmoid(x):
    return 1.0 / (1.0 + jnp.exp(-x))


def _silu(x):
    return x / (1.0 + jnp.exp(-x))


def _head_block_diag(value):
    r = lax.broadcasted_iota(jnp.int32, (LANES, LANES), 0) // HEAD_DIM
    c = lax.broadcasted_iota(jnp.int32, (LANES, LANES), 1) // HEAD_DIM
    return jnp.where(r == c, value, 0.0).astype(F32)


def _mod_row(ref, row):
    return ref[pl.ds(row, 1), :]


def _mod_kernel(c_ref, w_ref, b_ref, o_ref):
    o_ref[0] = _dot(_silu(c_ref[...]), w_ref[0], HI) + b_ref[0]


def _modulation(cmat, mod_w, mod_b):
    n_l, d, n6 = mod_w.shape
    tn = n6 // 4
    return pl.pallas_call(
        _mod_kernel,
        grid=(n_l, n6 // tn),
        in_specs=[
            pl.BlockSpec((8, d), lambda l, j: (0, 0)),
            pl.BlockSpec((1, d, tn), lambda l, j: (l, 0, j)),
            pl.BlockSpec((1, 1, tn), lambda l, j: (l, 0, j)),
        ],
        out_specs=pl.BlockSpec((1, 8, tn), lambda l, j: (l, 0, j)),
        out_shape=jax.ShapeDtypeStruct((n_l, 8, n6), F32),
        compiler_params=_cparams(("arbitrary", "arbitrary")),
        name="modulation",
    )(cmat, mod_w, mod_b.reshape(n_l, 1, n6))


_IN_SPLIT = (ATTN_W, 2 * KV_W, SGU_W, SGU_W, 3 * RW, LORA_ALL, 3 * D_MODEL)


def _inproj_kernel(x_ref, sh_ref, sc_ref, g_ref, w_ref, *out_refs, row0, rstride, bps):
    row = row0 + (pl.program_id(0) // bps) * rstride if rstride else row0
    x = x_ref[...]
    h = x * lax.rsqrt(jnp.mean(x * x, axis=-1, keepdims=True) + EPS) * g_ref[...]
    h = (h * (1.0 + _mod_row(sc_ref, row)) + _mod_row(sh_ref, row)).astype(BF16)
    off = 0
    for ref, size in zip(out_refs, _IN_SPLIT):
        ref[...] = _dot(h, w_ref[:, off:off + size]).astype(ref.dtype)
        off += size


def _inproj(x2, mod_l, g1, w_in_layers, row0, rstride, t_seq):
    m, d = x2.shape
    w_in_b, layer = w_in_layers
    tm = min(TOKEN_TILE, t_seq if rstride else m)
    kern = functools.partial(_inproj_kernel, row0=row0, rstride=rstride, bps=max(t_seq // tm, 1))
    return pl.pallas_call(
        kern,
        grid=(m // tm,),
        in_specs=[
            pl.BlockSpec((tm, d), lambda i: (i, 0)),
            pl.BlockSpec((8, d), lambda i: (0, 0)),
            pl.BlockSpec((8, d), lambda i: (0, 1)),
            pl.BlockSpec((1, d), lambda i: (0, 0)),
            pl.BlockSpec((None, d, N_IN), lambda i: (layer, 0, 0), pipeline_mode=pl.Buffered(1)),
        ],
        out_specs=[pl.BlockSpec((tm, s), lambda i: (i, 0)) for s in _IN_SPLIT],
        out_shape=[jax.ShapeDtypeStruct((m, s), BF16 if s == 3 * D_MODEL else F32) for s in _IN_SPLIT],
        compiler_params=_cparams(("arbitrary",)),
        name="inproj",
    )(x2, mod_l, mod_l, g1, w_in_b)


def _attn_kernel(*refs, t_seq, past, tq, latent):
    if latent:
        (q_ref, kv_ref, ck_ref, cv_ref, qg_ref, kg_ref, cq_ref, sq_ref, ckk_ref, skk_ref,
         o_ref, k_s, vt_s) = refs
    else:
        q_ref, kv_ref, qg_ref, kg_ref, o_ref, kn_ref, k_s, vt_s = refs
    lane = lax.broadcasted_iota(jnp.int32, (1, LANES), 1)
    first_half = ((lane // N_FREQ) % 2) == 0
    mean_bd = _head_block_diag(1.0 / HEAD_DIM).astype(BF16)
    q_per_kv = N_HEADS // N_KV_HEADS

    def head_norm(x, g):
        sq = x * x
        hi = sq.astype(BF16)
        lo = (sq - hi.astype(F32)).astype(BF16)
        return x * lax.rsqrt(_dot(hi, mean_bd) + _dot(lo, mean_bd) + EPS) * g

    def rope(x, cos, sin_signed):
        rot = jnp.where(first_half, pltpu.roll(x, LANES - N_FREQ, 1), pltpu.roll(x, N_FREQ, 1))
        return x * cos + rot * sin_signed

    @pl.when(pl.program_id(1) == 0)
    def _():
        rb = 256
        for i in range(past // rb):
            r0 = i * rb
            k_s[r0:r0 + rb, :] = ck_ref[0, r0:r0 + rb, :].astype(BF16)
            vt_s[:, r0:r0 + rb] = cv_ref[0, r0:r0 + rb, :].T.astype(BF16)
        for i in range(t_seq // rb):
            r0 = i * rb
            kn = head_norm(kv_ref[0, r0:r0 + rb, 0:KV_W], kg_ref[...])
            if latent:
                kn = rope(kn, ckk_ref[r0:r0 + rb, :], skk_ref[r0:r0 + rb, :])
            else:
                kn_ref[0, r0:r0 + rb, :] = kn
            k_s[past + r0:past + r0 + rb, :] = kn.astype(BF16)
            vt_s[:, past + r0:past + r0 + rb] = kv_ref[0, r0:r0 + rb, KV_W:2 * KV_W].T.astype(BF16)

    slabs = []
    for j in range(ATTN_W // LANES):
        qn = head_norm(q_ref[0, :, LANES * j:LANES * (j + 1)], qg_ref[...])
        if latent:
            qn = rope(qn, cq_ref[...], sq_ref[...])
        slabs.append(qn * (HEAD_DIM ** -0.5 * LOG2_E))
    q_t = jnp.concatenate(slabs, axis=1).T.astype(BF16)
    zero_h = jnp.zeros((HEAD_DIM, tq), BF16)
    qz = []
    for g in range(N_KV_HEADS):
        cols = []
        for h in range(q_per_kv * g, q_per_kv * (g + 1)):
            q_h = q_t[HEAD_DIM * h:HEAD_DIM * (h + 1), :]
            cols.append(jnp.concatenate([q_h, zero_h] if g == 0 else [zero_h, q_h], axis=0))
        qz.append(jnp.concatenate(cols, axis=1))
    s_t = [_dot(k_s[...], x) for x in qz]
    p_t = [jnp.exp2(x - jnp.max(x, axis=0, keepdims=True)) for x in s_t]
    den = [jnp.sum(x, axis=0, keepdims=True) for x in p_t]
    o_t = [_dot(vt_s[HEAD_DIM * g:HEAD_DIM * (g + 1), :], p_t[g].astype(BF16)) / den[g]
           for g in range(N_KV_HEADS)]
    outs =[o_t[g][:, tq * h:tq * (h + 1)] for g in range(N_KV_HEADS) for h in range(q_per_kv)]
    o_ref[0] = jnp.concatenate(outs, axis=0).T.astype(o_ref.dtype)


def _attention(q, kv, qg, kg, ctx_kv, rope_tabs):
    b, t, _ = q.shape
    latent = ctx_kv is not None
    past = ctx_kv[0].shape[1] if latent else 0
    s_all = past + t
    tq = 128
    kern = functools.partial(_attn_kernel, t_seq=t, past=past, tq=tq, latent=latent)
    in_specs = [
        pl.BlockSpec((1, tq, ATTN_W), lambda bi, qi: (bi, qi, 0)),
        pl.BlockSpec((1, t, 2 * KV_W), lambda bi, qi: (bi, 0, 0)),
    ]
    args = [q, kv]
    if latent:
        in_specs += [pl.BlockSpec((1, past, KV_W), lambda bi, qi: (bi, 0, 0))] * 2
        args += list(ctx_kv)
    in_specs += [pl.BlockSpec((1, LANES), lambda bi, qi: (0, 0))] * 2
    args += [qg, kg]
    out_specs = [pl.BlockSpec((1, tq, ATTN_W), lambda bi, qi: (bi, qi, 0))]
    out_shape = [jax.ShapeDtypeStruct((b, t, ATTN_W), BF16)]
    if latent:
        cos_t, sin_t = rope_tabs
        in_specs += [pl.BlockSpec((tq, LANES), lambda bi, qi: (qi, 0))] * 2
        in_specs += [pl.BlockSpec((t, LANES), lambda bi, qi: (0, 0))] * 2
        args += [cos_t, sin_t, cos_t, sin_t]
    else:
        out_specs.append(pl.BlockSpec((1, t, KV_W), lambda bi, qi: (bi, 0, 0)))
        out_shape.append(jax.ShapeDtypeStruct((b, t, KV_W), F32))
    res = pl.pallas_call(
        kern,
        grid=(b, t // tq),
        in_specs=in_specs,
        out_specs=out_specs,
        out_shape=out_shape,
        scratch_shapes=[pltpu.VMEM((s_all, LANES), BF16), pltpu.VMEM((LANES, s_all), BF16)],
        compiler_params=_cparams(("arbitrary", "arbitrary")),
        name="attention_lat" if latent else "attention_ctx",
    )(*args)
    return res if not latent else (res[0], None)


SGU_CHUNKS_PER_STEP = 4


def _sgu_kernel(su_ref, sv_ref, g_ref, ws_ref, bs_ref, o_ref):
    for c in range(su_ref.shape[0] // CHUNK):
        rows = slice(CHUNK * c, CHUNK * (c + 1))
        v = sv_ref[rows, :]
        vn = (v * lax.rsqrt(jnp.mean(v * v, axis=-1, keepdims=True) + EPS) * g_ref[...]).astype(BF16)
        for g in range(SGU_GROUPS):
            sl = slice(LANES * g, LANES * (g + 1))
            o_ref[rows, sl] = (su_ref[rows, sl] * (_dot(ws_ref[g], vn[:, sl]) + bs_ref[g])).astype(o_ref.dtype)


def _sgu(su, sv, g, ws_b, bs_b):
    m = su.shape[0]
    tm = min(SGU_CHUNKS_PER_STEP * CHUNK, m)
    return pl.pallas_call(
        _sgu_kernel,
        grid=(m // tm,),
        in_specs=[
            pl.BlockSpec((tm, SGU_W), lambda i: (i, 0)),
            pl.BlockSpec((tm, SGU_W), lambda i: (i, 0)),
            pl.BlockSpec((1, SGU_W), lambda i: (0, 0)),
            pl.BlockSpec((SGU_GROUPS, CHUNK, CHUNK), lambda i: (0, 0, 0)),
            pl.BlockSpec((SGU_GROUPS, CHUNK, LANES), lambda i: (0, 0, 0)),
        ],
        out_specs=pl.BlockSpec((tm, SGU_W), lambda i: (i, 0)),
        out_shape=jax.ShapeDtypeStruct((m, SGU_W), BF16),
        compiler_params=_cparams(("arbitrary",)),
        name="sgu",
    )(su, sv, g, ws_b, bs_b)


def _wkv_kernel(*refs, t_seq, latent, nb):
    (r_ref, k_ref, v_ref, lora_ref, cwr_ref, cwk_ref, cwv_ref, w0_ref, a0_ref, w2_ref, a2_ref, g2_ref,
     kk_ref, ka_ref, rk_ref, lng_ref, lnb_ref) = refs[:17]
    rest = refs[17:]
    if latent:
        st0_ref, o_ref = rest[:2]
        scr = rest[2:]
        sf_ref = None
    else:
        o_ref, sf_ref = rest[:2]
        scr = rest[2:]
        st0_ref = None
    tar_s, tv_s, ov_s, arb_s, w_s, n_s, gc_s, st_s, bon_s = scr
    c_len = WKV_C
    c2 = 2 * c_len
    nc = t_seq // c_len

    lane = lax.broadcasted_iota(jnp.int32, (1, LANES), 1)
    lo = lane < R_HEAD
    sum_bd = _head_block_diag(1.0)
    mean_bd = _head_block_diag(1.0 / R_HEAD)
    row_c = lax.broadcasted_iota(jnp.int32, (c_len, LANES), 0)
    ri = lax.broadcasted_iota(jnp.int32, (c2, c2), 0)
    ci = lax.broadcasted_iota(jnp.int32, (c2, c2), 1)
    same_head = (ri // c_len) == (ci // c_len)
    tt, ss = ri % c_len, ci % c_len
    eye2 = jnp.where(ri == ci, 1.0, 0.0).astype(F32)
    strict = (same_head & (ss < tt), same_head & (ss > tt))
    incl = (same_head & (ss <= tt), same_head & (ss >= tt))
    tj = lax.broadcasted_iota(jnp.int32, (c_len, c2), 0)
    sj = lax.broadcasted_iota(jnp.int32, (c_len, c2), 1)
    cum_both = jnp.where(((sj < c_len) & (sj <= tj)) | ((sj >= c_len) & (sj - c_len >= tj)), 1.0, 0.0).astype(BF16)
    sum_bd_b = sum_bd.astype(BF16)
    mean_bd_b = mean_bd.astype(BF16)

    def bdot(a, b):
        return _dot(a.astype(BF16), b.astype(BF16))

    def split2(x):
        hi = x.astype(BF16)
        return hi, (x - hi.astype(F32)).astype(BF16)

    def head_reduce(x, bd_b):
        hi, lo = split2(x)
        return _dot(hi, bd_b) + _dot(lo, bd_b)

    def stack_heads(x):
        return jnp.concatenate([jnp.where(lo, x, 0.0), jnp.where(lo, 0.0, x)], axis=0)

    def fold_heads(x):
        return x[0:c_len] + x[c_len:c2]

    def slot(bi, c):
        return bi * nc + c

    def conv_tokens(ref, w_ref, bi, c, t0):
        x = ref[bi, pl.ds(t0, c_len), :]
        if isinstance(c, int):
            zero_row = jnp.zeros((1, LANES), F32)
            prev = ref[bi, pl.ds(t0 - 1, 1), :] if c > 0 else zero_row
            nxt = ref[bi, pl.ds(t0 + c_len, 1), :] if c < nc - 1 else zero_row
        else:
            prev = ref[bi, pl.ds(jnp.maximum(t0 - 1, 0), 1), :]
            nxt = ref[bi, pl.ds(jnp.minimum(t0 + c_len, t_seq - 1), 1), :]
            prev = jnp.where(c > 0, prev, 0.0)
            nxt = jnp.where(c < nc - 1, nxt, 0.0)
        xm = jnp.where(row_c == 0, prev, pltpu.roll(x, 1, 0))
        xp = jnp.where(row_c == c_len - 1, nxt, pltpu.roll(x, c_len - 1, 0))
        return xm * w_ref[0:1, :] + x * w_ref[1:2, :] + xp * w_ref[2:3, :]

    def chunk_group(chunks):
        n = len(chunks)
        t0s = [c * c_len if isinstance(c, int) else pl.multiple_of(c * c_len, c_len) for _, c in chunks]
        rc = [conv_tokens(r_ref, cwr_ref, bi, c, t0) for (bi, c), t0 in zip(chunks, t0s)]
        kc = [conv_tokens(k_ref, cwk_ref, bi, c, t0) for (bi, c), t0 in zip(chunks, t0s)]
        vc = [conv_tokens(v_ref, cwv_ref, bi, c, t0) for (bi, c), t0 in zip(chunks, t0s)]
        kkr = [k * kk_ref[...] for k in kc]
        kk = [x * lax.rsqrt(head_reduce(x * x, sum_bd_b) + 1e-12) for x in kkr]
        tw = [jnp.tanh(lora_ref[bi, pl.ds(t0, c_len), 0:2 * LORA_W]) for (bi, _), t0 in zip(chunks, t0s)]
        xa = [lora_ref[bi, pl.ds(t0, c_len), 2 * LORA_W:2 * LORA_W + 2 * LORA_A] for (bi, _), t0 in zip(chunks, t0s)]
        v_hs = [stack_heads(v).astype(BF16) for v in vc]
        prob =[(i, d) for i in range(n) for d in range(2)]
        pre_w = [_dot(x.astype(BF16), w2_ref[...]) for x in tw]
        pre_a = [_dot(x.astype(BF16), a2_ref[...]) for x in xa]
        lw = [-DECAY_SCALE * _sigmoid(w0_ref[d:d + 1, :] + pre_w[i][:, LANES * d:LANES * (d + 1)]) for i, d in prob]
        ag = [_sigmoid(a0_ref[d:d + 1, :] + pre_a[i][:, LANES * d:LANES * (d + 1)]) for i, d in prob]
        kd = [kc[i] * (1.0 + (ag[p] - 1.0) * ka_ref[...]) for p, (i, d) in enumerate(prob)]
        kb = [kk[i] * ag[p] for p, (i, d) in enumerate(prob)]
        for i in range(n):
            bon = head_reduce(rc[i] * (kd[2 * i] + kd[2 * i + 1]) * rk_ref[...], sum_bd_b) * vc[i]
            bon_s[chunks[i][0], pl.ds(t0s[i], c_len), :] = bon
        lw_sp = [split2(x) for x in lw]
        zc = jnp.zeros((c_len, LANES), BF16)
        cum_res = [_dot(cum_both, jnp.concatenate(
            [jnp.concatenate([lw_sp[2 * i][0], lw_sp[2 * i][1], zc, zc], axis=1),
             jnp.concatenate([zc, zc, lw_sp[2 * i + 1][0], lw_sp[2 * i + 1][1]], axis=1)], axis=0)) for i in range(n)]
        cs_in = [cum_res[i][:, 2 * LANES * d:2 * LANES * d + LANES] + cum_res[i][:, 2 * LANES * d + LANES:2 * LANES * (d + 1)]
                 for i, d in prob]
        tot = [cs_in[p][c_len - 1:c_len, :] if d == 0 else cs_in[p][0:1, :] for p, (i, d) in enumerate(prob)]
        e_neg = [jnp.exp(-x) for x in cs_in]
        e_rem = [jnp.exp(tot[p] - cs_in[p]) for p in range(len(prob))]
        a_t = [-kk[i] * jnp.exp(cs_in[p] - (lw_sp[p][0].astype(F32) + lw_sp[p][1].astype(F32)))
               for p, (i, d) in enumerate(prob)]
        r_t = [rc[i] * jnp.exp(cs_in[p]) for p, (i, d) in enumerate(prob)]
        a_hs = [stack_heads(x).astype(BF16) for x in a_t]
        lhs = [jnp.concatenate([a_hs[p], stack_heads(r_t[p]).astype(BF16)], axis=0) for p in range(len(prob))]
        rhs = [jnp.concatenate([stack_heads(kd[p] * e_neg[p]), stack_heads(kb[p] * e_neg[p])], axis=0).astype(BF16)
               for p in range(len(prob))]
        m4 = [_dot_nt(a, b) for a, b in zip(lhs, rhs)]
        a_ak = [jnp.where(strict[d], m4[p][0:c2, 0:c2], 0.0).astype(BF16) for p, (i, d) in enumerate(prob)]
        l_ab = [jnp.where(strict[d], m4[p][0:c2, c2:2 * c2], 0.0) for p, (i, d) in enumerate(prob)]
        a_rk = [jnp.where(incl[d], m4[p][c2:2 * c2, 0:c2], 0.0).astype(BF16) for p, (i, d) in enumerate(prob)]
        for p, (i, d) in enumerate(prob):
            arb_s[d, slot(*chunks[i])] = jnp.where(incl[d], m4[p][c2:2 * c2, c2:2 * c2], 0.0).astype(BF16)
        t_inv = [eye2 + x for x in l_ab]
        pw_b = [x.astype(BF16) for x in l_ab]
        pw_b = [_dot(x, x).astype(BF16) for x in pw_b]
        for _ in range(4):
            res = [_dot(x, jnp.concatenate([x, t.astype(BF16)], axis=1)) for x, t in zip(pw_b, t_inv)]
            pw_b = [r[:, 0:c2].astype(BF16) for r in res]
            t_inv = [t + r[:, c2:2 * c2] for t, r in zip(t_inv, res)]
        t_inv = [t + _dot(x, t.astype(BF16)) for x, t in zip(pw_b, t_inv)]
        t_b = [t.astype(BF16) for t in t_inv]
        av = [_dot(a_ak[p], v_hs[i]).astype(BF16) for p, (i, d) in enumerate(prob)]
        ov = [fold_heads(_dot(a_rk[p], v_hs[i])) for p, (i, d) in enumerate(prob)]
        tt = [_dot(t, jnp.concatenate([a, x], axis=1)) for t, a, x in zip(t_b, a_hs, av)]
        ta = [fold_heads(r[:, 0:LANES]) for r in tt]
        tv = [fold_heads(r[:, LANES:2 * LANES]) for r in tt]
        kbt = [jnp.concatenate([kd[p] * e_rem[p], kb[p] * e_rem[p]], axis=0).T.astype(BF16) for p in range(len(prob))]
        zero_c = jnp.zeros((c_len, LANES), BF16)
        upd_rhs = [jnp.concatenate([jnp.concatenate([vc[i].astype(BF16), zero_c], axis=1),
                                    jnp.concatenate([tv[p].astype(BF16), ta[p].astype(BF16)], axis=1)], axis=0)
                   for p, (i, d) in enumerate(prob)]
        nw = [_dot(a, b) for a, b in zip(kbt, upd_rhs)]
        for p, (i, d) in enumerate(prob):
            c = slot(*chunks[i])
            tar_s[d, c] = jnp.concatenate([ta[p], r_t[p]], axis=0).astype(BF16)
            tv_s[d, c] = tv[p]
            ov_s[d, c] = ov[p]
            n_s[d, c] = jnp.where(bd_mask, nw[p][:, 0:LANES], 0.0)
            w_s[d, c] = jnp.where(bd_mask, nw[p][:, LANES:2 * LANES], 0.0).astype(BF16)
            gc_s[d, c] = jnp.broadcast_to(jnp.exp(tot[p]), (LANES, LANES)).T

    bd_mask = sum_bd > 0.5
    group1 = 8
    if nb * nc <= group1:
        chunk_group([(bi, c) for bi in range(nb) for c in range(nc)])
    else:
        assert nb == 1 and nc % group1 == 0

        def phase1(i, carry):
            chunk_group([(0, i * group1 + j) for j in range(group1)])
            return carry

        lax.fori_loop(0, nc // group1, phase1, 0)

    chains = [(bi, d) for bi in range(nb) for d in range(2)]

    def phase2(i, carry):
        cs = [slot(bi, i if d == 0 else nc - 1 - i) for bi, d in chains]
        st_b = [x.astype(BF16) for x in carry]
        for (bi, d), c, x in zip(chains, cs, st_b):
            st_s[d, c] = x
        prod = [_dot(w_s[d, c], x) for (bi, d), c, x in zip(chains, cs, st_b)]
        return tuple(gc_s[d, c] * s + pr + n_s[d, c] for (bi, d), c, s, pr in zip(chains, cs, carry, prod))

    if latent:
        init = tuple(st0_ref[bi, d, 0] for bi, d in chains)
    else:
        init = (jnp.zeros((LANES, LANES), F32),) * len(chains)
    fin = lax.fori_loop(0, nc, phase2, init)
    if sf_ref is not None:
        for (bi, d), st in zip(chains, fin):
            s_vk = st.T
            sf_ref[bi, d, 0] = s_vk[0:R_HEAD, 0:R_HEAD]
            sf_ref[bi, d, 1] = pltpu.roll(s_vk[R_HEAD:2 * R_HEAD, :], R_HEAD, 1)[:, 0:R_HEAD]

    def out_group(chunks):
        prob = [(bi, c, slot(bi, c), d) for bi, c in chunks for d in range(2)]
        both = [_dot(tar_s[d, s], st_s[d, s]) for _, _, s, d in prob]
        u = [both[p][0:c_len] + tv_s[d, s] for p, (_, _, s, d) in enumerate(prob)]
        ou = [_dot(arb_s[d, s], stack_heads(u[p]).astype(BF16)) for p, (_, _, s, d) in enumerate(prob)]
        o = [ov_s[d, s] + both[p][c_len:c2] + fold_heads(ou[p]) for p, (_, _, s, d) in enumerate(prob)]
        of = [o[2 * j] + o[2 * j + 1] for j in range(len(chunks))]
        xc = [x - head_reduce(x, mean_bd_b) for x in of]
        var = [head_reduce(x * x, mean_bd_b) for x in xc]
        for j, (bi, c) in enumerate(chunks):
            t0 = c * c_len if isinstance(c, int) else pl.multiple_of(c * c_len, c_len)
            on = xc[j] * lax.rsqrt(var[j] + GN_EPS) * lng_ref[...] + lnb_ref[...]
            xg = lora_ref[bi, pl.ds(t0, c_len), 2 * LORA_W + 2 * LORA_A:LORA_ALL]
            g_out = _dot(_sigmoid(xg).astype(BF16), g2_ref[...])
            o_ref[bi, pl.ds(t0, c_len), :] = ((on + bon_s[bi, pl.ds(t0, c_len), :]) * g_out).astype(o_ref.dtype)

    group3 = 8
    if nb * nc <= group3:
        out_group([(bi, c) for bi in range(nb) for c in range(nc)])
    else:
        assert nb == 1 and nc % group3 == 0

        def phase3(i, carry):
            out_group([(0, i * group3 + j) for j in range(group3)])
            return carry

        lax.fori_loop(0, nc // group3, phase3, 0)


def _wkv(rkv, lora, wts, st0):
    b, t, _ = rkv.shape
    latent = st0 is not None
    nc = t // WKV_C
    npair = RW // LANES
    nb = 2 if (2 * nc <= 8 and b % 2 == 0) else 1
    kern = functools.partial(_wkv_kernel, t_seq=t, latent=latent, nb=nb)

    def seq_spec(col0):
        return pl.BlockSpec((nb, t, LANES), lambda bi, p: (bi, 0, col0 + p))

    def row_spec(rows, col0=0):
        return pl.BlockSpec((rows, LANES), lambda bi, p: (0, col0 + p))

    in_specs = [
        seq_spec(0), seq_spec(npair), seq_spec(2 * npair),
        pl.BlockSpec((nb, t, LORA_ALL), lambda bi, p: (bi, 0, 0)),
        row_spec(3, 0), row_spec(3, npair), row_spec(3, 2 * npair),
        row_spec(2), row_spec(2),
        pl.BlockSpec((LANES, 2 * LANES), lambda bi, p: (0, p)), pl.BlockSpec((LANES, 2 * LANES), lambda bi, p: (0, p)),
        row_spec(LANES),
        row_spec(1), row_spec(1), row_spec(1), row_spec(1), row_spec(1),
    ]
    args = [rkv, rkv, rkv, lora, wts["conv"], wts["conv"], wts["conv"], wts["w0"], wts["a0"], wts["w2"], wts["a2"],
            wts["g2"], wts["k_k"], wts["k_a"], wts["r_k"], wts["ln_g"], wts["ln_b"]]
    out_specs = [seq_spec(0)]
    out_shape = [jax.ShapeDtypeStruct((b, t, RW), BF16)]
    if latent:
        in_specs.append(pl.BlockSpec((nb, 2, 1, LANES, LANES), lambda bi, p: (bi, 0, p, 0, 0)))
        args.append(st0)
    else:
        out_specs.append(pl.BlockSpec((nb, 2, 2, R_HEAD, R_HEAD), lambda bi, p: (bi, 0, p, 0, 0)))
        out_shape.append(jax.ShapeDtypeStruct((b, 2, R_HEADS, R_HEAD, R_HEAD), F32))
    chunk_rows = pltpu.VMEM((2, nb * nc, WKV_C, LANES), F32)
    chunk_sq = pltpu.VMEM((2, nb * nc, LANES, LANES), F32)
    chunk_sq_b = pltpu.VMEM((2, nb * nc, LANES, LANES), BF16)
    res = pl.pallas_call(
        kern,
        grid=(b // nb, npair),
        in_specs=in_specs,
        out_specs=out_specs,
        out_shape=out_shape,
        scratch_shapes=[chunk_sq_b, chunk_rows, chunk_rows, chunk_sq_b, chunk_sq_b, chunk_sq, chunk_sq, chunk_sq_b,
                        pltpu.VMEM((nb, t, LANES), F32)],
        compiler_params=_cparams(("arbitrary", "arbitrary")),
        name="wkv_lat" if latent else "wkv_ctx",
    )(*args)
    return (res[0], None) if latent else (res[0], res[1])


def _merge_kernel(x_ref, oa_ref, os_ref, or_ref, gates_ref, gt_ref, pa_ref, ps_ref, pr_ref, wo_ref, o_ref,
                  *, row0, rstride, bps):
    row = row0 + (pl.program_id(0) // bps) * rstride if rstride else row0
    d = D_MODEL
    gate = lambda j: _sigmoid(gates_ref[:, j * d:(j + 1) * d].astype(F32))
    merged = (gate(0) * _dot(oa_ref[...].astype(BF16), pa_ref[...])
              + gate(1) * _dot(os_ref[...].astype(BF16), ps_ref[...])
              + gate(2) * _dot(or_ref[...].astype(BF16), pr_ref[...]))
    o_ref[...] = x_ref[...] + _mod_row(gt_ref, row) * _dot(merged.astype(BF16), wo_ref[...])


def _merge(x2, oa, osg, orw, gates, mod_l, pa, ps, pr, wo, row0, rstride, t_seq):
    m, d = x2.shape
    tm = min(TOKEN_TILE, t_seq if rstride else m)
    kern = functools.partial(_merge_kernel, row0=row0, rstride=rstride, bps=max(t_seq // tm, 1))
    tok = lambda w: pl.BlockSpec((tm, w), lambda i: (i, 0))
    full = lambda a: pl.BlockSpec(a.shape, lambda i: (0, 0))
    return pl.pallas_call(
        kern,
        grid=(m // tm,),
        in_specs=[tok(d), tok(ATTN_W), tok(SGU_W), tok(RW), tok(3 * d),
                  pl.BlockSpec((8, d), lambda i: (0, 2)),
                  full(pa), full(ps), full(pr), full(wo)],
        out_specs=tok(d),
        out_shape=jax.ShapeDtypeStruct((m, d), F32),
        compiler_params=_cparams(("arbitrary",)),
        name="merge",
    )(x2, oa, osg, orw, gates, mod_l, pa, ps, pr, wo)


def _moe_kernel(x_ref, sh_ref, sc_ref, gt_ref, g2_ref, rwh_ref, rwl_ref, rb_ref, shg_ref, shu_ref, shd_ref,
                wg_ref, wu_ref, wd_ref, o_ref, h_s, comb_s, acc_s, *, row0, rstride, bps):
    row = row0 + (pl.program_id(0) // bps) * rstride if rstride else row0
    e = pl.program_id(1)
    tm = x_ref.shape[0]

    @pl.when(e == 0)
    def _():
        x = x_ref[...]
        h = x * lax.rsqrt(jnp.mean(x * x, axis=-1, keepdims=True) + EPS) * g2_ref[...]
        h = h * (1.0 + _mod_row(sc_ref, row)) + _mod_row(sh_ref, row)
        hb = h.astype(BF16)
        h_lo = (h - hb.astype(F32)).astype(BF16)
        logits = _dot(hb, rwh_ref[...]) + _dot(hb, rwl_ref[...]) + _dot(h_lo, rwh_ref[...])
        scores = _sigmoid(logits)
        sc_t = scores.T[0:N_EXPERTS]
        sel = (scores + rb_ref[...]).T[0:N_EXPERTS]
        eid = lax.broadcasted_iota(jnp.int32, (N_EXPERTS, tm), 0).astype(F32)
        picked = jnp.zeros((N_EXPERTS, tm), jnp.bool_)
        for _ in range(TOP_K):
            best = jnp.max(sel, axis=0, keepdims=True)
            first = jnp.min(jnp.where(sel == best, eid, float(N_EXPERTS)), axis=0, keepdims=True)
            hit = eid == first
            picked = jnp.logical_or(picked, hit)
            sel = jnp.where(hit, -jnp.inf, sel)
        sw = jnp.where(picked, sc_t, 0.0)
        comb_t = sw / jnp.sum(sw, axis=0, keepdims=True) * ROUTED_SCALE
        comb = jnp.concatenate([comb_t, jnp.zeros((LANES - N_EXPERTS, tm), F32)], axis=0).T
        for b in range(N_EXPERTS // MOE_EB):
            comb_s[b] = comb if b == 0 else pltpu.roll(comb, LANES - MOE_EB * b, 1)
        h_s[...] = hb
        acc_s[...] = _dot((_silu(_dot(hb, shg_ref[...])) * _dot(hb, shu_ref[...])).astype(BF16), shd_ref[...])

    hb = h_s[...]
    comb = comb_s[e]
    hid = []
    for j in range(MOE_EB):
        gu = _dot(hb, jnp.concatenate([wg_ref[j], wu_ref[j]], axis=1))
        cw = jnp.broadcast_to(comb[:, j:j + 1], (tm, D_EXPERT))
        hid.append((_silu(gu[:, 0:D_EXPERT]) * gu[:, D_EXPERT:2 * D_EXPERT] * cw).astype(BF16))
    acc_s[...] += _dot(jnp.concatenate(hid, axis=1), wd_ref[...])

    @pl.when(e == pl.num_programs(1) - 1)
    def _():
        o_ref[...] = x_ref[...] + _mod_row(gt_ref, row) * acc_s[...]


def _moe(x2, mod_l, g2, rw_hi, rw_lo, rb_p, shg, shu, shd, wg_l, wu_l, wd_l, row0, rstride, t_seq):
    m, d = x2.shape
    (wg, layer), (wu, _), (wd, _) = wg_l, wu_l, wd_l
    tm = min(1024, t_seq if rstride else m)
    n_hid = MOE_EB * D_EXPERT
    kern = functools.partial(_moe_kernel, row0=row0, rstride=rstride, bps=max(t_seq // tm, 1))
    full = lambda a: pl.BlockSpec(a.shape, lambda i, e: (0, 0))
    return pl.pallas_call(
        kern,
        grid=(m // tm, N_EXPERTS // MOE_EB),
        in_specs=[
            pl.BlockSpec((tm, d), lambda i, e: (i, 0)),
            pl.BlockSpec((8, d), lambda i, e: (0, 3)),
            pl.BlockSpec((8, d), lambda i, e: (0, 4)),
            pl.BlockSpec((8, d), lambda i, e: (0, 5)),
            full(g2), full(rw_hi), full(rw_lo), full(rb_p), full(shg), full(shu), full(shd),
            pl.BlockSpec((None, MOE_EB, d, D_EXPERT), lambda i, e: (layer, e, 0, 0)),
            pl.BlockSpec((None, MOE_EB, d, D_EXPERT), lambda i, e: (layer, e, 0, 0)),
            pl.BlockSpec((None, n_hid, d), lambda i, e: (layer, e, 0)),
        ],
        out_specs=pl.BlockSpec((tm, d), lambda i, e: (i, 0)),
        out_shape=jax.ShapeDtypeStruct((m, d), F32),
        scratch_shapes=[pltpu.VMEM((tm, d), BF16), pltpu.VMEM((N_EXPERTS // MOE_EB, tm, LANES), F32),
                        pltpu.VMEM((tm, d), F32)],
        compiler_params=_cparams(("arbitrary", "arbitrary")),
        name="moe",
    )(x2, mod_l, mod_l, mod_l, g2, rw_hi, rw_lo, rb_p, shg, shu, shd, wg, wu, wd)


def _layer(x, mod_l, w, row0, rstride, ctx, rope_tabs):
    b, t, d = x.shape
    x2 = x.reshape(b * t, d)
    q, kv, su, sv, rkv, lora, gates = _inproj(x2, mod_l, w["norm1_g"], w["w_in"], row0, rstride, t)
    if ctx is None:
        o_attn, k_new = _attention(q.reshape(b, t, -1), kv.reshape(b, t, -1), w["q_norm"], w["k_norm"], None, None)
        st0 = None
    else:
        o_attn, k_new = _attention(q.reshape(b, t, -1), kv.reshape(b, t, -1), w["q_norm"], w["k_norm"],
                                   ctx[:2], rope_tabs)
        st0 = ctx[2]
    o_sgu = _sgu(su, sv, w["sgu_norm_g"], w["sgu_ws"], w["sgu_bs"])
    o_wkv, s_fin = _wkv(rkv.reshape(b, t, -1), lora.reshape(b, t, -1), w["wkv"], st0)
    x1 = _merge(x2, o_attn.reshape(b * t, -1), o_sgu, o_wkv.reshape(b * t, -1), gates, mod_l,
                w["proj_attn"], w["proj_sgu"], w["proj_rwkv"], w["w_out"], row0, rstride, t)
    x_out = _moe(x1, mod_l, w["norm2_g"], w["router_w_hi"], w["router_w_lo"], w["router_bias"],
                 w["sh_gate"], w["sh_up"], w["sh_down"],
                 w["exp_gate"], w["exp_up"], w["exp_down"], row0, rstride, t)
    return x_out.reshape(b, t, d), (k_new, kv, s_fin)


def _rope_tables(n_tok):
    t = jnp.arange(n_tok)
    row = (t // GRID_W).astype(F32)
    col = (t % GRID_W).astype(F32)
    inv = ROPE_THETA ** (-jnp.arange(N_FREQ, dtype=F32) / N_FREQ)
    ang = jnp.stack([row[:, None] * inv[None, :], col[:, None] * inv[None, :]], axis=1)
    cos, sin = jnp.cos(ang), jnp.sin(ang)
    cos64 = jnp.stack([cos, cos], axis=2).reshape(n_tok, HEAD_DIM)
    sin64 = jnp.stack([-sin, sin], axis=2).reshape(n_tok, HEAD_DIM)
    return jnp.tile(cos64, (1, LANES // HEAD_DIM)), jnp.tile(sin64, (1, LANES // HEAD_DIM))


def _lora_pair_layout(w):
    npair = RW // LANES
    w4 = w.reshape(2, w.shape[1], npair, LANES)
    bd = jnp.einsum("dkpc,de->dkpec", w4, jnp.eye(2, dtype=w.dtype))
    return bd.reshape(2 * w.shape[1], npair * 2 * LANES).astype(BF16)


def _layer_weights(l, mod_w, mod_b, norm1_g, norm2_g, w_in, q_norm, k_norm, sgu_norm_g, sgu_ws, sgu_bs, rwkv_conv,
                   rwkv_w0, rwkv_w2, rwkv_a0, rwkv_a2, rwkv_g2, rwkv_k_k, rwkv_k_a, rwkv_r_k, rwkv_ln_g, rwkv_ln_b,
                   proj_attn, proj_sgu, proj_rwkv, w_out, router_w, router_bias, exp_gate, exp_up, exp_down,
                   sh_gate, sh_up, sh_down):
    d = D_MODEL
    row = lambda a: a.reshape(1, -1)
    pad_e = LANES - N_EXPERTS
    rw_pad = jnp.pad(router_w[l], ((0, 0), (0, pad_e)))
    return {
        "norm1_g": row(norm1_g[l]), "norm2_g": row(norm2_g[l]),
        "w_in": (w_in.astype(BF16), l),
        "q_norm": jnp.tile(row(q_norm[l]), (1, LANES // HEAD_DIM)),
        "k_norm": jnp.tile(row(k_norm[l]), (1, LANES // HEAD_DIM)),
        "sgu_norm_g": row(sgu_norm_g[l]),
        "sgu_ws": sgu_ws[l].astype(BF16),
        "sgu_bs": jnp.broadcast_to(sgu_bs[l][:, :, None], (SGU_GROUPS, CHUNK, LANES)),
        "wkv": {
            "conv": rwkv_conv[l],
            "w0": rwkv_w0[l], "a0": rwkv_a0[l],
            "w2": _lora_pair_layout(rwkv_w2[l]), "a2": _lora_pair_layout(rwkv_a2[l]),
            "g2": rwkv_g2[l].astype(BF16),
            "k_k": row(rwkv_k_k[l]), "k_a": row(rwkv_k_a[l]), "r_k": row(rwkv_r_k[l]),
            "ln_g": row(rwkv_ln_g[l]), "ln_b": row(rwkv_ln_b[l]),
        },
        "proj_attn": proj_attn[l].astype(BF16), "proj_sgu": proj_sgu[l].astype(BF16),
        "proj_rwkv": proj_rwkv[l].astype(BF16), "w_out": w_out[l].astype(BF16),
        "router_w_hi": rw_pad.astype(BF16),
        "router_w_lo": (rw_pad - rw_pad.astype(BF16).astype(F32)).astype(BF16),
        "router_bias": jnp.pad(row(router_bias[l]), ((0, 0), (0, pad_e))),
        "sh_gate": sh_gate[l].astype(BF16), "sh_up": sh_up[l].astype(BF16), "sh_down": sh_down[l].astype(BF16),
        "exp_gate": (exp_gate.astype(BF16), l), "exp_up": (exp_up.astype(BF16), l),
        "exp_down": (exp_down.astype(BF16).reshape(-1, N_EXPERTS * D_EXPERT, d), l),
    }


def _state_to_blockdiag_t(s):
    b = s.shape[0]
    st = jnp.swapaxes(s, -1, -2).reshape(b, 2, R_HEADS // 2, 2, R_HEAD, R_HEAD)
    bd = jnp.einsum("bdpjkv,ji->bdpjkiv", st, jnp.eye(2, dtype=s.dtype))
    return bd.reshape(b, 2, R_HEADS // 2, LANES, LANES)


def kernel(x_prompt, x_sample, cache_k, cache_v, state_wkv, c, c_ctx, mod_w, mod_b, norm1_g, norm2_g, w_in, q_norm, k_norm, sgu_norm_g, sgu_ws, sgu_bs, rwkv_conv, rwkv_w0, rwkv_w2, rwkv_a0, rwkv_a2, rwkv_g2, rwkv_k_k, rwkv_k_a, rwkv_r_k, rwkv_ln_g, rwkv_ln_b, proj_attn, proj_sgu, proj_rwkv, w_out, router_w, router_bias, exp_gate, exp_up, exp_down, sh_gate, sh_up, sh_down):
    params = (mod_w, mod_b, norm1_g, norm2_g, w_in, q_norm, k_norm, sgu_norm_g, sgu_ws, sgu_bs, rwkv_conv,
              rwkv_w0, rwkv_w2, rwkv_a0, rwkv_a2, rwkv_g2, rwkv_k_k, rwkv_k_a, rwkv_r_k, rwkv_ln_g, rwkv_ln_b,
              proj_attn, proj_sgu, proj_rwkv, w_out, router_w, router_bias, exp_gate, exp_up, exp_down,
              sh_gate, sh_up, sh_down)
    n_b, n_t, d = x_prompt.shape
    n_db, n_dt, _ = x_sample.shape
    assert n_db + 1 <= 8
    cmat = jnp.zeros((8, d), F32).at[0].set(c_ctx).at[1:1 + n_db].set(c)
    mod = _modulation(cmat, mod_w, mod_b)
    rope_tabs = _rope_tables(n_dt)
    xp, xs = x_prompt, x_sample
    ks, vs, ss = [], [], []
    for l in range(DEPTH):
        w = _layer_weights(l, *params)
        xp, (k_new, kv, s_fin) = _layer(xp, mod[l], w, 0, 0, None, None)
        ks.append(k_new.reshape(n_b, n_t, N_KV_HEADS, HEAD_DIM))
        vs.append(kv.reshape(n_b, n_t, 2 * KV_W)[:, :, KV_W:].reshape(n_b, n_t, N_KV_HEADS, HEAD_DIM))
        ss.append(s_fin)
        ctx = (cache_k[:, l].reshape(n_db, -1, KV_W), cache_v[:, l].reshape(n_db, -1, KV_W),
               _state_to_blockdiag_t(state_wkv[:, l]))
        xs, _ = _layer(xs, mod[l], w, 1, 1, ctx, rope_tabs)
    return (xp, xs, jnp.stack(ks, axis=1), jnp.stack(vs, axis=1), jnp.stack(ss, axis=1))
```

```python
import functools

import jax
import jax.numpy as jnp
from jax import lax
from jax.experimental import pallas as pl
from jax.experimental.pallas import tpu as pltpu

F32 = jnp.float32
BF16 = jnp.bfloat16
HI = lax.Precision.HIGHEST

D_MODEL = 1024
DEPTH = 2
GRID_W = 64
N_HEADS = 8
N_KV_HEADS = 2
HEAD_DIM = 64
ATTN_W = N_HEADS * HEAD_DIM
KV_W = N_KV_HEADS * HEAD_DIM
N_FREQ = HEAD_DIM // 4
ROPE_THETA = 10000.0
CHUNK = 128
SGU_GROUPS = 4
SGU_W = 512
R_HEADS = 8
R_HEAD = 64
RW = R_HEADS * R_HEAD
LORA_W = 64
LORA_A = 64
LORA_G = 128
DECAY_SCALE = 0.606531
GN_EPS = 64e-5
N_EXPERTS = 64
TOP_K = 6
D_EXPERT = 128
D_SHARED = 256
ROUTED_SCALE = 2.5
EPS = 1e-6
LORA_ALL = 2 * LORA_W + 2 * LORA_A + LORA_G
N_IN = ATTN_W + 2 * KV_W + 2 * SGU_W + 3 * RW + LORA_ALL + 3 * D_MODEL

LOG2_E = 1.4426950408889634
TOKEN_TILE = 512
LANES = 128
WKV_C = 64
MOE_EB = 8
VMEM_LIMIT = 56 * 1024 * 1024


def _cparams(sem, vmem=VMEM_LIMIT):
    return pltpu.CompilerParams(dimension_semantics=sem, vmem_limit_bytes=vmem)


def _dot(a, b, prec=None):
    return jnp.dot(a, b, preferred_element_type=F32, precision=prec)


def _dot_nt(a, b, prec=None):
    return lax.dot_general(a, b, (((1,), (1,)), ((), ())), preferred_element_type=F32, precision=prec)


def _sigmoid(x):
    return 1.0 / (1.0 + jnp.exp(-x))


def _silu(x):
    return x / (1.0 + jnp.exp(-x))


def _head_block_diag(value):
    r = lax.broadcasted_iota(jnp.int32, (LANES, LANES), 0) // HEAD_DIM
    c = lax.broadcasted_iota(jnp.int32, (LANES, LANES), 1) // HEAD_DIM
    return jnp.where(r == c, value, 0.0).astype(F32)


def _mod_row(ref, row):
    return ref[pl.ds(row, 1), :]


def _mod_kernel(c_ref, w_ref, b_ref, o_ref):
    o_ref[0] = _dot(_silu(c_ref[...]), w_ref[0], HI) + b_ref[0]


def _modulation(cmat, mod_w, mod_b):
    n_l, d, n6 = mod_w.shape
    tn = n6 // 4
    return pl.pallas_call(
        _mod_kernel,
        grid=(n_l, n6 // tn),
        in_specs=[
            pl.BlockSpec((8, d), lambda l, j: (0, 0)),
            pl.BlockSpec((1, d, tn), lambda l, j: (l, 0, j)),
            pl.BlockSpec((1, 1, tn), lambda l, j: (l, 0, j)),
        ],
        out_specs=pl.BlockSpec((1, 8, tn), lambda l, j: (l, 0, j)),
        out_shape=jax.ShapeDtypeStruct((n_l, 8, n6), F32),
        compiler_params=_cparams(("arbitrary", "arbitrary")),
        name="modulation",
    )(cmat, mod_w, mod_b.reshape(n_l, 1, n6))


_IN_SPLIT = (ATTN_W, 2 * KV_W, SGU_W, SGU_W, 3 * RW, LORA_ALL, 3 * D_MODEL)


def _inproj_kernel(x_ref, sh_ref, sc_ref, g_ref, w_ref, *out_refs, row0, rstride, bps):
    row = row0 + (pl.program_id(0) // bps) * rstride if rstride else row0
    x = x_ref[...]
    h = x * lax.rsqrt(jnp.mean(x * x, axis=-1, keepdims=True) + EPS) * g_ref[...]
    h = (h * (1.0 + _mod_row(sc_ref, row)) + _mod_row(sh_ref, row)).astype(BF16)
    off = 0
    for ref, size in zip(out_refs, _IN_SPLIT):
        ref[...] = _dot(h, w_ref[:, off:off + size]).astype(ref.dtype)
        off += size


def _inproj(x2, mod_l, g1, w_in_layers, row0, rstride, t_seq):
    m, d = x2.shape
    w_in_b, layer = w_in_layers
    tm = min(TOKEN_TILE, t_seq if rstride else m)
    kern = functools.partial(_inproj_kernel, row0=row0, rstride=rstride, bps=max(t_seq // tm, 1))
    return pl.pallas_call(
        kern,
        grid=(m // tm,),
        in_specs=[
            pl.BlockSpec((tm, d), lambda i: (i, 0)),
            pl.BlockSpec((8, d), lambda i: (0, 0)),
            pl.BlockSpec((8, d), lambda i: (0, 1)),
            pl.BlockSpec((1, d), lambda i: (0, 0)),
            pl.BlockSpec((None, d, N_IN), lambda i: (layer, 0, 0), pipeline_mode=pl.Buffered(1)),
        ],
        out_specs=[pl.BlockSpec((tm, s), lambda i: (i, 0)) for s in _IN_SPLIT],
        out_shape=[jax.ShapeDtypeStruct((m, s), BF16 if s == 3 * D_MODEL else F32) for s in _IN_SPLIT],
        compiler_params=_cparams(("arbitrary",)),
        name="inproj",
    )(x2, mod_l, mod_l, g1, w_in_b)


def _attn_kernel(*refs, t_seq, past, tq, latent):
    if latent:
        (q_ref, kv_ref, ck_ref, cv_ref, qg_ref, kg_ref, cq_ref, sq_ref, ckk_ref, skk_ref,
         o_ref, k_s, vt_s) = refs
    else:
        q_ref, kv_ref, qg_ref, kg_ref, o_ref, kn_ref, k_s, vt_s = refs
    lane = lax.broadcasted_iota(jnp.int32, (1, LANES), 1)
    first_half = ((lane // N_FREQ) % 2) == 0
    mean_bd = _head_block_diag(1.0 / HEAD_DIM).astype(BF16)
    q_per_kv = N_HEADS // N_KV_HEADS

    def head_norm(x, g):
        sq = x * x
        hi = sq.astype(BF16)
        lo = (sq - hi.astype(F32)).astype(BF16)
        return x * lax.rsqrt(_dot(hi, mean_bd) + _dot(lo, mean_bd) + EPS) * g

    def rope(x, cos, sin_signed):
        rot = jnp.where(first_half, pltpu.roll(x, LANES - N_FREQ, 1), pltpu.roll(x, N_FREQ, 1))
        return x * cos + rot * sin_signed

    @pl.when(pl.program_id(1) == 0)
    def _():
        rb = 256
        for i in range(past // rb):
            r0 = i * rb
            k_s[r0:r0 + rb, :] = ck_ref[0, r0:r0 + rb, :].astype(BF16)
            vt_s[:, r0:r0 + rb] = cv_ref[0, r0:r0 + rb, :].T.astype(BF16)
        for i in range(t_seq // rb):
            r0 = i * rb
            kn = head_norm(kv_ref[0, r0:r0 + rb, 0:KV_W], kg_ref[...])
            if latent:
                kn = rope(kn, ckk_ref[r0:r0 + rb, :], skk_ref[r0:r0 + rb, :])
            else:
                kn_ref[0, r0:r0 + rb, :] = kn
            k_s[past + r0:past + r0 + rb, :] = kn.astype(BF16)
            vt_s[:, past + r0:past + r0 + rb] = kv_ref[0, r0:r0 + rb, KV_W:2 * KV_W].T.astype(BF16)

    slabs = []
    for j in range(ATTN_W // LANES):
        qn = head_norm(q_ref[0, :, LANES * j:LANES * (j + 1)], qg_ref[...])
        if latent:
            qn = rope(qn, cq_ref[...], sq_ref[...])
        slabs.append(qn * (HEAD_DIM ** -0.5 * LOG2_E))
    q_t = jnp.concatenate(slabs, axis=1).T.astype(BF16)
    zero_h = jnp.zeros((HEAD_DIM, tq), BF16)
    qz = []
    for g in range(N_KV_HEADS):
        cols = []
        for h in range(q_per_kv * g, q_per_kv * (g + 1)):
            q_h = q_t[HEAD_DIM * h:HEAD_DIM * (h + 1), :]
            cols.append(jnp.concatenate([q_h, zero_h] if g == 0 else [zero_h, q_h], axis=0))
        qz.append(jnp.concatenate(cols, axis=1))
    s_t = [_dot(k_s[...], x) for x in qz]
    p_t = [jnp.exp2(x - jnp.max(x, axis=0, keepdims=True)) for x in s_t]
    den = [jnp.sum(x, axis=0, keepdims=True) for x in p_t]
    o_t = [_dot(vt_s[HEAD_DIM * g:HEAD_DIM * (g + 1), :], p_t[g].astype(BF16)) / den[g]
           for g in range(N_KV_HEADS)]
    outs =[o_t[g][:, tq * h:tq * (h + 1)] for g in range(N_KV_HEADS) for h in range(q_per_kv)]
    o_ref[0] = jnp.concatenate(outs, axis=0).T.astype(o_ref.dtype)


def _attention(q, kv, qg, kg, ctx_kv, rope_tabs):
    b, t, _ = q.shape
    latent = ctx_kv is not None
    past = ctx_kv[0].shape[1] if latent else 0
    s_all = past + t
    tq = 128 if latent else min(256, t)
    kern = functools.partial(_attn_kernel, t_seq=t, past=past, tq=tq, latent=latent)
    in_specs = [
        pl.BlockSpec((1, tq, ATTN_W), lambda bi, qi: (bi, qi, 0)),
        pl.BlockSpec((1, t, 2 * KV_W), lambda bi, qi: (bi, 0, 0)),
    ]
    args = [q, kv]
    if latent:
        in_specs += [pl.BlockSpec((1, past, KV_W), lambda bi, qi: (bi, 0, 0))] * 2
        args += list(ctx_kv)
    in_specs += [pl.BlockSpec((1, LANES), lambda bi, qi: (0, 0))] * 2
    args += [qg, kg]
    out_specs = [pl.BlockSpec((1, tq, ATTN_W), lambda bi, qi: (bi, qi, 0))]
    out_shape = [jax.ShapeDtypeStruct((b, t, ATTN_W), BF16)]
    if latent:
        cos_t, sin_t = rope_tabs
        in_specs += [pl.BlockSpec((tq, LANES), lambda bi, qi: (qi, 0))] * 2
        in_specs += [pl.BlockSpec((t, LANES), lambda bi, qi: (0, 0))] * 2
        args += [cos_t, sin_t, cos_t, sin_t]
    else:
        out_specs.append(pl.BlockSpec((1, t, KV_W), lambda bi, qi: (bi, 0, 0)))
        out_shape.append(jax.ShapeDtypeStruct((b, t, KV_W), F32))
    res = pl.pallas_call(
        kern,
        grid=(b, t // tq),
        in_specs=in_specs,
        out_specs=out_specs,
        out_shape=out_shape,
        scratch_shapes=[pltpu.VMEM((s_all, LANES), BF16), pltpu.VMEM((LANES, s_all), BF16)],
        compiler_params=_cparams(("arbitrary", "arbitrary")),
        name="attention_lat" if latent else "attention_ctx",
    )(*args)
    return res if not latent else (res[0], None)


def _wkv_kernel(*refs, t_seq, latent, nb):
    (r_ref, k_ref, v_ref, lora_ref, cwr_ref, cwk_ref, cwv_ref, w0_ref, a0_ref, w2_ref, a2_ref, g2_ref,
     kk_ref, ka_ref, rk_ref, lng_ref, lnb_ref) = refs[:17]
    rest = refs[17:]
    if latent:
        st0_ref, o_ref = rest[:2]
        scr = rest[2:]
        sf_ref = None
    else:
        o_ref, sf_ref = rest[:2]
        scr = rest[2:]
        st0_ref = None
    tar_s, tv_s, ov_s, arb_s, w_s, n_s, gc_s, st_s, bon_s = scr
    c_len = WKV_C
    c2 = 2 * c_len
    nc = t_seq // c_len

    lane = lax.broadcasted_iota(jnp.int32, (1, LANES), 1)
    lo = lane < R_HEAD
    sum_bd = _head_block_diag(1.0)
    mean_bd = _head_block_diag(1.0 / R_HEAD)
    row_c = lax.broadcasted_iota(jnp.int32, (c_len, LANES), 0)
    ri = lax.broadcasted_iota(jnp.int32, (c2, c2), 0)
    ci = lax.broadcasted_iota(jnp.int32, (c2, c2), 1)
    same_head = (ri // c_len) == (ci // c_len)
    tt, ss = ri % c_len, ci % c_len
    eye2 = jnp.where(ri == ci, 1.0, 0.0).astype(F32)
    strict = (same_head & (ss < tt), same_head & (ss > tt))
    incl = (same_head & (ss <= tt), same_head & (ss >= tt))
    tj = lax.broadcasted_iota(jnp.int32, (c_len, c2), 0)
    sj = lax.broadcasted_iota(jnp.int32, (c_len, c2), 1)
    cum_both = jnp.where(((sj < c_len) & (sj <= tj)) | ((sj >= c_len) & (sj - c_len >= tj)), 1.0, 0.0).astype(BF16)
    sum_bd_b = sum_bd.astype(BF16)
    mean_bd_b = mean_bd.astype(BF16)

    def bdot(a, b):
        return _dot(a.astype(BF16), b.astype(BF16))

    def split2(x):
        hi = x.astype(BF16)
        return hi, (x - hi.astype(F32)).astype(BF16)

    def head_reduce(x, bd_b):
        hi, lo = split2(x)
        return _dot(hi, bd_b) + _dot(lo, bd_b)

    def stack_heads(x):
        return jnp.concatenate([jnp.where(lo, x, 0.0), jnp.where(lo, 0.0, x)], axis=0)

    def fold_heads(x):
        return x[0:c_len] + x[c_len:c2]

    def slot(bi, c):
        return bi * nc + c

    def conv_tokens(ref, w_ref, bi, c, t0):
        x = ref[bi, pl.ds(t0, c_len), :]
        if isinstance(c, int):
            zero_row = jnp.zeros((1, LANES), F32)
            prev = ref[bi, pl.ds(t0 - 1, 1), :] if c > 0 else zero_row
            nxt = ref[bi, pl.ds(t0 + c_len, 1), :] if c < nc - 1 else zero_row
        else:
            prev = ref[bi, pl.ds(jnp.maximum(t0 - 1, 0), 1), :]
            nxt = ref[bi, pl.ds(jnp.minimum(t0 + c_len, t_seq - 1), 1), :]
            prev = jnp.where(c > 0, prev, 0.0)
            nxt = jnp.where(c < nc - 1, nxt, 0.0)
        xm = jnp.where(row_c == 0, prev, pltpu.roll(x, 1, 0))
        xp = jnp.where(row_c == c_len - 1, nxt, pltpu.roll(x, c_len - 1, 0))
        return xm * w_ref[0:1, :] + x * w_ref[1:2, :] + xp * w_ref[2:3, :]

    def chunk_group(chunks):
        n = len(chunks)
        t0s = [c * c_len if isinstance(c, int) else pl.multiple_of(c * c_len, c_len) for _, c in chunks]
        rc = [conv_tokens(r_ref, cwr_ref, bi, c, t0) for (bi, c), t0 in zip(chunks, t0s)]
        kc = [conv_tokens(k_ref, cwk_ref, bi, c, t0) for (bi, c), t0 in zip(chunks, t0s)]
        vc = [conv_tokens(v_ref, cwv_ref, bi, c, t0) for (bi, c), t0 in zip(chunks, t0s)]
        kkr = [k * kk_ref[...] for k in kc]
        kk = [x * lax.rsqrt(head_reduce(x * x, sum_bd_b) + 1e-12) for x in kkr]
        tw = [jnp.tanh(lora_ref[bi, pl.ds(t0, c_len), 0:2 * LORA_W]) for (bi, _), t0 in zip(chunks, t0s)]
        xa = [lora_ref[bi, pl.ds(t0, c_len), 2 * LORA_W:2 * LORA_W + 2 * LORA_A] for (bi, _), t0 in zip(chunks, t0s)]
        v_hs = [stack_heads(v).astype(BF16) for v in vc]
        prob =[(i, d) for i in range(n) for d in range(2)]
        pre_w = [_dot(x.astype(BF16), w2_ref[...]) for x in tw]
        pre_a = [_dot(x.astype(BF16), a2_ref[...]) for x in xa]
        lw = [-DECAY_SCALE * _sigmoid(w0_ref[d:d + 1, :] + pre_w[i][:, LANES * d:LANES * (d + 1)]) for i, d in prob]
        ag = [_sigmoid(a0_ref[d:d + 1, :] + pre_a[i][:, LANES * d:LANES * (d + 1)]) for i, d in prob]
        kd = [kc[i] * (1.0 + (ag[p] - 1.0) * ka_ref[...]) for p, (i, d) in enumerate(prob)]
        kb = [kk[i] * ag[p] for p, (i, d) in enumerate(prob)]
        for i in range(n):
            bon = head_reduce(rc[i] * (kd[2 * i] + kd[2 * i + 1]) * rk_ref[...], sum_bd_b) * vc[i]
            bon_s[chunks[i][0], pl.ds(t0s[i], c_len), :] = bon
        lw_sp = [split2(x) for x in lw]
        zc = jnp.zeros((c_len, LANES), BF16)
        cum_res = [_dot(cum_both, jnp.concatenate(
            [jnp.concatenate([lw_sp[2 * i][0], lw_sp[2 * i][1], zc, zc], axis=1),
             jnp.concatenate([zc, zc, lw_sp[2 * i + 1][0], lw_sp[2 * i + 1][1]], axis=1)], axis=0)) for i in range(n)]
        cs_in = [cum_res[i][:, 2 * LANES * d:2 * LANES * d + LANES] + cum_res[i][:, 2 * LANES * d + LANES:2 * LANES * (d + 1)]
                 for i, d in prob]
        tot = [cs_in[p][c_len - 1:c_len, :] if d == 0 else cs_in[p][0:1, :] for p, (i, d) in enumerate(prob)]
        e_neg = [jnp.exp(-x) for x in cs_in]
        e_rem = [jnp.exp(tot[p] - cs_in[p]) for p in range(len(prob))]
        a_t = [-kk[i] * jnp.exp(cs_in[p] - (lw_sp[p][0].astype(F32) + lw_sp[p][1].astype(F32)))
               for p, (i, d) in enumerate(prob)]
        r_t = [rc[i] * jnp.exp(cs_in[p]) for p, (i, d) in enumerate(prob)]
        a_hs = [stack_heads(x).astype(BF16) for x in a_t]
        lhs = [jnp.concatenate([a_hs[p], stack_heads(r_t[p]).astype(BF16)], axis=0) for p in range(len(prob))]
        rhs = [jnp.concatenate([stack_heads(kd[p] * e_neg[p]), stack_heads(kb[p] * e_neg[p])], axis=0).astype(BF16)
               for p in range(len(prob))]
        m4 = [_dot_nt(a, b) for a, b in zip(lhs, rhs)]
        a_ak = [jnp.where(strict[d], m4[p][0:c2, 0:c2], 0.0).astype(BF16) for p, (i, d) in enumerate(prob)]
        l_ab = [jnp.where(strict[d], m4[p][0:c2, c2:2 * c2], 0.0) for p, (i, d) in enumerate(prob)]
        a_rk = [jnp.where(incl[d], m4[p][c2:2 * c2, 0:c2], 0.0).astype(BF16) for p, (i, d) in enumerate(prob)]
        for p, (i, d) in enumerate(prob):
            arb_s[d, slot(*chunks[i])] = jnp.where(incl[d], m4[p][c2:2 * c2, c2:2 * c2], 0.0).astype(BF16)
        t_inv = [eye2 + x for x in l_ab]
        pw_b = [x.astype(BF16) for x in l_ab]
        pw_b = [_dot(x, x).astype(BF16) for x in pw_b]
        for _ in range(4):
            res = [_dot(x, jnp.concatenate([x, t.astype(BF16)], axis=1)) for x, t in zip(pw_b, t_inv)]
            pw_b = [r[:, 0:c2].astype(BF16) for r in res]
            t_inv = [t + r[:, c2:2 * c2] for t, r in zip(t_inv, res)]
        t_inv = [t + _dot(x, t.astype(BF16)) for x, t in zip(pw_b, t_inv)]
        t_b = [t.astype(BF16) for t in t_inv]
        av = [_dot(a_ak[p], v_hs[i]).astype(BF16) for p, (i, d) in enumerate(prob)]
        ov = [fold_heads(_dot(a_rk[p], v_hs[i])) for p, (i, d) in enumerate(prob)]
        tt = [_dot(t, jnp.concatenate([a, x], axis=1)) for t, a, x in zip(t_b, a_hs, av)]
        ta = [fold_heads(r[:, 0:LANES]) for r in tt]
        tv = [fold_heads(r[:, LANES:2 * LANES]) for r in tt]
        kbt = [jnp.concatenate([kd[p] * e_rem[p], kb[p] * e_rem[p]], axis=0).T.astype(BF16) for p in range(len(prob))]
        zero_c = jnp.zeros((c_len, LANES), BF16)
        upd_rhs = [jnp.concatenate([jnp.concatenate([vc[i].astype(BF16), zero_c], axis=1),
                                    jnp.concatenate([tv[p].astype(BF16), ta[p].astype(BF16)], axis=1)], axis=0)
                   for p, (i, d) in enumerate(prob)]
        nw = [_dot(a, b) for a, b in zip(kbt, upd_rhs)]
        for p, (i, d) in enumerate(prob):
            c = slot(*chunks[i])
            tar_s[d, c] = jnp.concatenate([ta[p], r_t[p]], axis=0).astype(BF16)
            tv_s[d, c] = tv[p]
            ov_s[d, c] = ov[p]
            n_s[d, c] = jnp.where(bd_mask, nw[p][:, 0:LANES], 0.0)
            w_s[d, c] = jnp.where(bd_mask, nw[p][:, LANES:2 * LANES], 0.0).astype(BF16)
            gc_s[d, c] = jnp.broadcast_to(jnp.exp(tot[p]), (LANES, LANES)).T

    bd_mask = sum_bd > 0.5
    group1 = 8
    if nb > 1 or nc <= group1:
        every = [(bi, c) for bi in range(nb) for c in range(nc)]
        for g0 in range(0, len(every), group1):
            chunk_group(every[g0:g0 + group1])
    else:
        assert nb == 1 and nc % group1 == 0

        def phase1(i, carry):
            chunk_group([(0, i * group1 + j) for j in range(group1)])
            return carry

        lax.fori_loop(0, nc // group1, phase1, 0)

    chains = [(bi, d) for bi in range(nb) for d in range(2)]

    def phase2(i, carry):
        cs = [slot(bi, i if d == 0 else nc - 1 - i) for bi, d in chains]
        st_b = [x.astype(BF16) for x in carry]
        for (bi, d), c, x in zip(chains, cs, st_b):
            st_s[d, c] = x
        prod = [_dot(w_s[d, c], x) for (bi, d), c, x in zip(chains, cs, st_b)]
        return tuple(gc_s[d, c] * s + pr + n_s[d, c] for (bi, d), c, s, pr in zip(chains, cs, carry, prod))

    if latent:
        init = tuple(st0_ref[bi, d, 0] for bi, d in chains)
    else:
        init = (jnp.zeros((LANES, LANES), F32),) * len(chains)
    fin = lax.fori_loop(0, nc, phase2, init)
    if sf_ref is not None:
        for (bi, d), st in zip(chains, fin):
            s_vk = st.T
            sf_ref[bi, d, 0] = s_vk[0:R_HEAD, 0:R_HEAD]
            sf_ref[bi, d, 1] = pltpu.roll(s_vk[R_HEAD:2 * R_HEAD, :], R_HEAD, 1)[:, 0:R_HEAD]

    def out_group(chunks):
        prob = [(bi, c, slot(bi, c), d) for bi, c in chunks for d in range(2)]
        both = [_dot(tar_s[d, s], st_s[d, s]) for _, _, s, d in prob]
        u = [both[p][0:c_len] + tv_s[d, s] for p, (_, _, s, d) in enumerate(prob)]
        ou = [_dot(arb_s[d, s], stack_heads(u[p]).astype(BF16)) for p, (_, _, s, d) in enumerate(prob)]
        o = [ov_s[d, s] + both[p][c_len:c2] + fold_heads(ou[p]) for p, (_, _, s, d) in enumerate(prob)]
        of = [o[2 * j] + o[2 * j + 1] for j in range(len(chunks))]
        xc = [x - head_reduce(x, mean_bd_b) for x in of]
        var = [head_reduce(x * x, mean_bd_b) for x in xc]
        for j, (bi, c) in enumerate(chunks):
            t0 = c * c_len if isinstance(c, int) else pl.multiple_of(c * c_len, c_len)
            on = xc[j] * lax.rsqrt(var[j] + GN_EPS) * lng_ref[...] + lnb_ref[...]
            xg = lora_ref[bi, pl.ds(t0, c_len), 2 * LORA_W + 2 * LORA_A:LORA_ALL]
            g_out = _dot(_sigmoid(xg).astype(BF16), g2_ref[...])
            o_ref[bi, pl.ds(t0, c_len), :] = ((on + bon_s[bi, pl.ds(t0, c_len), :]) * g_out).astype(o_ref.dtype)

    group3 = 8
    if nb > 1 or nc <= group3:
        every = [(bi, c) for bi in range(nb) for c in range(nc)]
        for g0 in range(0, len(every), group3):
            out_group(every[g0:g0 + group3])
    else:
        assert nb == 1 and nc % group3 == 0

        def phase3(i, carry):
            out_group([(0, i * group3 + j) for j in range(group3)])
            return carry

        lax.fori_loop(0, nc // group3, phase3, 0)


def _wkv(rkv, lora, wts, st0):
    b, t, _ = rkv.shape
    latent = st0 is not None
    nc = t // WKV_C
    npair = RW // LANES
    nb = 4 if (4 * nc <= 16 and b % 4 == 0) else 1
    kern = functools.partial(_wkv_kernel, t_seq=t, latent=latent, nb=nb)

    def seq_spec(col0):
        return pl.BlockSpec((nb, t, LANES), lambda bi, p: (bi, 0, col0 + p))

    def row_spec(rows, col0=0):
        return pl.BlockSpec((rows, LANES), lambda bi, p: (0, col0 + p))

    in_specs = [
        seq_spec(0), seq_spec(npair), seq_spec(2 * npair),
        pl.BlockSpec((nb, t, LORA_ALL), lambda bi, p: (bi, 0, 0)),
        row_spec(3, 0), row_spec(3, npair), row_spec(3, 2 * npair),
        row_spec(2), row_spec(2),
        pl.BlockSpec((LANES, 2 * LANES), lambda bi, p: (0, p)), pl.BlockSpec((LANES, 2 * LANES), lambda bi, p: (0, p)),
        row_spec(LANES),
        row_spec(1), row_spec(1), row_spec(1), row_spec(1), row_spec(1),
    ]
    args = [rkv, rkv, rkv, lora, wts["conv"], wts["conv"], wts["conv"], wts["w0"], wts["a0"], wts["w2"], wts["a2"],
            wts["g2"], wts["k_k"], wts["k_a"], wts["r_k"], wts["ln_g"], wts["ln_b"]]
    out_specs = [seq_spec(0)]
    out_shape = [jax.ShapeDtypeStruct((b, t, RW), BF16)]
    if latent:
        in_specs.append(pl.BlockSpec((nb, 2, 1, LANES, LANES), lambda bi, p: (bi, 0, p, 0, 0)))
        args.append(st0)
    else:
        out_specs.append(pl.BlockSpec((nb, 2, 2, R_HEAD, R_HEAD), lambda bi, p: (bi, 0, p, 0, 0)))
        out_shape.append(jax.ShapeDtypeStruct((b, 2, R_HEADS, R_HEAD, R_HEAD), F32))
    chunk_rows = pltpu.VMEM((2, nb * nc, WKV_C, LANES), F32)
    chunk_sq = pltpu.VMEM((2, nb * nc, LANES, LANES), F32)
    chunk_sq_b = pltpu.VMEM((2, nb * nc, LANES, LANES), BF16)
    res = pl.pallas_call(
        kern,
        grid=(b // nb, npair),
        in_specs=in_specs,
        out_specs=out_specs,
        out_shape=out_shape,
        scratch_shapes=[chunk_sq_b, chunk_rows, chunk_rows, chunk_sq_b, chunk_sq_b, chunk_sq, chunk_sq, chunk_sq_b,
                        pltpu.VMEM((nb, t, LANES), F32)],
        compiler_params=_cparams(("arbitrary", "arbitrary")),
        name="wkv_lat" if latent else "wkv_ctx",
    )(*args)
    return (res[0], None) if latent else (res[0], res[1])


def _merge_kernel(x_ref, oa_ref, su_ref, sv_ref, or_ref, gates_ref, gt_ref, sg_ref, ws_ref, bs_ref,
                  pa_ref, ps_ref, pr_ref, wo_ref, o_ref, *, row0, rstride, bps):
    row = row0 + (pl.program_id(0) // bps) * rstride if rstride else row0
    d = D_MODEL
    sgu_rows = []
    for c in range(x_ref.shape[0] // CHUNK):
        rows = slice(CHUNK * c, CHUNK * (c + 1))
        v = sv_ref[rows, :]
        vn = (v * lax.rsqrt(jnp.mean(v * v, axis=-1, keepdims=True) + EPS) * sg_ref[...]).astype(BF16)
        sgu_rows.append(jnp.concatenate(
            [(su_ref[rows, LANES * g:LANES * (g + 1)]
              * (_dot(ws_ref[g], vn[:, LANES * g:LANES * (g + 1)]) + bs_ref[g])).astype(BF16)
             for g in range(SGU_GROUPS)], axis=1))
    o_sgu = jnp.concatenate(sgu_rows, axis=0)
    gate = lambda j: _sigmoid(gates_ref[:, j * d:(j + 1) * d].astype(F32))
    merged = (gate(0) * _dot(oa_ref[...].astype(BF16), pa_ref[...])
              + gate(1) * _dot(o_sgu, ps_ref[...])
              + gate(2) * _dot(or_ref[...].astype(BF16), pr_ref[...]))
    o_ref[...] = x_ref[...] + _mod_row(gt_ref, row) * _dot(merged.astype(BF16), wo_ref[...])


def _merge(x2, oa, su, sv, orw, gates, mod_l, sgu_g, sgu_ws, sgu_bs, pa, ps, pr, wo, row0, rstride, t_seq):
    m, d = x2.shape
    tm = min(TOKEN_TILE, t_seq if rstride else m)
    assert tm % CHUNK == 0 and t_seq % CHUNK == 0
    kern = functools.partial(_merge_kernel, row0=row0, rstride=rstride, bps=max(t_seq // tm, 1))
    tok = lambda w: pl.BlockSpec((tm, w), lambda i: (i, 0))
    full = lambda a: pl.BlockSpec(a.shape, lambda i: (0,) * a.ndim)
    return pl.pallas_call(
        kern,
        grid=(m // tm,),
        in_specs=[tok(d), tok(ATTN_W), tok(SGU_W), tok(SGU_W), tok(RW), tok(3 * d),
                  pl.BlockSpec((8, d), lambda i: (0, 2)),
                  full(sgu_g), full(sgu_ws), full(sgu_bs), full(pa), full(ps), full(pr), full(wo)],
        out_specs=tok(d),
        out_shape=jax.ShapeDtypeStruct((m, d), F32),
        compiler_params=_cparams(("arbitrary",)),
        name="merge",
    )(x2, oa, su, sv, orw, gates, mod_l, sgu_g, sgu_ws, sgu_bs, pa, ps, pr, wo)


def _moe_kernel(x_ref, sh_ref, sc_ref, gt_ref, g2_ref, rwh_ref, rwl_ref, rb_ref, shg_ref, shu_ref, shd_ref,
                wg_ref, wu_ref, wd_ref, o_ref, h_s, comb_s, acc_s, *, row0, rstride, bps):
    row = row0 + (pl.program_id(0) // bps) * rstride if rstride else row0
    e = pl.program_id(1)
    tm = x_ref.shape[0]

    @pl.when(e == 0)
    def _():
        x = x_ref[...]
        h = x * lax.rsqrt(jnp.mean(x * x, axis=-1, keepdims=True) + EPS) * g2_ref[...]
        h = h * (1.0 + _mod_row(sc_ref, row)) + _mod_row(sh_ref, row)
        hb = h.astype(BF16)
        h_lo = (h - hb.astype(F32)).astype(BF16)
        logits = _dot(hb, rwh_ref[...]) + _dot(hb, rwl_ref[...]) + _dot(h_lo, rwh_ref[...])
        scores = _sigmoid(logits)
        sc_t = scores.T[0:N_EXPERTS]
        sel = (scores + rb_ref[...]).T[0:N_EXPERTS]
        eid = lax.broadcasted_iota(jnp.int32, (N_EXPERTS, tm), 0).astype(F32)
        picked = jnp.zeros((N_EXPERTS, tm), jnp.bool_)
        for _ in range(TOP_K):
            best = jnp.max(sel, axis=0, keepdims=True)
            first = jnp.min(jnp.where(sel == best, eid, float(N_EXPERTS)), axis=0, keepdims=True)
            hit = eid == first
            picked = jnp.logical_or(picked, hit)
            sel = jnp.where(hit, -jnp.inf, sel)
        sw = jnp.where(picked, sc_t, 0.0)
        comb_t = sw / jnp.sum(sw, axis=0, keepdims=True) * ROUTED_SCALE
        comb = jnp.concatenate([comb_t, jnp.zeros((LANES - N_EXPERTS, tm), F32)], axis=0).T
        for b in range(N_EXPERTS // MOE_EB):
            comb_s[b] = comb if b == 0 else pltpu.roll(comb, LANES - MOE_EB * b, 1)
        h_s[...] = hb
        acc_s[...] = _dot((_silu(_dot(hb, shg_ref[...])) * _dot(hb, shu_ref[...])).astype(BF16), shd_ref[...])

    hb = h_s[...]
    comb = comb_s[e]
    hid = []
    for j in range(MOE_EB):
        gu = _dot(hb, jnp.concatenate([wg_ref[j], wu_ref[j]], axis=1))
        cw = jnp.broadcast_to(comb[:, j:j + 1], (tm, D_EXPERT))
        hid.append((_silu(gu[:, 0:D_EXPERT]) * gu[:, D_EXPERT:2 * D_EXPERT] * cw).astype(BF16))
    acc_s[...] += _dot(jnp.concatenate(hid, axis=1), wd_ref[...])

    @pl.when(e == pl.num_programs(1) - 1)
    def _():
        o_ref[...] = x_ref[...] + _mod_row(gt_ref, row) * acc_s[...]


def _moe(x2, mod_l, g2, rw_hi, rw_lo, rb_p, shg, shu, shd, wg_l, wu_l, wd_l, row0, rstride, t_seq):
    m, d = x2.shape
    (wg, layer), (wu, _), (wd, _) = wg_l, wu_l, wd_l
    tm = min(1024, t_seq if rstride else m)
    n_hid = MOE_EB * D_EXPERT
    kern = functools.partial(_moe_kernel, row0=row0, rstride=rstride, bps=max(t_seq // tm, 1))
    full = lambda a: pl.BlockSpec(a.shape, lambda i, e: (0, 0))
    return pl.pallas_call(
        kern,
        grid=(m // tm, N_EXPERTS // MOE_EB),
        in_specs=[
            pl.BlockSpec((tm, d), lambda i, e: (i, 0)),
            pl.BlockSpec((8, d), lambda i, e: (0, 3)),
            pl.BlockSpec((8, d), lambda i, e: (0, 4)),
            pl.BlockSpec((8, d), lambda i, e: (0, 5)),
            full(g2), full(rw_hi), full(rw_lo), full(rb_p), full(shg), full(shu), full(shd),
            pl.BlockSpec((None, MOE_EB, d, D_EXPERT), lambda i, e: (layer, e, 0, 0)),
            pl.BlockSpec((None, MOE_EB, d, D_EXPERT), lambda i, e: (layer, e, 0, 0)),
            pl.BlockSpec((None, n_hid, d), lambda i, e: (layer, e, 0)),
        ],
        out_specs=pl.BlockSpec((tm, d), lambda i, e: (i, 0)),
        out_shape=jax.ShapeDtypeStruct((m, d), F32),
        scratch_shapes=[pltpu.VMEM((tm, d), BF16), pltpu.VMEM((N_EXPERTS // MOE_EB, tm, LANES), F32),
                        pltpu.VMEM((tm, d), F32)],
        compiler_params=_cparams(("arbitrary", "arbitrary")),
        name="moe",
    )(x2, mod_l, mod_l, mod_l, g2, rw_hi, rw_lo, rb_p, shg, shu, shd, wg, wu, wd)


def _layer(x, mod_l, w, row0, rstride, ctx, rope_tabs):
    b, t, d = x.shape
    x2 = x.reshape(b * t, d)
    q, kv, su, sv, rkv, lora, gates = _inproj(x2, mod_l, w["norm1_g"], w["w_in"], row0, rstride, t)
    if ctx is None:
        o_attn, k_new = _attention(q.reshape(b, t, -1), kv.reshape(b, t, -1), w["q_norm"], w["k_norm"], None, None)
        st0 = None
    else:
        o_attn, k_new = _attention(q.reshape(b, t, -1), kv.reshape(b, t, -1), w["q_norm"], w["k_norm"],
                                   ctx[:2], rope_tabs)
        st0 = ctx[2]
    o_wkv, s_fin = _wkv(rkv.reshape(b, t, -1), lora.reshape(b, t, -1), w["wkv"], st0)
    x1 = _merge(x2, o_attn.reshape(b * t, -1), su, sv, o_wkv.reshape(b * t, -1), gates, mod_l,
                w["sgu_norm_g"], w["sgu_ws"], w["sgu_bs"],
                w["proj_attn"], w["proj_sgu"], w["proj_rwkv"], w["w_out"], row0, rstride, t)
    x_out = _moe(x1, mod_l, w["norm2_g"], w["router_w_hi"], w["router_w_lo"], w["router_bias"],
                 w["sh_gate"], w["sh_up"], w["sh_down"],
                 w["exp_gate"], w["exp_up"], w["exp_down"], row0, rstride, t)
    return x_out.reshape(b, t, d), (k_new, kv, s_fin)


def _rope_tables(n_tok):
    t = jnp.arange(n_tok)
    row = (t // GRID_W).astype(F32)
    col = (t % GRID_W).astype(F32)
    inv = ROPE_THETA ** (-jnp.arange(N_FREQ, dtype=F32) / N_FREQ)
    ang = jnp.stack([row[:, None] * inv[None, :], col[:, None] * inv[None, :]], axis=1)
    cos, sin = jnp.cos(ang), jnp.sin(ang)
    cos64 = jnp.stack([cos, cos], axis=2).reshape(n_tok, HEAD_DIM)
    sin64 = jnp.stack([-sin, sin], axis=2).reshape(n_tok, HEAD_DIM)
    return jnp.tile(cos64, (1, LANES // HEAD_DIM)), jnp.tile(sin64, (1, LANES // HEAD_DIM))


def _lora_pair_layout(w):
    npair = RW // LANES
    w4 = w.reshape(2, w.shape[1], npair, LANES)
    bd = jnp.einsum("dkpc,de->dkpec", w4, jnp.eye(2, dtype=w.dtype))
    return bd.reshape(2 * w.shape[1], npair * 2 * LANES).astype(BF16)


def _layer_weights(l, mod_w, mod_b, norm1_g, norm2_g, w_in, q_norm, k_norm, sgu_norm_g, sgu_ws, sgu_bs, rwkv_conv,
                   rwkv_w0, rwkv_w2, rwkv_a0, rwkv_a2, rwkv_g2, rwkv_k_k, rwkv_k_a, rwkv_r_k, rwkv_ln_g, rwkv_ln_b,
                   proj_attn, proj_sgu, proj_rwkv, w_out, router_w, router_bias, exp_gate, exp_up, exp_down,
                   sh_gate, sh_up, sh_down):
    d = D_MODEL
    row = lambda a: a.reshape(1, -1)
    pad_e = LANES - N_EXPERTS
    rw_pad = jnp.pad(router_w[l], ((0, 0), (0, pad_e)))
    return {
        "norm1_g": row(norm1_g[l]), "norm2_g": row(norm2_g[l]),
        "w_in": (w_in.astype(BF16), l),
        "q_norm": jnp.tile(row(q_norm[l]), (1, LANES // HEAD_DIM)),
        "k_norm": jnp.tile(row(k_norm[l]), (1, LANES // HEAD_DIM)),
        "sgu_norm_g": row(sgu_norm_g[l]),
        "sgu_ws": sgu_ws[l].astype(BF16),
        "sgu_bs": jnp.broadcast_to(sgu_bs[l][:, :, None], (SGU_GROUPS, CHUNK, LANES)),
        "wkv": {
            "conv": rwkv_conv[l],
            "w0": rwkv_w0[l], "a0": rwkv_a0[l],
            "w2": _lora_pair_layout(rwkv_w2[l]), "a2": _lora_pair_layout(rwkv_a2[l]),
            "g2": rwkv_g2[l].astype(BF16),
            "k_k": row(rwkv_k_k[l]), "k_a": row(rwkv_k_a[l]), "r_k": row(rwkv_r_k[l]),
            "ln_g": row(rwkv_ln_g[l]), "ln_b": row(rwkv_ln_b[l]),
        },
        "proj_attn": proj_attn[l].astype(BF16), "proj_sgu": proj_sgu[l].astype(BF16),
        "proj_rwkv": proj_rwkv[l].astype(BF16), "w_out": w_out[l].astype(BF16),
        "router_w_hi": rw_pad.astype(BF16),
        "router_w_lo": (rw_pad - rw_pad.astype(BF16).astype(F32)).astype(BF16),
        "router_bias": jnp.pad(row(router_bias[l]), ((0, 0), (0, pad_e))),
        "sh_gate": sh_gate[l].astype(BF16), "sh_up": sh_up[l].astype(BF16), "sh_down": sh_down[l].astype(BF16),
        "exp_gate": (exp_gate.astype(BF16), l), "exp_up": (exp_up.astype(BF16), l),
        "exp_down": (exp_down.astype(BF16).reshape(-1, N_EXPERTS * D_EXPERT, d), l),
    }


def _state_to_blockdiag_t(s):
    b = s.shape[0]
    st = jnp.swapaxes(s, -1, -2).reshape(b, 2, R_HEADS // 2, 2, R_HEAD, R_HEAD)
    bd = jnp.einsum("bdpjkv,ji->bdpjkiv", st, jnp.eye(2, dtype=s.dtype))
    return bd.reshape(b, 2, R_HEADS // 2, LANES, LANES)


def kernel(x_prompt, x_sample, cache_k, cache_v, state_wkv, c, c_ctx, mod_w, mod_b, norm1_g, norm2_g, w_in, q_norm, k_norm, sgu_norm_g, sgu_ws, sgu_bs, rwkv_conv, rwkv_w0, rwkv_w2, rwkv_a0, rwkv_a2, rwkv_g2, rwkv_k_k, rwkv_k_a, rwkv_r_k, rwkv_ln_g, rwkv_ln_b, proj_attn, proj_sgu, proj_rwkv, w_out, router_w, router_bias, exp_gate, exp_up, exp_down, sh_gate, sh_up, sh_down):
    params = (mod_w, mod_b, norm1_g, norm2_g, w_in, q_norm, k_norm, sgu_norm_g, sgu_ws, sgu_bs, rwkv_conv,
              rwkv_w0, rwkv_w2, rwkv_a0, rwkv_a2, rwkv_g2, rwkv_k_k, rwkv_k_a, rwkv_r_k, rwkv_ln_g, rwkv_ln_b,
              proj_attn, proj_sgu, proj_rwkv, w_out, router_w, router_bias, exp_gate, exp_up, exp_down,
              sh_gate, sh_up, sh_down)
    n_b, n_t, d = x_prompt.shape
    n_db, n_dt, _ = x_sample.shape
    assert n_db + 1 <= 8
    cmat = jnp.zeros((8, d), F32).at[0].set(c_ctx).at[1:1 + n_db].set(c)
    mod = _modulation(cmat, mod_w, mod_b)
    rope_tabs = _rope_tables(n_dt)
    xp, xs = x_prompt, x_sample
    ks, vs, ss = [], [], []
    for l in range(DEPTH):
        w = _layer_weights(l, *params)
        xp, (k_new, kv, s_fin) = _layer(xp, mod[l], w, 0, 0, None, None)
        ks.append(k_new.reshape(n_b, n_t, N_KV_HEADS, HEAD_DIM))
        vs.append(kv.reshape(n_b, n_t, 2 * KV_W)[:, :, KV_W:].reshape(n_b, n_t, N_KV_HEADS, HEAD_DIM))
        ss.append(s_fin)
        ctx = (cache_k[:, l].reshape(n_db, -1, KV_W), cache_v[:, l].reshape(n_db, -1, KV_W),
               _state_to_blockdiag_t(state_wkv[:, l]))
        xs, _ = _layer(xs, mod[l], w, 1, 1, ctx, rope_tabs)
    return (xp, xs, jnp.stack(ks, axis=1), jnp.stack(vs, axis=1), jnp.stack(ss, axis=1))
```

```python
import functools

import jax
import jax.numpy as jnp
from jax import lax
from jax.experimental import pallas as pl
from jax.experimental.pallas import tpu as pltpu

F32 = jnp.float32
BF16 = jnp.bfloat16
HI = lax.Precision.HIGHEST

D_MODEL = 1024
DEPTH = 2
GRID_W = 64
N_HEADS = 8
N_KV_HEADS = 2
HEAD_DIM = 64
ATTN_W = N_HEADS * HEAD_DIM
KV_W = N_KV_HEADS * HEAD_DIM
N_FREQ = HEAD_DIM // 4
ROPE_THETA = 10000.0
CHUNK = 128
SGU_GROUPS = 4
SGU_W = 512
R_HEADS = 8
R_HEAD = 64
RW = R_HEADS * R_HEAD
LORA_W = 64
LORA_A = 64
LORA_G = 128
DECAY_SCALE = 0.606531
GN_EPS = 64e-5
N_EXPERTS = 64
TOP_K = 6
D_EXPERT = 128
D_SHARED = 256
ROUTED_SCALE = 2.5
EPS = 1e-6
LORA_ALL = 2 * LORA_W + 2 * LORA_A + LORA_G
N_IN = ATTN_W + 2 * KV_W + 2 * SGU_W + 3 * RW + LORA_ALL + 3 * D_MODEL

LOG2_E = 1.4426950408889634
TOKEN_TILE = 512
LANES = 128
WKV_C = 64
MOE_EB = 8
VMEM_LIMIT = 56 * 1024 * 1024


def _cparams(sem, vmem=VMEM_LIMIT):
    return pltpu.CompilerParams(dimension_semantics=sem, vmem_limit_bytes=vmem)


def _dot(a, b, prec=None):
    return jnp.dot(a, b, preferred_element_type=F32, precision=prec)


def _dot_nt(a, b, prec=None):
    return lax.dot_general(a, b, (((1,), (1,)), ((), ())), preferred_element_type=F32, precision=prec)


def _sigmoid(x):
    return 1.0 / (1.0 + jnp.exp(-x))


def _silu(x):
    return x / (1.0 + jnp.exp(-x))


def _head_block_diag(value):
    r = lax.broadcasted_iota(jnp.int32, (LANES, LANES), 0) // HEAD_DIM
    c = lax.broadcasted_iota(jnp.int32, (LANES, LANES), 1) // HEAD_DIM
    return jnp.where(r == c, value, 0.0).astype(F32)


def _mod_row(ref, row):
    return ref[pl.ds(row, 1), :]


def _stacked_spec(w_layers, block=None, tail=None):
    arr, layer = w_layers
    block = tuple(arr.shape[1:] if block is None else block)
    tail = tail or (lambda *g: (0,) * len(block))
    return pl.BlockSpec((None,) + block, lambda *g: (layer,) + tuple(tail(*g)))


def _mod_kernel(c_ref, w_ref, b_ref, o_ref):
    o_ref[0] = _dot(_silu(c_ref[...]), w_ref[0], HI) + b_ref[0]


def _modulation(cmat, mod_w, mod_b):
    n_l, d, n6 = mod_w.shape
    tn = n6 // 4
    return pl.pallas_call(
        _mod_kernel,
        grid=(n_l, n6 // tn),
        in_specs=[
            pl.BlockSpec((8, d), lambda l, j: (0, 0)),
            pl.BlockSpec((1, d, tn), lambda l, j: (l, 0, j)),
            pl.BlockSpec((1, 1, tn), lambda l, j: (l, 0, j)),
        ],
        out_specs=pl.BlockSpec((1, 8, tn), lambda l, j: (l, 0, j)),
        out_shape=jax.ShapeDtypeStruct((n_l, 8, n6), F32),
        compiler_params=_cparams(("arbitrary", "arbitrary")),
        name="modulation",
    )(cmat, mod_w, mod_b.reshape(n_l, 1, n6))


_IN_SPLIT = (ATTN_W, 2 * KV_W, SGU_W, SGU_W, 3 * RW, LORA_ALL, 3 * D_MODEL)


def _inproj_kernel(x_ref, sh_ref, sc_ref, g_ref, w_ref, *out_refs, row0, rstride, bps):
    row = row0 + (pl.program_id(0) // bps) * rstride if rstride else row0
    x = x_ref[...]
    h = x * lax.rsqrt(jnp.mean(x * x, axis=-1, keepdims=True) + EPS) * g_ref[...]
    h = (h * (1.0 + _mod_row(sc_ref, row)) + _mod_row(sh_ref, row)).astype(BF16)
    off = 0
    for ref, size in zip(out_refs, _IN_SPLIT):
        ref[...] = _dot(h, w_ref[:, off:off + size]).astype(ref.dtype)
        off += size


def _inproj(x2, mod_l, g1, w_in_layers, row0, rstride, t_seq):
    m, d = x2.shape
    w_in_b, layer = w_in_layers
    tm = min(TOKEN_TILE, t_seq if rstride else m)
    kern = functools.partial(_inproj_kernel, row0=row0, rstride=rstride, bps=max(t_seq // tm, 1))
    return pl.pallas_call(
        kern,
        grid=(m // tm,),
        in_specs=[
            pl.BlockSpec((tm, d), lambda i: (i, 0)),
            _stacked_spec(mod_l, (8, d), lambda i: (0, 0)),
            _stacked_spec(mod_l, (8, d), lambda i: (0, 1)),
            _stacked_spec(g1),
            pl.BlockSpec((None, d, N_IN), lambda i: (layer, 0, 0), pipeline_mode=pl.Buffered(1)),
        ],
        out_specs=[pl.BlockSpec((tm, s), lambda i: (i, 0)) for s in _IN_SPLIT],
        out_shape=[jax.ShapeDtypeStruct((m, s), BF16 if s == 3 * D_MODEL else F32) for s in _IN_SPLIT],
        compiler_params=_cparams(("arbitrary",)),
        name="inproj",
    )(x2, mod_l[0], mod_l[0], g1[0], w_in_b)


def _attn_kernel(*refs, t_seq, past, tq, latent):
    if latent:
        (q_ref, kv_ref, ck_ref, cv_ref, qg_ref, kg_ref, cq_ref, sq_ref, ckk_ref, skk_ref,
         o_ref, k_s, vt_s) = refs
    else:
        q_ref, kv_ref, qg_ref, kg_ref, o_ref, kn_ref, k_s, vt_s = refs
    lane = lax.broadcasted_iota(jnp.int32, (1, LANES), 1)
    first_half = ((lane // N_FREQ) % 2) == 0
    mean_bd = _head_block_diag(1.0 / HEAD_DIM).astype(BF16)
    q_per_kv = N_HEADS // N_KV_HEADS

    def head_norm(x, g):
        sq = x * x
        hi = sq.astype(BF16)
        lo = (sq - hi.astype(F32)).astype(BF16)
        return x * lax.rsqrt(_dot(hi, mean_bd) + _dot(lo, mean_bd) + EPS) * g

    def rope(x, cos, sin_signed):
        rot = jnp.where(first_half, pltpu.roll(x, LANES - N_FREQ, 1), pltpu.roll(x, N_FREQ, 1))
        return x * cos + rot * sin_signed

    @pl.when(pl.program_id(1) == 0)
    def _():
        rb = 256
        for i in range(past // rb):
            r0 = i * rb
            k_s[r0:r0 + rb, :] = ck_ref[0, r0:r0 + rb, :].astype(BF16)
            vt_s[:, r0:r0 + rb] = cv_ref[0, r0:r0 + rb, :].T.astype(BF16)
        for i in range(t_seq // rb):
            r0 = i * rb
            kn = head_norm(kv_ref[0, r0:r0 + rb, 0:KV_W], kg_ref[...])
            if latent:
                kn = rope(kn, ckk_ref[r0:r0 + rb, :], skk_ref[r0:r0 + rb, :])
            else:
                kn_ref[0, r0:r0 + rb, :] = kn
            k_s[past + r0:past + r0 + rb, :] = kn.astype(BF16)
            vt_s[:, past + r0:past + r0 + rb] = kv_ref[0, r0:r0 + rb, KV_W:2 * KV_W].T.astype(BF16)

    slabs = []
    for j in range(ATTN_W // LANES):
        qn = head_norm(q_ref[0, :, LANES * j:LANES * (j + 1)], qg_ref[...])
        if latent:
            qn = rope(qn, cq_ref[...], sq_ref[...])
        slabs.append(qn * (HEAD_DIM ** -0.5 * LOG2_E))
    q_t = jnp.concatenate(slabs, axis=1).T.astype(BF16)
    zero_h = jnp.zeros((HEAD_DIM, tq), BF16)
    qz = []
    for g in range(N_KV_HEADS):
        cols = []
        for h in range(q_per_kv * g, q_per_kv * (g + 1)):
            q_h = q_t[HEAD_DIM * h:HEAD_DIM * (h + 1), :]
            cols.append(jnp.concatenate([q_h, zero_h] if g == 0 else [zero_h, q_h], axis=0))
        qz.append(jnp.concatenate(cols, axis=1))
    s_t = [_dot(k_s[...], x) for x in qz]
    p_t = [jnp.exp2(x - jnp.max(x, axis=0, keepdims=True)) for x in s_t]
    den = [jnp.sum(x, axis=0, keepdims=True) for x in p_t]
    o_t = [_dot(vt_s[HEAD_DIM * g:HEAD_DIM * (g + 1), :], p_t[g].astype(BF16)) / den[g]
           for g in range(N_KV_HEADS)]
    outs =[o_t[g][:, tq * h:tq * (h + 1)] for g in range(N_KV_HEADS) for h in range(q_per_kv)]
    o_ref[0] = jnp.concatenate(outs, axis=0).T.astype(o_ref.dtype)


def _attention(q, kv, qg, kg, ctx_kv, rope_tabs):
    b, t, _ = q.shape
    latent = ctx_kv is not None
    past = ctx_kv[0].shape[1] if latent else 0
    s_all = past + t
    tq = 128 if latent else min(256, t)
    kern = functools.partial(_attn_kernel, t_seq=t, past=past, tq=tq, latent=latent)
    in_specs = [
        pl.BlockSpec((1, tq, ATTN_W), lambda bi, qi: (bi, qi, 0)),
        pl.BlockSpec((1, t, 2 * KV_W), lambda bi, qi: (bi, 0, 0)),
    ]
    args = [q, kv]
    if latent:
        in_specs += [pl.BlockSpec((1, past, KV_W), lambda bi, qi: (bi, 0, 0))] * 2
        args += list(ctx_kv)
    in_specs += [_stacked_spec(qg), _stacked_spec(kg)]
    args += [qg[0], kg[0]]
    out_specs = [pl.BlockSpec((1, tq, ATTN_W), lambda bi, qi: (bi, qi, 0))]
    out_shape = [jax.ShapeDtypeStruct((b, t, ATTN_W), BF16)]
    if latent:
        cos_t, sin_t = rope_tabs
        in_specs += [pl.BlockSpec((tq, LANES), lambda bi, qi: (qi, 0))] * 2
        in_specs += [pl.BlockSpec((t, LANES), lambda bi, qi: (0, 0))] * 2
        args += [cos_t, sin_t, cos_t, sin_t]
    else:
        out_specs.append(pl.BlockSpec((1, t, KV_W), lambda bi, qi: (bi, 0, 0)))
        out_shape.append(jax.ShapeDtypeStruct((b, t, KV_W), F32))
    res = pl.pallas_call(
        kern,
        grid=(b, t // tq),
        in_specs=in_specs,
        out_specs=out_specs,
        out_shape=out_shape,
        scratch_shapes=[pltpu.VMEM((s_all, LANES), BF16), pltpu.VMEM((LANES, s_all), BF16)],
        compiler_params=_cparams(("arbitrary", "arbitrary")),
        name="attention_lat" if latent else "attention_ctx",
    )(*args)
    return res if not latent else (res[0], None)


def _wkv_kernel(*refs, t_seq, latent, nb):
    (r_ref, k_ref, v_ref, lora_ref, cwr_ref, cwk_ref, cwv_ref, w0_ref, a0_ref, w2_ref, a2_ref, g2_ref,
     kk_ref, ka_ref, rk_ref, lng_ref, lnb_ref) = refs[:17]
    rest = refs[17:]
    if latent:
        st0_ref, o_ref = rest[:2]
        scr = rest[2:]
        sf_ref = None
    else:
        o_ref, sf_ref = rest[:2]
        scr = rest[2:]
        st0_ref = None
    tar_s, tv_s, ov_s, arb_s, w_s, n_s, gc_s, st_s, bon_s = scr
    c_len = WKV_C
    c2 = 2 * c_len
    nc = t_seq // c_len

    lane = lax.broadcasted_iota(jnp.int32, (1, LANES), 1)
    lo = lane < R_HEAD
    sum_bd = _head_block_diag(1.0)
    mean_bd = _head_block_diag(1.0 / R_HEAD)
    row_c = lax.broadcasted_iota(jnp.int32, (c_len, LANES), 0)
    ri = lax.broadcasted_iota(jnp.int32, (c2, c2), 0)
    ci = lax.broadcasted_iota(jnp.int32, (c2, c2), 1)
    same_head = (ri // c_len) == (ci // c_len)
    tt, ss = ri % c_len, ci % c_len
    eye2 = jnp.where(ri == ci, 1.0, 0.0).astype(F32)
    strict = (same_head & (ss < tt), same_head & (ss > tt))
    incl = (same_head & (ss <= tt), same_head & (ss >= tt))
    tj = lax.broadcasted_iota(jnp.int32, (c_len, c2), 0)
    sj = lax.broadcasted_iota(jnp.int32, (c_len, c2), 1)
    cum_both = jnp.where(((sj < c_len) & (sj <= tj)) | ((sj >= c_len) & (sj - c_len >= tj)), 1.0, 0.0).astype(BF16)
    sum_bd_b = sum_bd.astype(BF16)
    mean_bd_b = mean_bd.astype(BF16)

    def bdot(a, b):
        return _dot(a.astype(BF16), b.astype(BF16))

    def split2(x):
        hi = x.astype(BF16)
        return hi, (x - hi.astype(F32)).astype(BF16)

    def head_reduce(x, bd_b):
        hi, lo = split2(x)
        return _dot(hi, bd_b) + _dot(lo, bd_b)

    def stack_heads(x):
        return jnp.concatenate([jnp.where(lo, x, 0.0), jnp.where(lo, 0.0, x)], axis=0)

    def fold_heads(x):
        return x[0:c_len] + x[c_len:c2]

    def slot(bi, c):
        return bi * nc + c

    def conv_tokens(ref, w_ref, bi, c, t0):
        x = ref[bi, pl.ds(t0, c_len), :]
        if isinstance(c, int):
            zero_row = jnp.zeros((1, LANES), F32)
            prev = ref[bi, pl.ds(t0 - 1, 1), :] if c > 0 else zero_row
            nxt = ref[bi, pl.ds(t0 + c_len, 1), :] if c < nc - 1 else zero_row
        else:
            prev = ref[bi, pl.ds(jnp.maximum(t0 - 1, 0), 1), :]
            nxt = ref[bi, pl.ds(jnp.minimum(t0 + c_len, t_seq - 1), 1), :]
            prev = jnp.where(c > 0, prev, 0.0)
            nxt = jnp.where(c < nc - 1, nxt, 0.0)
        xm = jnp.where(row_c == 0, prev, pltpu.roll(x, 1, 0))
        xp = jnp.where(row_c == c_len - 1, nxt, pltpu.roll(x, c_len - 1, 0))
        return xm * w_ref[0:1, :] + x * w_ref[1:2, :] + xp * w_ref[2:3, :]

    def chunk_group(chunks):
        n = len(chunks)
        t0s = [c * c_len if isinstance(c, int) else pl.multiple_of(c * c_len, c_len) for _, c in chunks]
        rc = [conv_tokens(r_ref, cwr_ref, bi, c, t0) for (bi, c), t0 in zip(chunks, t0s)]
        kc = [conv_tokens(k_ref, cwk_ref, bi, c, t0) for (bi, c), t0 in zip(chunks, t0s)]
        vc = [conv_tokens(v_ref, cwv_ref, bi, c, t0) for (bi, c), t0 in zip(chunks, t0s)]
        kkr = [k * kk_ref[...] for k in kc]
        kk = [x * lax.rsqrt(head_reduce(x * x, sum_bd_b) + 1e-12) for x in kkr]
        tw = [jnp.tanh(lora_ref[bi, pl.ds(t0, c_len), 0:2 * LORA_W]) for (bi, _), t0 in zip(chunks, t0s)]
        xa = [lora_ref[bi, pl.ds(t0, c_len), 2 * LORA_W:2 * LORA_W + 2 * LORA_A] for (bi, _), t0 in zip(chunks, t0s)]
        v_hs = [stack_heads(v).astype(BF16) for v in vc]
        prob =[(i, d) for i in range(n) for d in range(2)]
        pre_w = [_dot(x.astype(BF16), w2_ref[...]) for x in tw]
        pre_a = [_dot(x.astype(BF16), a2_ref[...]) for x in xa]
        lw = [-DECAY_SCALE * _sigmoid(w0_ref[d:d + 1, :] + pre_w[i][:, LANES * d:LANES * (d + 1)]) for i, d in prob]
        ag = [_sigmoid(a0_ref[d:d + 1, :] + pre_a[i][:, LANES * d:LANES * (d + 1)]) for i, d in prob]
        kd = [kc[i] * (1.0 + (ag[p] - 1.0) * ka_ref[...]) for p, (i, d) in enumerate(prob)]
        kb = [kk[i] * ag[p] for p, (i, d) in enumerate(prob)]
        for i in range(n):
            bon = head_reduce(rc[i] * (kd[2 * i] + kd[2 * i + 1]) * rk_ref[...], sum_bd_b) * vc[i]
            bon_s[chunks[i][0], pl.ds(t0s[i], c_len), :] = bon
        lw_sp = [split2(x) for x in lw]
        zc = jnp.zeros((c_len, LANES), BF16)
        cum_res = [_dot(cum_both, jnp.concatenate(
            [jnp.concatenate([lw_sp[2 * i][0], lw_sp[2 * i][1], zc, zc], axis=1),
             jnp.concatenate([zc, zc, lw_sp[2 * i + 1][0], lw_sp[2 * i + 1][1]], axis=1)], axis=0)) for i in range(n)]
        cs_in = [cum_res[i][:, 2 * LANES * d:2 * LANES * d + LANES] + cum_res[i][:, 2 * LANES * d + LANES:2 * LANES * (d + 1)]
                 for i, d in prob]
        tot = [cs_in[p][c_len - 1:c_len, :] if d == 0 else cs_in[p][0:1, :] for p, (i, d) in enumerate(prob)]
        e_neg = [jnp.exp(-x) for x in cs_in]
        e_rem = [jnp.exp(tot[p] - cs_in[p]) for p in range(len(prob))]
        a_t = [-kk[i] * jnp.exp(cs_in[p] - (lw_sp[p][0].astype(F32) + lw_sp[p][1].astype(F32)))
               for p, (i, d) in enumerate(prob)]
        r_t = [rc[i] * jnp.exp(cs_in[p]) for p, (i, d) in enumerate(prob)]
        a_hs = [stack_heads(x).astype(BF16) for x in a_t]
        lhs = [jnp.concatenate([a_hs[p], stack_heads(r_t[p]).astype(BF16)], axis=0) for p in range(len(prob))]
        rhs = [jnp.concatenate([stack_heads(kd[p] * e_neg[p]), stack_heads(kb[p] * e_neg[p])], axis=0).astype(BF16)
               for p in range(len(prob))]
        m4 = [_dot_nt(a, b) for a, b in zip(lhs, rhs)]
        a_ak = [jnp.where(strict[d], m4[p][0:c2, 0:c2], 0.0).astype(BF16) for p, (i, d) in enumerate(prob)]
        l_ab = [jnp.where(strict[d], m4[p][0:c2, c2:2 * c2], 0.0) for p, (i, d) in enumerate(prob)]
        a_rk = [jnp.where(incl[d], m4[p][c2:2 * c2, 0:c2], 0.0).astype(BF16) for p, (i, d) in enumerate(prob)]
        for p, (i, d) in enumerate(prob):
            arb_s[d, slot(*chunks[i])] = jnp.where(incl[d], m4[p][c2:2 * c2, c2:2 * c2], 0.0).astype(BF16)
        t_inv = [eye2 + x for x in l_ab]
        pw_b = [x.astype(BF16) for x in l_ab]
        pw_b = [_dot(x, x).astype(BF16) for x in pw_b]
        for _ in range(4):
            res = [_dot(x, jnp.concatenate([x, t.astype(BF16)], axis=1)) for x, t in zip(pw_b, t_inv)]
            pw_b = [r[:, 0:c2].astype(BF16) for r in res]
            t_inv = [t + r[:, c2:2 * c2] for t, r in zip(t_inv, res)]
        t_inv = [t + _dot(x, t.astype(BF16)) for x, t in zip(pw_b, t_inv)]
        t_b = [t.astype(BF16) for t in t_inv]
        av = [_dot(a_ak[p], v_hs[i]).astype(BF16) for p, (i, d) in enumerate(prob)]
        ov = [fold_heads(_dot(a_rk[p], v_hs[i])) for p, (i, d) in enumerate(prob)]
        tt = [_dot(t, jnp.concatenate([a, x], axis=1)) for t, a, x in zip(t_b, a_hs, av)]
        ta = [fold_heads(r[:, 0:LANES]) for r in tt]
        tv = [fold_heads(r[:, LANES:2 * LANES]) for r in tt]
        kbt = [jnp.concatenate([kd[p] * e_rem[p], kb[p] * e_rem[p]], axis=0).T.astype(BF16) for p in range(len(prob))]
        zero_c = jnp.zeros((c_len, LANES), BF16)
        upd_rhs = [jnp.concatenate([jnp.concatenate([vc[i].astype(BF16), zero_c], axis=1),
                                    jnp.concatenate([tv[p].astype(BF16), ta[p].astype(BF16)], axis=1)], axis=0)
                   for p, (i, d) in enumerate(prob)]
        nw = [_dot(a, b) for a, b in zip(kbt, upd_rhs)]
        for p, (i, d) in enumerate(prob):
            c = slot(*chunks[i])
            tar_s[d, c] = jnp.concatenate([ta[p], r_t[p]], axis=0).astype(BF16)
            tv_s[d, c] = tv[p]
            ov_s[d, c] = ov[p]
            n_s[d, c] = jnp.where(bd_mask, nw[p][:, 0:LANES], 0.0)
            w_s[d, c] = jnp.where(bd_mask, nw[p][:, LANES:2 * LANES], 0.0).astype(BF16)
            gc_s[d, c] = jnp.broadcast_to(jnp.exp(tot[p]), (LANES, LANES)).T

    bd_mask = sum_bd > 0.5
    group1 = 8
    if nb > 1 or nc <= group1:
        every = [(bi, c) for bi in range(nb) for c in range(nc)]
        for g0 in range(0, len(every), group1):
            chunk_group(every[g0:g0 + group1])
    else:
        assert nb == 1 and nc % group1 == 0

        def phase1(i, carry):
            chunk_group([(0, i * group1 + j) for j in range(group1)])
            return carry

        lax.fori_loop(0, nc // group1, phase1, 0)

    chains = [(bi, d) for bi in range(nb) for d in range(2)]

    def phase2(i, carry):
        cs = [slot(bi, i if d == 0 else nc - 1 - i) for bi, d in chains]
        st_b = [x.astype(BF16) for x in carry]
        for (bi, d), c, x in zip(chains, cs, st_b):
            st_s[d, c] = x
        prod = [_dot(w_s[d, c], x) for (bi, d), c, x in zip(chains, cs, st_b)]
        return tuple(gc_s[d, c] * s + pr + n_s[d, c] for (bi, d), c, s, pr in zip(chains, cs, carry, prod))

    if latent:
        init = tuple(st0_ref[bi, d, 0] for bi, d in chains)
    else:
        init = (jnp.zeros((LANES, LANES), F32),) * len(chains)
    fin = lax.fori_loop(0, nc, phase2, init)
    if sf_ref is not None:
        for (bi, d), st in zip(chains, fin):
            s_vk = st.T
            sf_ref[bi, d, 0] = s_vk[0:R_HEAD, 0:R_HEAD]
            sf_ref[bi, d, 1] = pltpu.roll(s_vk[R_HEAD:2 * R_HEAD, :], R_HEAD, 1)[:, 0:R_HEAD]

    def out_group(chunks):
        prob = [(bi, c, slot(bi, c), d) for bi, c in chunks for d in range(2)]
        both = [_dot(tar_s[d, s], st_s[d, s]) for _, _, s, d in prob]
        u = [both[p][0:c_len] + tv_s[d, s] for p, (_, _, s, d) in enumerate(prob)]
        ou = [_dot(arb_s[d, s], stack_heads(u[p]).astype(BF16)) for p, (_, _, s, d) in enumerate(prob)]
        o = [ov_s[d, s] + both[p][c_len:c2] + fold_heads(ou[p]) for p, (_, _, s, d) in enumerate(prob)]
        of = [o[2 * j] + o[2 * j + 1] for j in range(len(chunks))]
        xc = [x - head_reduce(x, mean_bd_b) for x in of]
        var = [head_reduce(x * x, mean_bd_b) for x in xc]
        for j, (bi, c) in enumerate(chunks):
            t0 = c * c_len if isinstance(c, int) else pl.multiple_of(c * c_len, c_len)
            on = xc[j] * lax.rsqrt(var[j] + GN_EPS) * lng_ref[...] + lnb_ref[...]
            xg = lora_ref[bi, pl.ds(t0, c_len), 2 * LORA_W + 2 * LORA_A:LORA_ALL]
            g_out = _dot(_sigmoid(xg).astype(BF16), g2_ref[...])
            o_ref[bi, pl.ds(t0, c_len), :] = ((on + bon_s[bi, pl.ds(t0, c_len), :]) * g_out).astype(o_ref.dtype)

    group3 = 8
    if nb > 1 or nc <= group3:
        every = [(bi, c) for bi in range(nb) for c in range(nc)]
        for g0 in range(0, len(every), group3):
            out_group(every[g0:g0 + group3])
    else:
        assert nb == 1 and nc % group3 == 0

        def phase3(i, carry):
            out_group([(0, i * group3 + j) for j in range(group3)])
            return carry

        lax.fori_loop(0, nc // group3, phase3, 0)


def _wkv(rkv, lora, wts, st0):
    b, t, _ = rkv.shape
    latent = st0 is not None
    nc = t // WKV_C
    npair = RW // LANES
    nb = 4 if (4 * nc <= 16 and b % 4 == 0) else 1
    kern = functools.partial(_wkv_kernel, t_seq=t, latent=latent, nb=nb)

    def seq_spec(col0):
        return pl.BlockSpec((nb, t, LANES), lambda bi, p: (bi, 0, col0 + p))

    def row_spec(name, rows, col0=0, width=LANES):
        return _stacked_spec(wts[name], (rows, width), lambda bi, p: (0, col0 + p))

    in_specs = [
        seq_spec(0), seq_spec(npair), seq_spec(2 * npair),
        pl.BlockSpec((nb, t, LORA_ALL), lambda bi, p: (bi, 0, 0)),
        row_spec("conv", 3, 0), row_spec("conv", 3, npair), row_spec("conv", 3, 2 * npair),
        row_spec("w0", 2), row_spec("a0", 2),
        row_spec("w2", LANES, width=2 * LANES), row_spec("a2", LANES, width=2 * LANES),
        row_spec("g2", LANES),
        row_spec("k_k", 1), row_spec("k_a", 1), row_spec("r_k", 1), row_spec("ln_g", 1), row_spec("ln_b", 1),
    ]
    args = [rkv, rkv, rkv, lora] + [wts[n][0] for n in ("conv", "conv", "conv", "w0", "a0", "w2", "a2", "g2",
                                                        "k_k", "k_a", "r_k", "ln_g", "ln_b")]
    out_specs = [seq_spec(0)]
    out_shape = [jax.ShapeDtypeStruct((b, t, RW), BF16)]
    if latent:
        in_specs.append(pl.BlockSpec((nb, 2, 1, LANES, LANES), lambda bi, p: (bi, 0, p, 0, 0)))
        args.append(st0)
    else:
        out_specs.append(pl.BlockSpec((nb, 2, 2, R_HEAD, R_HEAD), lambda bi, p: (bi, 0, p, 0, 0)))
        out_shape.append(jax.ShapeDtypeStruct((b, 2, R_HEADS, R_HEAD, R_HEAD), F32))
    chunk_rows = pltpu.VMEM((2, nb * nc, WKV_C, LANES), F32)
    chunk_sq = pltpu.VMEM((2, nb * nc, LANES, LANES), F32)
    chunk_sq_b = pltpu.VMEM((2, nb * nc, LANES, LANES), BF16)
    res = pl.pallas_call(
        kern,
        grid=(b // nb, npair),
        in_specs=in_specs,
        out_specs=out_specs,
        out_shape=out_shape,
        scratch_shapes=[chunk_sq_b, chunk_rows, chunk_rows, chunk_sq_b, chunk_sq_b, chunk_sq, chunk_sq, chunk_sq_b,
                        pltpu.VMEM((nb, t, LANES), F32)],
        compiler_params=_cparams(("arbitrary", "arbitrary")),
        name="wkv_lat" if latent else "wkv_ctx",
    )(*args)
    return (res[0], None) if latent else (res[0], res[1])


def _merge_kernel(x_ref, oa_ref, su_ref, sv_ref, or_ref, gates_ref, gt_ref, sg_ref, ws_ref, bs_ref,
                  pa_ref, ps_ref, pr_ref, wo_ref, o_ref, *, row0, rstride, bps):
    row = row0 + (pl.program_id(0) // bps) * rstride if rstride else row0
    d = D_MODEL
    sgu_rows = []
    for c in range(x_ref.shape[0] // CHUNK):
        rows = slice(CHUNK * c, CHUNK * (c + 1))
        v = sv_ref[rows, :]
        vn = (v * lax.rsqrt(jnp.mean(v * v, axis=-1, keepdims=True) + EPS) * sg_ref[...]).astype(BF16)
        sgu_rows.append(jnp.concatenate(
            [(su_ref[rows, LANES * g:LANES * (g + 1)]
              * (_dot(ws_ref[g], vn[:, LANES * g:LANES * (g + 1)]) + bs_ref[g])).astype(BF16)
             for g in range(SGU_GROUPS)], axis=1))
    o_sgu = jnp.concatenate(sgu_rows, axis=0)
    gate = lambda j: _sigmoid(gates_ref[:, j * d:(j + 1) * d].astype(F32))
    merged = (gate(0) * _dot(oa_ref[...].astype(BF16), pa_ref[...])
              + gate(1) * _dot(o_sgu, ps_ref[...])
              + gate(2) * _dot(or_ref[...].astype(BF16), pr_ref[...]))
    o_ref[...] = x_ref[...] + _mod_row(gt_ref, row) * _dot(merged.astype(BF16), wo_ref[...])


def _merge(x2, oa, su, sv, orw, gates, mod_l, sgu_g, sgu_ws, sgu_bs, pa, ps, pr, wo, row0, rstride, t_seq):
    m, d = x2.shape
    tm = min(TOKEN_TILE, t_seq if rstride else m)
    assert tm % CHUNK == 0 and t_seq % CHUNK == 0
    kern = functools.partial(_merge_kernel, row0=row0, rstride=rstride, bps=max(t_seq // tm, 1))
    tok = lambda w: pl.BlockSpec((tm, w), lambda i: (i, 0))
    weights = (sgu_g, sgu_ws, sgu_bs, pa, ps, pr, wo)
    return pl.pallas_call(
        kern,
        grid=(m // tm,),
        in_specs=[tok(d), tok(ATTN_W), tok(SGU_W), tok(SGU_W), tok(RW), tok(3 * d),
                  _stacked_spec(mod_l, (8, d), lambda i: (0, 2))]
                 + [_stacked_spec(a) for a in weights],
        out_specs=tok(d),
        out_shape=jax.ShapeDtypeStruct((m, d), F32),
        compiler_params=_cparams(("arbitrary",)),
        name="merge",
    )(x2, oa, su, sv, orw, gates, mod_l[0], *[a[0] for a in weights])


def _moe_kernel(x_ref, sh_ref, sc_ref, gt_ref, g2_ref, rwh_ref, rwl_ref, rb_ref, shg_ref, shu_ref, shd_ref,
                wg_ref, wu_ref, wd_ref, o_ref, h_s, comb_s, acc_s, *, row0, rstride, bps):
    row = row0 + (pl.program_id(0) // bps) * rstride if rstride else row0
    e = pl.program_id(1)
    tm = x_ref.shape[0]

    @pl.when(e == 0)
    def _():
        x = x_ref[...]
        h = x * lax.rsqrt(jnp.mean(x * x, axis=-1, keepdims=True) + EPS) * g2_ref[...]
        h = h * (1.0 + _mod_row(sc_ref, row)) + _mod_row(sh_ref, row)
        hb = h.astype(BF16)
        h_lo = (h - hb.astype(F32)).astype(BF16)
        logits = _dot(hb, rwh_ref[...]) + _dot(hb, rwl_ref[...]) + _dot(h_lo, rwh_ref[...])
        scores = _sigmoid(logits)
        sc_t = scores.T[0:N_EXPERTS]
        sel = (scores + rb_ref[...]).T[0:N_EXPERTS]
        eid = lax.broadcasted_iota(jnp.int32, (N_EXPERTS, tm), 0).astype(F32)
        picked = jnp.zeros((N_EXPERTS, tm), jnp.bool_)
        for _ in range(TOP_K):
            best = jnp.max(sel, axis=0, keepdims=True)
            first = jnp.min(jnp.where(sel == best, eid, float(N_EXPERTS)), axis=0, keepdims=True)
            hit = eid == first
            picked = jnp.logical_or(picked, hit)
            sel = jnp.where(hit, -jnp.inf, sel)
        sw = jnp.where(picked, sc_t, 0.0)
        comb_t = sw / jnp.sum(sw, axis=0, keepdims=True) * ROUTED_SCALE
        comb = jnp.concatenate([comb_t, jnp.zeros((LANES - N_EXPERTS, tm), F32)], axis=0).T
        for b in range(N_EXPERTS // MOE_EB):
            comb_s[b] = comb if b == 0 else pltpu.roll(comb, LANES - MOE_EB * b, 1)
        h_s[...] = hb
        acc_s[...] = _dot((_silu(_dot(hb, shg_ref[...])) * _dot(hb, shu_ref[...])).astype(BF16), shd_ref[...])

    hb = h_s[...]
    comb = comb_s[e]
    hid = []
    for j in range(MOE_EB):
        gu = _dot(hb, jnp.concatenate([wg_ref[j], wu_ref[j]], axis=1))
        cw = jnp.broadcast_to(comb[:, j:j + 1], (tm, D_EXPERT))
        hid.append((_silu(gu[:, 0:D_EXPERT]) * gu[:, D_EXPERT:2 * D_EXPERT] * cw).astype(BF16))
    acc_s[...] += _dot(jnp.concatenate(hid, axis=1), wd_ref[...])

    @pl.when(e == pl.num_programs(1) - 1)
    def _():
        o_ref[...] = x_ref[...] + _mod_row(gt_ref, row) * acc_s[...]


def _moe(x2, mod_l, g2, rw_hi, rw_lo, rb_p, shg, shu, shd, wg_l, wu_l, wd_l, row0, rstride, t_seq):
    m, d = x2.shape
    (wg, layer), (wu, _), (wd, _) = wg_l, wu_l, wd_l
    tm = min(1024, t_seq if rstride else m)
    n_hid = MOE_EB * D_EXPERT
    kern = functools.partial(_moe_kernel, row0=row0, rstride=rstride, bps=max(t_seq // tm, 1))
    small = (g2, rw_hi, rw_lo, rb_p, shg, shu, shd)
    return pl.pallas_call(
        kern,
        grid=(m // tm, N_EXPERTS // MOE_EB),
        in_specs=[
            pl.BlockSpec((tm, d), lambda i, e: (i, 0)),
            _stacked_spec(mod_l, (8, d), lambda i, e: (0, 3)),
            _stacked_spec(mod_l, (8, d), lambda i, e: (0, 4)),
            _stacked_spec(mod_l, (8, d), lambda i, e: (0, 5)),
            *[_stacked_spec(a) for a in small],
            pl.BlockSpec((None, MOE_EB, d, D_EXPERT), lambda i, e: (layer, e, 0, 0)),
            pl.BlockSpec((None, MOE_EB, d, D_EXPERT), lambda i, e: (layer, e, 0, 0)),
            pl.BlockSpec((None, n_hid, d), lambda i, e: (layer, e, 0)),
        ],
        out_specs=pl.BlockSpec((tm, d), lambda i, e: (i, 0)),
        out_shape=jax.ShapeDtypeStruct((m, d), F32),
        scratch_shapes=[pltpu.VMEM((tm, d), BF16), pltpu.VMEM((N_EXPERTS // MOE_EB, tm, LANES), F32),
                        pltpu.VMEM((tm, d), F32)],
        compiler_params=_cparams(("arbitrary", "arbitrary")),
        name="moe",
    )(x2, mod_l[0], mod_l[0], mod_l[0], *[a[0] for a in small], wg, wu, wd)


def _layer(x, mod_l, w, row0, rstride, ctx, rope_tabs):
    b, t, d = x.shape
    x2 = x.reshape(b * t, d)
    q, kv, su, sv, rkv, lora, gates = _inproj(x2, mod_l, w["norm1_g"], w["w_in"], row0, rstride, t)
    if ctx is None:
        o_attn, k_new = _attention(q.reshape(b, t, -1), kv.reshape(b, t, -1), w["q_norm"], w["k_norm"], None, None)
        st0 = None
    else:
        o_attn, k_new = _attention(q.reshape(b, t, -1), kv.reshape(b, t, -1), w["q_norm"], w["k_norm"],
                                   ctx[:2], rope_tabs)
        st0 = ctx[2]
    o_wkv, s_fin = _wkv(rkv.reshape(b, t, -1), lora.reshape(b, t, -1), w["wkv"], st0)
    x1 = _merge(x2, o_attn.reshape(b * t, -1), su, sv, o_wkv.reshape(b * t, -1), gates, mod_l,
                w["sgu_norm_g"], w["sgu_ws"], w["sgu_bs"],
                w["proj_attn"], w["proj_sgu"], w["proj_rwkv"], w["w_out"], row0, rstride, t)
    x_out = _moe(x1, mod_l, w["norm2_g"], w["router_w_hi"], w["router_w_lo"], w["router_bias"],
                 w["sh_gate"], w["sh_up"], w["sh_down"],
                 w["exp_gate"], w["exp_up"], w["exp_down"], row0, rstride, t)
    return x_out.reshape(b, t, d), (k_new, kv, s_fin)


def _rope_tables(n_tok):
    t = jnp.arange(n_tok)
    row = (t // GRID_W).astype(F32)
    col = (t % GRID_W).astype(F32)
    inv = ROPE_THETA ** (-jnp.arange(N_FREQ, dtype=F32) / N_FREQ)
    ang = jnp.stack([row[:, None] * inv[None, :], col[:, None] * inv[None, :]], axis=1)
    cos, sin = jnp.cos(ang), jnp.sin(ang)
    cos64 = jnp.stack([cos, cos], axis=2).reshape(n_tok, HEAD_DIM)
    sin64 = jnp.stack([-sin, sin], axis=2).reshape(n_tok, HEAD_DIM)
    return jnp.tile(cos64, (1, LANES // HEAD_DIM)), jnp.tile(sin64, (1, LANES // HEAD_DIM))


def _lora_pair_layout(w):
    npair = RW // LANES
    n_l, _, k, _ = w.shape
    w4 = w.reshape(n_l, 2, k, npair, LANES)
    bd = jnp.einsum("ldkpc,de->ldkpec", w4, jnp.eye(2, dtype=w.dtype))
    return bd.reshape(n_l, 2 * k, npair * 2 * LANES).astype(BF16)


def _layer_weights(l, mod_w, mod_b, norm1_g, norm2_g, w_in, q_norm, k_norm, sgu_norm_g, sgu_ws, sgu_bs, rwkv_conv,
                   rwkv_w0, rwkv_w2, rwkv_a0, rwkv_a2, rwkv_g2, rwkv_k_k, rwkv_k_a, rwkv_r_k, rwkv_ln_g, rwkv_ln_b,
                   proj_attn, proj_sgu, proj_rwkv, w_out, router_w, router_bias, exp_gate, exp_up, exp_down,
                   sh_gate, sh_up, sh_down):
    d = D_MODEL
    n_l = w_in.shape[0]
    row = lambda a: a.reshape(n_l, 1, -1)
    pad_e = LANES - N_EXPERTS
    rw_pad = jnp.pad(router_w, ((0, 0), (0, 0), (0, pad_e)))
    stacked = {
        "norm1_g": row(norm1_g), "norm2_g": row(norm2_g),
        "w_in": w_in.astype(BF16),
        "q_norm": jnp.tile(row(q_norm), (1, 1, LANES // HEAD_DIM)),
        "k_norm": jnp.tile(row(k_norm), (1, 1, LANES // HEAD_DIM)),
        "sgu_norm_g": row(sgu_norm_g),
        "sgu_ws": sgu_ws.astype(BF16),
        "sgu_bs": jnp.broadcast_to(sgu_bs[:, :, :, None], (n_l, SGU_GROUPS, CHUNK, LANES)),
        "wkv": {
            "conv": rwkv_conv,
            "w0": rwkv_w0, "a0": rwkv_a0,
            "w2": _lora_pair_layout(rwkv_w2), "a2": _lora_pair_layout(rwkv_a2),
            "g2": rwkv_g2.astype(BF16),
            "k_k": row(rwkv_k_k), "k_a": row(rwkv_k_a), "r_k": row(rwkv_r_k),
            "ln_g": row(rwkv_ln_g), "ln_b": row(rwkv_ln_b),
        },
        "proj_attn": proj_attn.astype(BF16), "proj_sgu": proj_sgu.astype(BF16),
        "proj_rwkv": proj_rwkv.astype(BF16), "w_out": w_out.astype(BF16),
        "router_w_hi": rw_pad.astype(BF16),
        "router_w_lo": (rw_pad - rw_pad.astype(BF16).astype(F32)).astype(BF16),
        "router_bias": jnp.pad(row(router_bias), ((0, 0), (0, 0), (0, pad_e))),
        "sh_gate": sh_gate.astype(BF16), "sh_up": sh_up.astype(BF16), "sh_down": sh_down.astype(BF16),
        "exp_gate": exp_gate.astype(BF16), "exp_up": exp_up.astype(BF16),
        "exp_down": exp_down.astype(BF16).reshape(n_l, N_EXPERTS * D_EXPERT, d),
    }
    return jax.tree.map(lambda a: (a, l), stacked)


def _state_to_blockdiag_t(s):
    b = s.shape[0]
    st = jnp.swapaxes(s, -1, -2).reshape(b, 2, R_HEADS // 2, 2, R_HEAD, R_HEAD)
    bd = jnp.einsum("bdpjkv,ji->bdpjkiv", st, jnp.eye(2, dtype=s.dtype))
    return bd.reshape(b, 2, R_HEADS // 2, LANES, LANES)


def kernel(x_prompt, x_sample, cache_k, cache_v, state_wkv, c, c_ctx, mod_w, mod_b, norm1_g, norm2_g, w_in, q_norm, k_norm, sgu_norm_g, sgu_ws, sgu_bs, rwkv_conv, rwkv_w0, rwkv_w2, rwkv_a0, rwkv_a2, rwkv_g2, rwkv_k_k, rwkv_k_a, rwkv_r_k, rwkv_ln_g, rwkv_ln_b, proj_attn, proj_sgu, proj_rwkv, w_out, router_w, router_bias, exp_gate, exp_up, exp_down, sh_gate, sh_up, sh_down):
    params = (mod_w, mod_b, norm1_g, norm2_g, w_in, q_norm, k_norm, sgu_norm_g, sgu_ws, sgu_bs, rwkv_conv,
              rwkv_w0, rwkv_w2, rwkv_a0, rwkv_a2, rwkv_g2, rwkv_k_k, rwkv_k_a, rwkv_r_k, rwkv_ln_g, rwkv_ln_b,
              proj_attn, proj_sgu, proj_rwkv, w_out, router_w, router_bias, exp_gate, exp_up, exp_down,
              sh_gate, sh_up, sh_down)
    n_b, n_t, d = x_prompt.shape
    n_db, n_dt, _ = x_sample.shape
    assert n_db + 1 <= 8
    cmat = jnp.zeros((8, d), F32).at[0].set(c_ctx).at[1:1 + n_db].set(c)
    mod = _modulation(cmat, mod_w, mod_b)
    rope_tabs = _rope_tables(n_dt)
    xp, xs = x_prompt, x_sample
    ks, vs, ss = [], [], []
    for l in range(DEPTH):
        w = _layer_weights(l, *params)
        xp, (k_new, kv, s_fin) = _layer(xp, (mod, l), w, 0, 0, None, None)
        ks.append(k_new.reshape(n_b, n_t, N_KV_HEADS, HEAD_DIM))
        vs.append(kv.reshape(n_b, n_t, 2 * KV_W)[:, :, KV_W:].reshape(n_b, n_t, N_KV_HEADS, HEAD_DIM))
        ss.append(s_fin)
        ctx = (cache_k[:, l].reshape(n_db, -1, KV_W), cache_v[:, l].reshape(n_db, -1, KV_W),
               _state_to_blockdiag_t(state_wkv[:, l]))
        xs, _ = _layer(xs, (mod, l), w, 1, 1, ctx, rope_tabs)
    return (xp, xs, jnp.stack(ks, axis=1), jnp.stack(vs, axis=1), jnp.stack(ss, axis=1))
```

```python
import functools

import jax
import jax.numpy as jnp
from jax import lax
from jax.experimental import pallas as pl
from jax.experimental.pallas import tpu as pltpu

F32 = jnp.float32
BF16 = jnp.bfloat16
HI = lax.Precision.HIGHEST

D_MODEL = 1024
DEPTH = 2
GRID_W = 64
N_HEADS = 8
N_KV_HEADS = 2
HEAD_DIM = 64
ATTN_W = N_HEADS * HEAD_DIM
KV_W = N_KV_HEADS * HEAD_DIM
N_FREQ = HEAD_DIM // 4
ROPE_THETA = 10000.0
CHUNK = 128
SGU_GROUPS = 4
SGU_W = 512
R_HEADS = 8
R_HEAD = 64
RW = R_HEADS * R_HEAD
LORA_W = 64
LORA_A = 64
LORA_G = 128
DECAY_SCALE = 0.606531
GN_EPS = 64e-5
N_EXPERTS = 64
TOP_K = 6
D_EXPERT = 128
D_SHARED = 256
ROUTED_SCALE = 2.5
EPS = 1e-6
LORA_ALL = 2 * LORA_W + 2 * LORA_A + LORA_G
N_IN = ATTN_W + 2 * KV_W + 2 * SGU_W + 3 * RW + LORA_ALL + 3 * D_MODEL

LOG2_E = 1.4426950408889634
TOKEN_TILE = 512
LANES = 128
WKV_C = 64
MOE_EB = 8
VMEM_LIMIT = 56 * 1024 * 1024


def _cparams(sem, vmem=VMEM_LIMIT):
    return pltpu.CompilerParams(dimension_semantics=sem, vmem_limit_bytes=vmem)


def _dot(a, b, prec=None):
    return jnp.dot(a, b, preferred_element_type=F32, precision=prec)


def _dot_nt(a, b, prec=None):
    return lax.dot_general(a, b, (((1,), (1,)), ((), ())), preferred_element_type=F32, precision=prec)


def _sigmoid(x):
    return 1.0 / (1.0 + jnp.exp(-x))


def _silu(x):
    return x / (1.0 + jnp.exp(-x))


def _head_block_diag(value):
    r = lax.broadcasted_iota(jnp.int32, (LANES, LANES), 0) // HEAD_DIM
    c = lax.broadcasted_iota(jnp.int32, (LANES, LANES), 1) // HEAD_DIM
    return jnp.where(r == c, value, 0.0).astype(F32)


def _mod_row(ref, row):
    return ref[pl.ds(row, 1), :]


def _stacked_spec(w_layers, block=None, tail=None):
    arr, layer = w_layers
    block = tuple(arr.shape[1:] if block is None else block)
    tail = tail or (lambda *g: (0,) * len(block))
    return pl.BlockSpec((None,) + block, lambda *g: (layer,) + tuple(tail(*g)))


def _mod_kernel(c_ref, w_ref, b_ref, o_ref):
    o_ref[0] = _dot(_silu(c_ref[...]), w_ref[0], HI) + b_ref[0]


def _modulation(cmat, mod_w, mod_b):
    n_l, d, n6 = mod_w.shape
    tn = n6 // 4
    return pl.pallas_call(
        _mod_kernel,
        grid=(n_l, n6 // tn),
        in_specs=[
            pl.BlockSpec((8, d), lambda l, j: (0, 0)),
            pl.BlockSpec((1, d, tn), lambda l, j: (l, 0, j)),
            pl.BlockSpec((1, 1, tn), lambda l, j: (l, 0, j)),
        ],
        out_specs=pl.BlockSpec((1, 8, tn), lambda l, j: (l, 0, j)),
        out_shape=jax.ShapeDtypeStruct((n_l, 8, n6), F32),
        compiler_params=_cparams(("arbitrary", "arbitrary")),
        name="modulation",
    )(cmat, mod_w, mod_b.reshape(n_l, 1, n6))


_IN_SPLIT = (ATTN_W, 2 * KV_W, SGU_W, SGU_W, 3 * RW, LORA_ALL, 3 * D_MODEL)


def _inproj_kernel(x_ref, sh_ref, sc_ref, g_ref, w_ref, *out_refs, row0, rstride, bps):
    row = row0 + (pl.program_id(0) // bps) * rstride if rstride else row0
    x = x_ref[...]
    h = x * lax.rsqrt(jnp.mean(x * x, axis=-1, keepdims=True) + EPS) * g_ref[...]
    h = (h * (1.0 + _mod_row(sc_ref, row)) + _mod_row(sh_ref, row)).astype(BF16)
    off = 0
    for ref, size in zip(out_refs, _IN_SPLIT):
        ref[...] = _dot(h, w_ref[:, off:off + size]).astype(ref.dtype)
        off += size


def _inproj(x2, mod_l, g1, w_in_layers, row0, rstride, t_seq):
    m, d = x2.shape
    w_in_b, layer = w_in_layers
    tm = min(TOKEN_TILE, t_seq if rstride else m)
    kern = functools.partial(_inproj_kernel, row0=row0, rstride=rstride, bps=max(t_seq // tm, 1))
    return pl.pallas_call(
        kern,
        grid=(m // tm,),
        in_specs=[
            pl.BlockSpec((tm, d), lambda i: (i, 0)),
            _stacked_spec(mod_l, (8, d), lambda i: (0, 0)),
            _stacked_spec(mod_l, (8, d), lambda i: (0, 1)),
            _stacked_spec(g1),
            pl.BlockSpec((None, d, N_IN), lambda i: (layer, 0, 0), pipeline_mode=pl.Buffered(1)),
        ],
        out_specs=[pl.BlockSpec((tm, s), lambda i: (i, 0)) for s in _IN_SPLIT],
        out_shape=[jax.ShapeDtypeStruct((m, s), BF16 if s == 3 * D_MODEL else F32) for s in _IN_SPLIT],
        compiler_params=_cparams(("arbitrary",)),
        name="inproj",
    )(x2, mod_l[0], mod_l[0], g1[0], w_in_b)


def _attn_kernel(*refs, t_seq, past, tq, latent):
    if latent:
        (q_ref, kv_ref, ck_ref, cv_ref, qg_ref, kg_ref, cq_ref, sq_ref, ckk_ref, skk_ref,
         o_ref, k_s, vt_s) = refs
    else:
        q_ref, kv_ref, qg_ref, kg_ref, o_ref, kn_ref, k_s, vt_s = refs
    lane = lax.broadcasted_iota(jnp.int32, (1, LANES), 1)
    first_half = ((lane // N_FREQ) % 2) == 0
    mean_bd = _head_block_diag(1.0 / HEAD_DIM).astype(BF16)
    q_per_kv = N_HEADS // N_KV_HEADS

    def head_norm(x, g):
        return x * lax.rsqrt(_dot((x * x).astype(BF16), mean_bd) + EPS) * g

    def rope(x, cos, sin_signed):
        rot = jnp.where(first_half, pltpu.roll(x, LANES - N_FREQ, 1), pltpu.roll(x, N_FREQ, 1))
        return x * cos + rot * sin_signed

    @pl.when(pl.program_id(1) == 0)
    def _():
        rb = 256
        for i in range(past // rb):
            r0 = i * rb
            k_s[r0:r0 + rb, :] = ck_ref[0, r0:r0 + rb, :].astype(BF16)
            vt_s[:, r0:r0 + rb] = cv_ref[0, r0:r0 + rb, :].T.astype(BF16)
        for i in range(t_seq // rb):
            r0 = i * rb
            kn = head_norm(kv_ref[0, r0:r0 + rb, 0:KV_W], kg_ref[...])
            if latent:
                kn = rope(kn, ckk_ref[r0:r0 + rb, :], skk_ref[r0:r0 + rb, :])
            else:
                kn_ref[0, r0:r0 + rb, :] = kn
            k_s[past + r0:past + r0 + rb, :] = kn.astype(BF16)
            vt_s[:, past + r0:past + r0 + rb] = kv_ref[0, r0:r0 + rb, KV_W:2 * KV_W].T.astype(BF16)

    slabs = []
    for j in range(ATTN_W // LANES):
        qn = head_norm(q_ref[0, :, LANES * j:LANES * (j + 1)], qg_ref[...])
        if latent:
            qn = rope(qn, cq_ref[...], sq_ref[...])
        slabs.append(qn * (HEAD_DIM ** -0.5 * LOG2_E))
    q_t = jnp.concatenate(slabs, axis=1).T.astype(BF16)
    zero_h = jnp.zeros((HEAD_DIM, tq), BF16)
    qz = []
    for g in range(N_KV_HEADS):
        cols = []
        for h in range(q_per_kv * g, q_per_kv * (g + 1)):
            q_h = q_t[HEAD_DIM * h:HEAD_DIM * (h + 1), :]
            cols.append(jnp.concatenate([q_h, zero_h] if g == 0 else [zero_h, q_h], axis=0))
        qz.append(jnp.concatenate(cols, axis=1))
    s_t = [_dot(k_s[...], x) for x in qz]
    p_t = [jnp.exp2(x - jnp.max(x, axis=0, keepdims=True)) for x in s_t]
    den = [jnp.sum(x, axis=0, keepdims=True) for x in p_t]
    o_t = [_dot(vt_s[HEAD_DIM * g:HEAD_DIM * (g + 1), :], p_t[g].astype(BF16)) / den[g]
           for g in range(N_KV_HEADS)]
    outs =[o_t[g][:, tq * h:tq * (h + 1)] for g in range(N_KV_HEADS) for h in range(q_per_kv)]
    o_ref[0] = jnp.concatenate(outs, axis=0).T.astype(o_ref.dtype)


def _attention(q, kv, qg, kg, ctx_kv, rope_tabs):
    b, t, _ = q.shape
    latent = ctx_kv is not None
    past = ctx_kv[0].shape[1] if latent else 0
    s_all = past + t
    tq = 128 if latent else min(256, t)
    kern = functools.partial(_attn_kernel, t_seq=t, past=past, tq=tq, latent=latent)
    in_specs = [
        pl.BlockSpec((1, tq, ATTN_W), lambda bi, qi: (bi, qi, 0)),
        pl.BlockSpec((1, t, 2 * KV_W), lambda bi, qi: (bi, 0, 0)),
    ]
    args = [q, kv]
    if latent:
        in_specs += [pl.BlockSpec((1, past, KV_W), lambda bi, qi: (bi, 0, 0))] * 2
        args += list(ctx_kv)
    in_specs += [_stacked_spec(qg), _stacked_spec(kg)]
    args += [qg[0], kg[0]]
    out_specs = [pl.BlockSpec((1, tq, ATTN_W), lambda bi, qi: (bi, qi, 0))]
    out_shape = [jax.ShapeDtypeStruct((b, t, ATTN_W), BF16)]
    if latent:
        cos_t, sin_t = rope_tabs
        in_specs += [pl.BlockSpec((tq, LANES), lambda bi, qi: (qi, 0))] * 2
        in_specs += [pl.BlockSpec((t, LANES), lambda bi, qi: (0, 0))] * 2
        args += [cos_t, sin_t, cos_t, sin_t]
    else:
        out_specs.append(pl.BlockSpec((1, t, KV_W), lambda bi, qi: (bi, 0, 0)))
        out_shape.append(jax.ShapeDtypeStruct((b, t, KV_W), F32))
    res = pl.pallas_call(
        kern,
        grid=(b, t // tq),
        in_specs=in_specs,
        out_specs=out_specs,
        out_shape=out_shape,
        scratch_shapes=[pltpu.VMEM((s_all, LANES), BF16), pltpu.VMEM((LANES, s_all), BF16)],
        compiler_params=_cparams(("arbitrary", "arbitrary")),
        name="attention_lat" if latent else "attention_ctx",
    )(*args)
    return res if not latent else (res[0], None)


def _wkv_kernel(*refs, t_seq, latent, nb):
    (r_ref, k_ref, v_ref, lora_ref, cwr_ref, cwk_ref, cwv_ref, w0_ref, a0_ref, w2_ref, a2_ref, g2_ref,
     kk_ref, ka_ref, rk_ref, lng_ref, lnb_ref) = refs[:17]
    rest = refs[17:]
    if latent:
        st0_ref, o_ref = rest[:2]
        scr = rest[2:]
        sf_ref = None
    else:
        o_ref, sf_ref = rest[:2]
        scr = rest[2:]
        st0_ref = None
    tar_s, tv_s, ov_s, arb_s, w_s, n_s, gc_s, st_s, bon_s = scr
    c_len = WKV_C
    c2 = 2 * c_len
    nc = t_seq // c_len

    lane = lax.broadcasted_iota(jnp.int32, (1, LANES), 1)
    lo = lane < R_HEAD
    sum_bd = _head_block_diag(1.0)
    mean_bd = _head_block_diag(1.0 / R_HEAD)
    row_c = lax.broadcasted_iota(jnp.int32, (c_len, LANES), 0)
    ri = lax.broadcasted_iota(jnp.int32, (c2, c2), 0)
    ci = lax.broadcasted_iota(jnp.int32, (c2, c2), 1)
    same_head = (ri // c_len) == (ci // c_len)
    tt, ss = ri % c_len, ci % c_len
    eye2 = jnp.where(ri == ci, 1.0, 0.0).astype(F32)
    strict = (same_head & (ss < tt), same_head & (ss > tt))
    incl = (same_head & (ss <= tt), same_head & (ss >= tt))
    tj = lax.broadcasted_iota(jnp.int32, (c_len, c2), 0)
    sj = lax.broadcasted_iota(jnp.int32, (c_len, c2), 1)
    cum_both = jnp.where(((sj < c_len) & (sj <= tj)) | ((sj >= c_len) & (sj - c_len >= tj)), 1.0, 0.0).astype(BF16)
    sum_bd_b = sum_bd.astype(BF16)
    mean_bd_b = mean_bd.astype(BF16)

    def bdot(a, b):
        return _dot(a.astype(BF16), b.astype(BF16))

    def split2(x):
        hi = x.astype(BF16)
        return hi, (x - hi.astype(F32)).astype(BF16)

    def head_reduce(x, bd_b):
        return _dot(x.astype(BF16), bd_b)

    def stack_heads(x):
        return jnp.concatenate([jnp.where(lo, x, 0.0), jnp.where(lo, 0.0, x)], axis=0)

    def fold_heads(x):
        return x[0:c_len] + x[c_len:c2]

    def slot(bi, c):
        return bi * nc + c

    def conv_tokens(ref, w_ref, bi, c, t0):
        x = ref[bi, pl.ds(t0, c_len), :]
        if isinstance(c, int):
            zero_row = jnp.zeros((1, LANES), F32)
            prev = ref[bi, pl.ds(t0 - 1, 1), :] if c > 0 else zero_row
            nxt = ref[bi, pl.ds(t0 + c_len, 1), :] if c < nc - 1 else zero_row
        else:
            prev = ref[bi, pl.ds(jnp.maximum(t0 - 1, 0), 1), :]
            nxt = ref[bi, pl.ds(jnp.minimum(t0 + c_len, t_seq - 1), 1), :]
            prev = jnp.where(c > 0, prev, 0.0)
            nxt = jnp.where(c < nc - 1, nxt, 0.0)
        xm = jnp.where(row_c == 0, prev, pltpu.roll(x, 1, 0))
        xp = jnp.where(row_c == c_len - 1, nxt, pltpu.roll(x, c_len - 1, 0))
        return xm * w_ref[0:1, :] + x * w_ref[1:2, :] + xp * w_ref[2:3, :]

    def chunk_group(chunks):
        n = len(chunks)
        t0s = [c * c_len if isinstance(c, int) else pl.multiple_of(c * c_len, c_len) for _, c in chunks]
        rc = [conv_tokens(r_ref, cwr_ref, bi, c, t0) for (bi, c), t0 in zip(chunks, t0s)]
        kc = [conv_tokens(k_ref, cwk_ref, bi, c, t0) for (bi, c), t0 in zip(chunks, t0s)]
        vc = [conv_tokens(v_ref, cwv_ref, bi, c, t0) for (bi, c), t0 in zip(chunks, t0s)]
        kkr = [k * kk_ref[...] for k in kc]
        kk = [x * lax.rsqrt(head_reduce(x * x, sum_bd_b) + 1e-12) for x in kkr]
        tw = [jnp.tanh(lora_ref[bi, pl.ds(t0, c_len), 0:2 * LORA_W]) for (bi, _), t0 in zip(chunks, t0s)]
        xa = [lora_ref[bi, pl.ds(t0, c_len), 2 * LORA_W:2 * LORA_W + 2 * LORA_A] for (bi, _), t0 in zip(chunks, t0s)]
        v_hs = [stack_heads(v).astype(BF16) for v in vc]
        prob =[(i, d) for i in range(n) for d in range(2)]
        pre_w = [_dot(x.astype(BF16), w2_ref[...]) for x in tw]
        pre_a = [_dot(x.astype(BF16), a2_ref[...]) for x in xa]
        lw = [-DECAY_SCALE * _sigmoid(w0_ref[d:d + 1, :] + pre_w[i][:, LANES * d:LANES * (d + 1)]) for i, d in prob]
        ag = [_sigmoid(a0_ref[d:d + 1, :] + pre_a[i][:, LANES * d:LANES * (d + 1)]) for i, d in prob]
        kd = [kc[i] * (1.0 + (ag[p] - 1.0) * ka_ref[...]) for p, (i, d) in enumerate(prob)]
        kb = [kk[i] * ag[p] for p, (i, d) in enumerate(prob)]
        for i in range(n):
            bon = head_reduce(rc[i] * (kd[2 * i] + kd[2 * i + 1]) * rk_ref[...], sum_bd_b) * vc[i]
            bon_s[chunks[i][0], pl.ds(t0s[i], c_len), :] = bon
        lw_sp = [split2(x) for x in lw]
        zc = jnp.zeros((c_len, LANES), BF16)
        cum_res = [_dot(cum_both, jnp.concatenate(
            [jnp.concatenate([lw_sp[2 * i][0], lw_sp[2 * i][1], zc, zc], axis=1),
             jnp.concatenate([zc, zc, lw_sp[2 * i + 1][0], lw_sp[2 * i + 1][1]], axis=1)], axis=0)) for i in range(n)]
        cs_in = [cum_res[i][:, 2 * LANES * d:2 * LANES * d + LANES] + cum_res[i][:, 2 * LANES * d + LANES:2 * LANES * (d + 1)]
                 for i, d in prob]
        tot = [cs_in[p][c_len - 1:c_len, :] if d == 0 else cs_in[p][0:1, :] for p, (i, d) in enumerate(prob)]
        e_neg = [jnp.exp(-x) for x in cs_in]
        e_rem = [jnp.exp(tot[p] - cs_in[p]) for p in range(len(prob))]
        a_t = [-kk[i] * jnp.exp(cs_in[p] - (lw_sp[p][0].astype(F32) + lw_sp[p][1].astype(F32)))
               for p, (i, d) in enumerate(prob)]
        r_t = [rc[i] * jnp.exp(cs_in[p]) for p, (i, d) in enumerate(prob)]
        a_hs = [stack_heads(x).astype(BF16) for x in a_t]
        lhs = [jnp.concatenate([a_hs[p], stack_heads(r_t[p]).astype(BF16)], axis=0) for p in range(len(prob))]
        rhs = [jnp.concatenate([stack_heads(kd[p] * e_neg[p]), stack_heads(kb[p] * e_neg[p])], axis=0).astype(BF16)
               for p in range(len(prob))]
        m4 = [_dot_nt(a, b) for a, b in zip(lhs, rhs)]
        a_ak = [jnp.where(strict[d], m4[p][0:c2, 0:c2], 0.0).astype(BF16) for p, (i, d) in enumerate(prob)]
        l_ab = [jnp.where(strict[d], m4[p][0:c2, c2:2 * c2], 0.0) for p, (i, d) in enumerate(prob)]
        a_rk = [jnp.where(incl[d], m4[p][c2:2 * c2, 0:c2], 0.0).astype(BF16) for p, (i, d) in enumerate(prob)]
        for p, (i, d) in enumerate(prob):
            arb_s[d, slot(*chunks[i])] = jnp.where(incl[d], m4[p][c2:2 * c2, c2:2 * c2], 0.0).astype(BF16)
        t_inv = [eye2 + x for x in l_ab]
        pw_b = [x.astype(BF16) for x in l_ab]
        pw_b = [_dot(x, x).astype(BF16) for x in pw_b]
        for _ in range(4):
            res = [_dot(x, jnp.concatenate([x, t.astype(BF16)], axis=1)) for x, t in zip(pw_b, t_inv)]
            pw_b = [r[:, 0:c2].astype(BF16) for r in res]
            t_inv = [t + r[:, c2:2 * c2] for t, r in zip(t_inv, res)]
        t_inv = [t + _dot(x, t.astype(BF16)) for x, t in zip(pw_b, t_inv)]
        t_b = [t.astype(BF16) for t in t_inv]
        av = [_dot(a_ak[p], v_hs[i]).astype(BF16) for p, (i, d) in enumerate(prob)]
        ov = [fold_heads(_dot(a_rk[p], v_hs[i])) for p, (i, d) in enumerate(prob)]
        tt = [_dot(t, jnp.concatenate([a, x], axis=1)) for t, a, x in zip(t_b, a_hs, av)]
        ta = [fold_heads(r[:, 0:LANES]) for r in tt]
        tv = [fold_heads(r[:, LANES:2 * LANES]) for r in tt]
        kbt = [jnp.concatenate([kd[p] * e_rem[p], kb[p] * e_rem[p]], axis=0).T.astype(BF16) for p in range(len(prob))]
        zero_c = jnp.zeros((c_len, LANES), BF16)
        upd_rhs = [jnp.concatenate([jnp.concatenate([vc[i].astype(BF16), zero_c], axis=1),
                                    jnp.concatenate([tv[p].astype(BF16), ta[p].astype(BF16)], axis=1)], axis=0)
                   for p, (i, d) in enumerate(prob)]
        nw = [_dot(a, b) for a, b in zip(kbt, upd_rhs)]
        for p, (i, d) in enumerate(prob):
            c = slot(*chunks[i])
            tar_s[d, c] = jnp.concatenate([ta[p], r_t[p]], axis=0).astype(BF16)
            tv_s[d, c] = tv[p]
            ov_s[d, c] = ov[p]
            n_s[d, c] = jnp.where(bd_mask, nw[p][:, 0:LANES], 0.0)
            w_s[d, c] = jnp.where(bd_mask, nw[p][:, LANES:2 * LANES], 0.0).astype(BF16)
            gc_s[d, c] = jnp.broadcast_to(jnp.exp(tot[p]), (LANES, LANES)).T

    bd_mask = sum_bd > 0.5
    group1 = 8
    if nb > 1 or nc <= group1:
        every = [(bi, c) for bi in range(nb) for c in range(nc)]
        for g0 in range(0, len(every), group1):
            chunk_group(every[g0:g0 + group1])
    else:
        assert nb == 1 and nc % group1 == 0

        def phase1(i, carry):
            chunk_group([(0, i * group1 + j) for j in range(group1)])
            return carry

        lax.fori_loop(0, nc // group1, phase1, 0)

    chains = [(bi, d) for bi in range(nb) for d in range(2)]

    def phase2(i, carry):
        cs = [slot(bi, i if d == 0 else nc - 1 - i) for bi, d in chains]
        st_b = [x.astype(BF16) for x in carry]
        for (bi, d), c, x in zip(chains, cs, st_b):
            st_s[d, c] = x
        prod = [_dot(w_s[d, c], x) for (bi, d), c, x in zip(chains, cs, st_b)]
        return tuple(gc_s[d, c] * s + pr + n_s[d, c] for (bi, d), c, s, pr in zip(chains, cs, carry, prod))

    if latent:
        init = tuple(st0_ref[bi, d, 0] for bi, d in chains)
    else:
        init = (jnp.zeros((LANES, LANES), F32),) * len(chains)
    fin = lax.fori_loop(0, nc, phase2, init)
    if sf_ref is not None:
        for (bi, d), st in zip(chains, fin):
            s_vk = st.T
            sf_ref[bi, d, 0] = s_vk[0:R_HEAD, 0:R_HEAD]
            sf_ref[bi, d, 1] = pltpu.roll(s_vk[R_HEAD:2 * R_HEAD, :], R_HEAD, 1)[:, 0:R_HEAD]

    def out_group(chunks):
        prob = [(bi, c, slot(bi, c), d) for bi, c in chunks for d in range(2)]
        both = [_dot(tar_s[d, s], st_s[d, s]) for _, _, s, d in prob]
        u = [both[p][0:c_len] + tv_s[d, s] for p, (_, _, s, d) in enumerate(prob)]
        ou = [_dot(arb_s[d, s], stack_heads(u[p]).astype(BF16)) for p, (_, _, s, d) in enumerate(prob)]
        o = [ov_s[d, s] + both[p][c_len:c2] + fold_heads(ou[p]) for p, (_, _, s, d) in enumerate(prob)]
        of = [o[2 * j] + o[2 * j + 1] for j in range(len(chunks))]
        xc = [x - head_reduce(x, mean_bd_b) for x in of]
        var = [head_reduce(x * x, mean_bd_b) for x in xc]
        for j, (bi, c) in enumerate(chunks):
            t0 = c * c_len if isinstance(c, int) else pl.multiple_of(c * c_len, c_len)
            on = xc[j] * lax.rsqrt(var[j] + GN_EPS) * lng_ref[...] + lnb_ref[...]
            xg = lora_ref[bi, pl.ds(t0, c_len), 2 * LORA_W + 2 * LORA_A:LORA_ALL]
            g_out = _dot(_sigmoid(xg).astype(BF16), g2_ref[...])
            o_ref[bi, pl.ds(t0, c_len), :] = ((on + bon_s[bi, pl.ds(t0, c_len), :]) * g_out).astype(o_ref.dtype)

    group3 = 8
    if nb > 1 or nc <= group3:
        every = [(bi, c) for bi in range(nb) for c in range(nc)]
        for g0 in range(0, len(every), group3):
            out_group(every[g0:g0 + group3])
    else:
        assert nb == 1 and nc % group3 == 0

        def phase3(i, carry):
            out_group([(0, i * group3 + j) for j in range(group3)])
            return carry

        lax.fori_loop(0, nc // group3, phase3, 0)


def _wkv(rkv, lora, wts, st0):
    b, t, _ = rkv.shape
    latent = st0 is not None
    nc = t // WKV_C
    npair = RW // LANES
    nb = 4 if (4 * nc <= 16 and b % 4 == 0) else 1
    kern = functools.partial(_wkv_kernel, t_seq=t, latent=latent, nb=nb)

    def seq_spec(col0):
        return pl.BlockSpec((nb, t, LANES), lambda bi, p: (bi, 0, col0 + p))

    def row_spec(name, rows, col0=0, width=LANES):
        return _stacked_spec(wts[name], (rows, width), lambda bi, p: (0, col0 + p))

    in_specs = [
        seq_spec(0), seq_spec(npair), seq_spec(2 * npair),
        pl.BlockSpec((nb, t, LORA_ALL), lambda bi, p: (bi, 0, 0)),
        row_spec("conv", 3, 0), row_spec("conv", 3, npair), row_spec("conv", 3, 2 * npair),
        row_spec("w0", 2), row_spec("a0", 2),
        row_spec("w2", LANES, width=2 * LANES), row_spec("a2", LANES, width=2 * LANES),
        row_spec("g2", LANES),
        row_spec("k_k", 1), row_spec("k_a", 1), row_spec("r_k", 1), row_spec("ln_g", 1), row_spec("ln_b", 1),
    ]
    args = [rkv, rkv, rkv, lora] + [wts[n][0] for n in ("conv", "conv", "conv", "w0", "a0", "w2", "a2", "g2",
                                                        "k_k", "k_a", "r_k", "ln_g", "ln_b")]
    out_specs = [seq_spec(0)]
    out_shape = [jax.ShapeDtypeStruct((b, t, RW), BF16)]
    if latent:
        in_specs.append(pl.BlockSpec((nb, 2, 1, LANES, LANES), lambda bi, p: (bi, 0, p, 0, 0)))
        args.append(st0)
    else:
        out_specs.append(pl.BlockSpec((nb, 2, 2, R_HEAD, R_HEAD), lambda bi, p: (bi, 0, p, 0, 0)))
        out_shape.append(jax.ShapeDtypeStruct((b, 2, R_HEADS, R_HEAD, R_HEAD), F32))
    chunk_rows = pltpu.VMEM((2, nb * nc, WKV_C, LANES), F32)
    chunk_sq = pltpu.VMEM((2, nb * nc, LANES, LANES), F32)
    chunk_sq_b = pltpu.VMEM((2, nb * nc, LANES, LANES), BF16)
    res = pl.pallas_call(
        kern,
        grid=(b // nb, npair),
        in_specs=in_specs,
        out_specs=out_specs,
        out_shape=out_shape,
        scratch_shapes=[chunk_sq_b, chunk_rows, chunk_rows, chunk_sq_b, chunk_sq_b, chunk_sq, chunk_sq, chunk_sq_b,
                        pltpu.VMEM((nb, t, LANES), F32)],
        compiler_params=_cparams(("arbitrary", "arbitrary")),
        name="wkv_lat" if latent else "wkv_ctx",
    )(*args)
    return (res[0], None) if latent else (res[0], res[1])


def _merge_kernel(x_ref, oa_ref, su_ref, sv_ref, or_ref, gates_ref, gt_ref, sg_ref, ws_ref, bs_ref,
                  pa_ref, ps_ref, pr_ref, wo_ref, o_ref, *, row0, rstride, bps):
    row = row0 + (pl.program_id(0) // bps) * rstride if rstride else row0
    d = D_MODEL
    sgu_rows = []
    for c in range(x_ref.shape[0] // CHUNK):
        rows = slice(CHUNK * c, CHUNK * (c + 1))
        v = sv_ref[rows, :]
        vn = (v * lax.rsqrt(jnp.mean(v * v, axis=-1, keepdims=True) + EPS) * sg_ref[...]).astype(BF16)
        sgu_rows.append(jnp.concatenate(
            [(su_ref[rows, LANES * g:LANES * (g + 1)]
              * (_dot(ws_ref[g], vn[:, LANES * g:LANES * (g + 1)]) + bs_ref[g])).astype(BF16)
             for g in range(SGU_GROUPS)], axis=1))
    o_sgu = jnp.concatenate(sgu_rows, axis=0)
    gate = lambda j: _sigmoid(gates_ref[:, j * d:(j + 1) * d].astype(F32))
    merged = (gate(0) * _dot(oa_ref[...].astype(BF16), pa_ref[...])
              + gate(1) * _dot(o_sgu, ps_ref[...])
              + gate(2) * _dot(or_ref[...].astype(BF16), pr_ref[...]))
    o_ref[...] = x_ref[...] + _mod_row(gt_ref, row) * _dot(merged.astype(BF16), wo_ref[...])


def _merge(x2, oa, su, sv, orw, gates, mod_l, sgu_g, sgu_ws, sgu_bs, pa, ps, pr, wo, row0, rstride, t_seq):
    m, d = x2.shape
    tm = min(TOKEN_TILE, t_seq if rstride else m)
    assert tm % CHUNK == 0 and t_seq % CHUNK == 0
    kern = functools.partial(_merge_kernel, row0=row0, rstride=rstride, bps=max(t_seq // tm, 1))
    tok = lambda w: pl.BlockSpec((tm, w), lambda i: (i, 0))
    weights = (sgu_g, sgu_ws, sgu_bs, pa, ps, pr, wo)
    return pl.pallas_call(
        kern,
        grid=(m // tm,),
        in_specs=[tok(d), tok(ATTN_W), tok(SGU_W), tok(SGU_W), tok(RW), tok(3 * d),
                  _stacked_spec(mod_l, (8, d), lambda i: (0, 2))]
                 + [_stacked_spec(a) for a in weights],
        out_specs=tok(d),
        out_shape=jax.ShapeDtypeStruct((m, d), F32),
        compiler_params=_cparams(("arbitrary",)),
        name="merge",
    )(x2, oa, su, sv, orw, gates, mod_l[0], *[a[0] for a in weights])


def _moe_kernel(x_ref, sh_ref, sc_ref, gt_ref, g2_ref, rw_ref, rb_ref, shg_ref, shu_ref, shd_ref,
                wg_ref, wu_ref, wd_ref, o_ref, h_s, comb_s, acc_s, *, row0, rstride, bps):
    row = row0 + (pl.program_id(0) // bps) * rstride if rstride else row0
    e = pl.program_id(1)
    tm = x_ref.shape[0]

    @pl.when(e == 0)
    def _():
        x = x_ref[...]
        h = x * lax.rsqrt(jnp.mean(x * x, axis=-1, keepdims=True) + EPS) * g2_ref[...]
        h = h * (1.0 + _mod_row(sc_ref, row)) + _mod_row(sh_ref, row)
        hb = h.astype(BF16)
        h_lo = (h - hb.astype(F32)).astype(BF16)
        hw = _dot(hb, rw_ref[...])
        logits = hw[:, 0:LANES] + hw[:, LANES:2 * LANES] + _dot(h_lo, rw_ref[:, 0:LANES])
        scores = _sigmoid(logits)
        sc_t = scores.T[0:N_EXPERTS]
        sel = (scores + rb_ref[...]).T[0:N_EXPERTS]
        eid = lax.broadcasted_iota(jnp.int32, (N_EXPERTS, tm), 0).astype(F32)
        picked = jnp.zeros((N_EXPERTS, tm), jnp.bool_)
        for _ in range(TOP_K):
            best = jnp.max(sel, axis=0, keepdims=True)
            first = jnp.min(jnp.where(sel == best, eid, float(N_EXPERTS)), axis=0, keepdims=True)
            hit = eid == first
            picked = jnp.logical_or(picked, hit)
            sel = jnp.where(hit, -jnp.inf, sel)
        sw = jnp.where(picked, sc_t, 0.0)
        comb_t = sw / jnp.sum(sw, axis=0, keepdims=True) * ROUTED_SCALE
        comb = jnp.concatenate([comb_t, jnp.zeros((LANES - N_EXPERTS, tm), F32)], axis=0).T
        for b in range(N_EXPERTS // MOE_EB):
            comb_s[b] = comb if b == 0 else pltpu.roll(comb, LANES - MOE_EB * b, 1)
        h_s[...] = hb
        acc_s[...] = _dot((_silu(_dot(hb, shg_ref[...])) * _dot(hb, shu_ref[...])).astype(BF16), shd_ref[...])

    hb = h_s[...]
    comb = comb_s[e]
    hid = []
    for j in range(MOE_EB):
        gu = _dot(hb, jnp.concatenate([wg_ref[j], wu_ref[j]], axis=1))
        cw = jnp.broadcast_to(comb[:, j:j + 1], (tm, D_EXPERT))
        hid.append((_silu(gu[:, 0:D_EXPERT]) * gu[:, D_EXPERT:2 * D_EXPERT] * cw).astype(BF16))
    acc_s[...] += _dot(jnp.concatenate(hid, axis=1), wd_ref[...])

    @pl.when(e == pl.num_programs(1) - 1)
    def _():
        o_ref[...] = x_ref[...] + _mod_row(gt_ref, row) * acc_s[...]


def _moe(x2, mod_l, g2, rw_hl, rb_p, shg, shu, shd, wg_l, wu_l, wd_l, row0, rstride, t_seq):
    m, d = x2.shape
    (wg, layer), (wu, _), (wd, _) = wg_l, wu_l, wd_l
    tm = min(1024, t_seq if rstride else m)
    n_hid = MOE_EB * D_EXPERT
    kern = functools.partial(_moe_kernel, row0=row0, rstride=rstride, bps=max(t_seq // tm, 1))
    small = (g2, rw_hl, rb_p, shg, shu, shd)
    return pl.pallas_call(
        kern,
        grid=(m // tm, N_EXPERTS // MOE_EB),
        in_specs=[
            pl.BlockSpec((tm, d), lambda i, e: (i, 0)),
            _stacked_spec(mod_l, (8, d), lambda i, e: (0, 3)),
            _stacked_spec(mod_l, (8, d), lambda i, e: (0, 4)),
            _stacked_spec(mod_l, (8, d), lambda i, e: (0, 5)),
            *[_stacked_spec(a) for a in small],
            pl.BlockSpec((None, MOE_EB, d, D_EXPERT), lambda i, e: (layer, e, 0, 0)),
            pl.BlockSpec((None, MOE_EB, d, D_EXPERT), lambda i, e: (layer, e, 0, 0)),
            pl.BlockSpec((None, n_hid, d), lambda i, e: (layer, e, 0)),
        ],
        out_specs=pl.BlockSpec((tm, d), lambda i, e: (i, 0)),
        out_shape=jax.ShapeDtypeStruct((m, d), F32),
        scratch_shapes=[pltpu.VMEM((tm, d), BF16), pltpu.VMEM((N_EXPERTS // MOE_EB, tm, LANES), F32),
                        pltpu.VMEM((tm, d), F32)],
        compiler_params=_cparams(("arbitrary", "arbitrary")),
        name="moe",
    )(x2, mod_l[0], mod_l[0], mod_l[0], *[a[0] for a in small], wg, wu, wd)


def _layer(x, mod_l, w, row0, rstride, ctx, rope_tabs):
    b, t, d = x.shape
    x2 = x.reshape(b * t, d)
    q, kv, su, sv, rkv, lora, gates = _inproj(x2, mod_l, w["norm1_g"], w["w_in"], row0, rstride, t)
    if ctx is None:
        o_attn, k_new = _attention(q.reshape(b, t, -1), kv.reshape(b, t, -1), w["q_norm"], w["k_norm"], None, None)
        st0 = None
    else:
        o_attn, k_new = _attention(q.reshape(b, t, -1), kv.reshape(b, t, -1), w["q_norm"], w["k_norm"],
                                   ctx[:2], rope_tabs)
        st0 = ctx[2]
    o_wkv, s_fin = _wkv(rkv.reshape(b, t, -1), lora.reshape(b, t, -1), w["wkv"], st0)
    x1 = _merge(x2, o_attn.reshape(b * t, -1), su, sv, o_wkv.reshape(b * t, -1), gates, mod_l,
                w["sgu_norm_g"], w["sgu_ws"], w["sgu_bs"],
                w["proj_attn"], w["proj_sgu"], w["proj_rwkv"], w["w_out"], row0, rstride, t)
    x_out = _moe(x1, mod_l, w["norm2_g"], w["router_w_hl"], w["router_bias"],
                 w["sh_gate"], w["sh_up"], w["sh_down"],
                 w["exp_gate"], w["exp_up"], w["exp_down"], row0, rstride, t)
    return x_out.reshape(b, t, d), (k_new, kv, s_fin)


def _rope_tables(n_tok):
    t = jnp.arange(n_tok)
    row = (t // GRID_W).astype(F32)
    col = (t % GRID_W).astype(F32)
    inv = ROPE_THETA ** (-jnp.arange(N_FREQ, dtype=F32) / N_FREQ)
    ang = jnp.stack([row[:, None] * inv[None, :], col[:, None] * inv[None, :]], axis=1)
    cos, sin = jnp.cos(ang), jnp.sin(ang)
    cos64 = jnp.stack([cos, cos], axis=2).reshape(n_tok, HEAD_DIM)
    sin64 = jnp.stack([-sin, sin], axis=2).reshape(n_tok, HEAD_DIM)
    return jnp.tile(cos64, (1, LANES // HEAD_DIM)), jnp.tile(sin64, (1, LANES // HEAD_DIM))


def _lora_pair_layout(w):
    npair = RW // LANES
    n_l, _, k, _ = w.shape
    w4 = w.reshape(n_l, 2, k, npair, LANES)
    bd = jnp.einsum("ldkpc,de->ldkpec", w4, jnp.eye(2, dtype=w.dtype))
    return bd.reshape(n_l, 2 * k, npair * 2 * LANES).astype(BF16)


def _layer_weights(l, mod_w, mod_b, norm1_g, norm2_g, w_in, q_norm, k_norm, sgu_norm_g, sgu_ws, sgu_bs, rwkv_conv,
                   rwkv_w0, rwkv_w2, rwkv_a0, rwkv_a2, rwkv_g2, rwkv_k_k, rwkv_k_a, rwkv_r_k, rwkv_ln_g, rwkv_ln_b,
                   proj_attn, proj_sgu, proj_rwkv, w_out, router_w, router_bias, exp_gate, exp_up, exp_down,
                   sh_gate, sh_up, sh_down):
    d = D_MODEL
    n_l = w_in.shape[0]
    row = lambda a: a.reshape(n_l, 1, -1)
    pad_e = LANES - N_EXPERTS
    rw_pad = jnp.pad(router_w, ((0, 0), (0, 0), (0, pad_e)))
    stacked = {
        "norm1_g": row(norm1_g), "norm2_g": row(norm2_g),
        "w_in": w_in.astype(BF16),
        "q_norm": jnp.tile(row(q_norm), (1, 1, LANES // HEAD_DIM)),
        "k_norm": jnp.tile(row(k_norm), (1, 1, LANES // HEAD_DIM)),
        "sgu_norm_g": row(sgu_norm_g),
        "sgu_ws": sgu_ws.astype(BF16),
        "sgu_bs": jnp.broadcast_to(sgu_bs[:, :, :, None], (n_l, SGU_GROUPS, CHUNK, LANES)),
        "wkv": {
            "conv": rwkv_conv,
            "w0": rwkv_w0, "a0": rwkv_a0,
            "w2": _lora_pair_layout(rwkv_w2), "a2": _lora_pair_layout(rwkv_a2),
            "g2": rwkv_g2.astype(BF16),
            "k_k": row(rwkv_k_k), "k_a": row(rwkv_k_a), "r_k": row(rwkv_r_k),
            "ln_g": row(rwkv_ln_g), "ln_b": row(rwkv_ln_b),
        },
        "proj_attn": proj_attn.astype(BF16), "proj_sgu": proj_sgu.astype(BF16),
        "proj_rwkv": proj_rwkv.astype(BF16), "w_out": w_out.astype(BF16),
        "router_w_hl": jnp.concatenate([rw_pad.astype(BF16),
                                        (rw_pad - rw_pad.astype(BF16).astype(F32)).astype(BF16)], axis=-1),
        "router_bias": jnp.pad(row(router_bias), ((0, 0), (0, 0), (0, pad_e))),
        "sh_gate": sh_gate.astype(BF16), "sh_up": sh_up.astype(BF16), "sh_down": sh_down.astype(BF16),
        "exp_gate": exp_gate.astype(BF16), "exp_up": exp_up.astype(BF16),
        "exp_down": exp_down.astype(BF16).reshape(n_l, N_EXPERTS * D_EXPERT, d),
    }
    return jax.tree.map(lambda a: (a, l), stacked)


def _state_to_blockdiag_t(s):
    b = s.shape[0]
    st = jnp.swapaxes(s, -1, -2).reshape(b, 2, R_HEADS // 2, 2, R_HEAD, R_HEAD)
    bd = jnp.einsum("bdpjkv,ji->bdpjkiv", st, jnp.eye(2, dtype=s.dtype))
    return bd.reshape(b, 2, R_HEADS // 2, LANES, LANES)


def kernel(x_prompt, x_sample, cache_k, cache_v, state_wkv, c, c_ctx, mod_w, mod_b, norm1_g, norm2_g, w_in, q_norm, k_norm, sgu_norm_g, sgu_ws, sgu_bs, rwkv_conv, rwkv_w0, rwkv_w2, rwkv_a0, rwkv_a2, rwkv_g2, rwkv_k_k, rwkv_k_a, rwkv_r_k, rwkv_ln_g, rwkv_ln_b, proj_attn, proj_sgu, proj_rwkv, w_out, router_w, router_bias, exp_gate, exp_up, exp_down, sh_gate, sh_up, sh_down):
    params = (mod_w, mod_b, norm1_g, norm2_g, w_in, q_norm, k_norm, sgu_norm_g, sgu_ws, sgu_bs, rwkv_conv,
              rwkv_w0, rwkv_w2, rwkv_a0, rwkv_a2, rwkv_g2, rwkv_k_k, rwkv_k_a, rwkv_r_k, rwkv_ln_g, rwkv_ln_b,
              proj_attn, proj_sgu, proj_rwkv, w_out, router_w, router_bias, exp_gate, exp_up, exp_down,
              sh_gate, sh_up, sh_down)
    n_b, n_t, d = x_prompt.shape
    n_db, n_dt, _ = x_sample.shape
    assert n_db + 1 <= 8
    cmat = jnp.zeros((8, d), F32).at[0].set(c_ctx).at[1:1 + n_db].set(c)
    mod = _modulation(cmat, mod_w, mod_b)
    rope_tabs = _rope_tables(n_dt)
    xp, xs = x_prompt, x_sample
    ks, vs, ss = [], [], []
    for l in range(DEPTH):
        w = _layer_weights(l, *params)
        xp, (k_new, kv, s_fin) = _layer(xp, (mod, l), w, 0, 0, None, None)
        ks.append(k_new.reshape(n_b, n_t, N_KV_HEADS, HEAD_DIM))
        vs.append(kv.reshape(n_b, n_t, 2 * KV_W)[:, :, KV_W:].reshape(n_b, n_t, N_KV_HEADS, HEAD_DIM))
        ss.append(s_fin)
        ctx = (cache_k[:, l].reshape(n_db, -1, KV_W), cache_v[:, l].reshape(n_db, -1, KV_W),
               _state_to_blockdiag_t(state_wkv[:, l]))
        xs, _ = _layer(xs, (mod, l), w, 1, 1, ctx, rope_tabs)
    return (xp, xs, jnp.stack(ks, axis=1), jnp.stack(vs, axis=1), jnp.stack(ss, axis=1))
```

```python
import functools

import jax
import jax.numpy as jnp
from jax import lax
from jax.experimental import pallas as pl
from jax.experimental.pallas import tpu as pltpu

F32 = jnp.float32
BF16 = jnp.bfloat16
HI = lax.Precision.HIGHEST

D_MODEL = 1024
DEPTH = 2
GRID_W = 64
N_HEADS = 8
N_KV_HEADS = 2
HEAD_DIM = 64
ATTN_W = N_HEADS * HEAD_DIM
KV_W = N_KV_HEADS * HEAD_DIM
N_FREQ = HEAD_DIM // 4
ROPE_THETA = 10000.0
CHUNK = 128
SGU_GROUPS = 4
SGU_W = 512
R_HEADS = 8
R_HEAD = 64
RW = R_HEADS * R_HEAD
LORA_W = 64
LORA_A = 64
LORA_G = 128
DECAY_SCALE = 0.606531
GN_EPS = 64e-5
N_EXPERTS = 64
TOP_K = 6
D_EXPERT = 128
D_SHARED = 256
ROUTED_SCALE = 2.5
EPS = 1e-6
LORA_ALL = 2 * LORA_W + 2 * LORA_A + LORA_G
N_IN = ATTN_W + 2 * KV_W + 2 * SGU_W + 3 * RW + LORA_ALL + 3 * D_MODEL

LOG2_E = 1.4426950408889634
TOKEN_TILE = 512
LANES = 128
WKV_C = 64
MOE_EB = 8
VMEM_LIMIT = 56 * 1024 * 1024


def _cparams(sem, vmem=VMEM_LIMIT):
    return pltpu.CompilerParams(dimension_semantics=sem, vmem_limit_bytes=vmem)


def _dot(a, b, prec=None):
    return jnp.dot(a, b, preferred_element_type=F32, precision=prec)


def _dot_nt(a, b, prec=None):
    return lax.dot_general(a, b, (((1,), (1,)), ((), ())), preferred_element_type=F32, precision=prec)


def _sigmoid(x):
    return 1.0 / (1.0 + jnp.exp(-x))


def _silu(x):
    return x / (1.0 + jnp.exp(-x))


def _head_block_diag(value):
    r = lax.broadcasted_iota(jnp.int32, (LANES, LANES), 0) // HEAD_DIM
    c = lax.broadcasted_iota(jnp.int32, (LANES, LANES), 1) // HEAD_DIM
    return jnp.where(r == c, value, 0.0).astype(F32)


def _mod_row(ref, row):
    return ref[pl.ds(row, 1), :]


def _stacked_spec(w_layers, block=None, tail=None):
    arr, layer = w_layers
    block = tuple(arr.shape[1:] if block is None else block)
    tail = tail or (lambda *g: (0,) * len(block))
    return pl.BlockSpec((None,) + block, lambda *g: (layer,) + tuple(tail(*g)))


def _mod_kernel(c_ref, w_ref, b_ref, o_ref):
    o_ref[0] = _dot(_silu(c_ref[...]), w_ref[0], HI) + b_ref[0]


def _modulation(cmat, mod_w, mod_b):
    n_l, d, n6 = mod_w.shape
    tn = n6 // 4
    return pl.pallas_call(
        _mod_kernel,
        grid=(n_l, n6 // tn),
        in_specs=[
            pl.BlockSpec((8, d), lambda l, j: (0, 0)),
            pl.BlockSpec((1, d, tn), lambda l, j: (l, 0, j)),
            pl.BlockSpec((1, 1, tn), lambda l, j: (l, 0, j)),
        ],
        out_specs=pl.BlockSpec((1, 8, tn), lambda l, j: (l, 0, j)),
        out_shape=jax.ShapeDtypeStruct((n_l, 8, n6), F32),
        compiler_params=_cparams(("arbitrary", "arbitrary")),
        name="modulation",
    )(cmat, mod_w, mod_b.reshape(n_l, 1, n6))


_IN_SPLIT = (ATTN_W, 2 * KV_W, SGU_W, SGU_W, 3 * RW, LORA_ALL, 3 * D_MODEL)


def _inproj_kernel(x_ref, sh_ref, sc_ref, g_ref, w_ref, *out_refs, row0, rstride, bps):
    row = row0 + (pl.program_id(0) // bps) * rstride if rstride else row0
    x = x_ref[...]
    h = x * lax.rsqrt(jnp.mean(x * x, axis=-1, keepdims=True) + EPS) * g_ref[...]
    h = (h * (1.0 + _mod_row(sc_ref, row)) + _mod_row(sh_ref, row)).astype(BF16)
    off = 0
    for ref, size in zip(out_refs, _IN_SPLIT):
        ref[...] = _dot(h, w_ref[:, off:off + size]).astype(ref.dtype)
        off += size


def _inproj(x2, mod_l, g1, w_in_layers, row0, rstride, t_seq):
    m, d = x2.shape
    w_in_b, layer = w_in_layers
    tm = min(TOKEN_TILE, t_seq if rstride else m)
    kern = functools.partial(_inproj_kernel, row0=row0, rstride=rstride, bps=max(t_seq // tm, 1))
    return pl.pallas_call(
        kern,
        grid=(m // tm,),
        in_specs=[
            pl.BlockSpec((tm, d), lambda i: (i, 0)),
            _stacked_spec(mod_l, (8, d), lambda i: (0, 0)),
            _stacked_spec(mod_l, (8, d), lambda i: (0, 1)),
            _stacked_spec(g1),
            pl.BlockSpec((None, d, N_IN), lambda i: (layer, 0, 0), pipeline_mode=pl.Buffered(1)),
        ],
        out_specs=[pl.BlockSpec((tm, s), lambda i: (i, 0)) for s in _IN_SPLIT],
        out_shape=[jax.ShapeDtypeStruct((m, s), BF16 if s == 3 * D_MODEL else F32) for s in _IN_SPLIT],
        compiler_params=_cparams(("arbitrary",)),
        name="inproj",
    )(x2, mod_l[0], mod_l[0], g1[0], w_in_b)


def _attn_kernel(*refs, t_seq, past, tq, latent):
    if latent:
        (q_ref, kv_ref, ck_ref, cv_ref, qg_ref, kg_ref, cq_ref, sq_ref, ckk_ref, skk_ref,
         o_ref, k_s, vt_s) = refs
    else:
        q_ref, kv_ref, qg_ref, kg_ref, o_ref, kn_ref, k_s, vt_s = refs
    lane = lax.broadcasted_iota(jnp.int32, (1, LANES), 1)
    first_half = ((lane // N_FREQ) % 2) == 0
    mean_bd = _head_block_diag(1.0 / HEAD_DIM).astype(BF16)
    q_per_kv = N_HEADS // N_KV_HEADS

    def head_norm(x, g):
        return x * lax.rsqrt(_dot((x * x).astype(BF16), mean_bd) + EPS) * g

    def rope(x, cos, sin_signed):
        rot = jnp.where(first_half, pltpu.roll(x, LANES - N_FREQ, 1), pltpu.roll(x, N_FREQ, 1))
        return x * cos + rot * sin_signed

    @pl.when(pl.program_id(1) == 0)
    def _():
        rb = 256
        for i in range(past // rb):
            r0 = i * rb
            k_s[r0:r0 + rb, :] = ck_ref[0, r0:r0 + rb, :].astype(BF16)
            vt_s[:, r0:r0 + rb] = cv_ref[0, r0:r0 + rb, :].T.astype(BF16)
        for i in range(t_seq // rb):
            r0 = i * rb
            kn = head_norm(kv_ref[0, r0:r0 + rb, 0:KV_W], kg_ref[...])
            if latent:
                kn = rope(kn, ckk_ref[r0:r0 + rb, :], skk_ref[r0:r0 + rb, :])
            else:
                kn_ref[0, r0:r0 + rb, :] = kn
            k_s[past + r0:past + r0 + rb, :] = kn.astype(BF16)
            vt_s[:, past + r0:past + r0 + rb] = kv_ref[0, r0:r0 + rb, KV_W:2 * KV_W].T.astype(BF16)

    slabs = []
    for j in range(ATTN_W // LANES):
        qn = head_norm(q_ref[0, :, LANES * j:LANES * (j + 1)], qg_ref[...])
        if latent:
            qn = rope(qn, cq_ref[...], sq_ref[...])
        slabs.append(qn * (HEAD_DIM ** -0.5 * LOG2_E))
    q_t = jnp.concatenate(slabs, axis=1).T.astype(BF16)
    zero_h = jnp.zeros((HEAD_DIM, tq), BF16)
    qz = []
    for g in range(N_KV_HEADS):
        cols = []
        for h in range(q_per_kv * g, q_per_kv * (g + 1)):
            q_h = q_t[HEAD_DIM * h:HEAD_DIM * (h + 1), :]
            cols.append(jnp.concatenate([q_h, zero_h] if g == 0 else [zero_h, q_h], axis=0))
        qz.append(jnp.concatenate(cols, axis=1))
    s_t = [_dot(k_s[...], x) for x in qz]
    p_t = [jnp.exp2(x - jnp.max(x, axis=0, keepdims=True)) for x in s_t]
    den = [jnp.sum(x, axis=0, keepdims=True) for x in p_t]
    o_t = [_dot(vt_s[HEAD_DIM * g:HEAD_DIM * (g + 1), :], p_t[g].astype(BF16)) / den[g]
           for g in range(N_KV_HEADS)]
    outs =[o_t[g][:, tq * h:tq * (h + 1)] for g in range(N_KV_HEADS) for h in range(q_per_kv)]
    o_ref[0] = jnp.concatenate(outs, axis=0).T.astype(o_ref.dtype)


def _attention(q, kv, qg, kg, ctx_kv, rope_tabs):
    b, t, _ = q.shape
    latent = ctx_kv is not None
    past = ctx_kv[0].shape[1] if latent else 0
    s_all = past + t
    tq = min(256, t)
    kern = functools.partial(_attn_kernel, t_seq=t, past=past, tq=tq, latent=latent)
    in_specs = [
        pl.BlockSpec((1, tq, ATTN_W), lambda bi, qi: (bi, qi, 0)),
        pl.BlockSpec((1, t, 2 * KV_W), lambda bi, qi: (bi, 0, 0)),
    ]
    args = [q, kv]
    if latent:
        in_specs += [pl.BlockSpec((1, past, KV_W), lambda bi, qi: (bi, 0, 0))] * 2
        args += list(ctx_kv)
    in_specs += [_stacked_spec(qg), _stacked_spec(kg)]
    args += [qg[0], kg[0]]
    out_specs = [pl.BlockSpec((1, tq, ATTN_W), lambda bi, qi: (bi, qi, 0))]
    out_shape = [jax.ShapeDtypeStruct((b, t, ATTN_W), BF16)]
    if latent:
        cos_t, sin_t = rope_tabs
        in_specs += [pl.BlockSpec((tq, LANES), lambda bi, qi: (qi, 0))] * 2
        in_specs += [pl.BlockSpec((t, LANES), lambda bi, qi: (0, 0))] * 2
        args += [cos_t, sin_t, cos_t, sin_t]
    else:
        out_specs.append(pl.BlockSpec((1, t, KV_W), lambda bi, qi: (bi, 0, 0)))
        out_shape.append(jax.ShapeDtypeStruct((b, t, KV_W), F32))
    res = pl.pallas_call(
        kern,
        grid=(b, t // tq),
        in_specs=in_specs,
        out_specs=out_specs,
        out_shape=out_shape,
        scratch_shapes=[pltpu.VMEM((s_all, LANES), BF16), pltpu.VMEM((LANES, s_all), BF16)],
        compiler_params=_cparams(("arbitrary", "arbitrary")),
        name="attention_lat" if latent else "attention_ctx",
    )(*args)
    return res if not latent else (res[0], None)


def _wkv_kernel(*refs, t_seq, latent, nb):
    (r_ref, k_ref, v_ref, lora_ref, cwr_ref, cwk_ref, cwv_ref, w0_ref, a0_ref, w2_ref, a2_ref, g2_ref,
     kk_ref, ka_ref, rk_ref, lng_ref, lnb_ref) = refs[:17]
    rest = refs[17:]
    if latent:
        st0_ref, o_ref = rest[:2]
        scr = rest[2:]
        sf_ref = None
    else:
        o_ref, sf_ref = rest[:2]
        scr = rest[2:]
        st0_ref = None
    tar_s, tv_s, ov_s, arb_s, w_s, n_s, gc_s, st_s, bon_s = scr
    c_len = WKV_C
    c2 = 2 * c_len
    nc = t_seq // c_len

    lane = lax.broadcasted_iota(jnp.int32, (1, LANES), 1)
    lo = lane < R_HEAD
    sum_bd = _head_block_diag(1.0)
    mean_bd = _head_block_diag(1.0 / R_HEAD)
    row_c = lax.broadcasted_iota(jnp.int32, (c_len, LANES), 0)
    ri = lax.broadcasted_iota(jnp.int32, (c2, c2), 0)
    ci = lax.broadcasted_iota(jnp.int32, (c2, c2), 1)
    same_head = (ri // c_len) == (ci // c_len)
    tt, ss = ri % c_len, ci % c_len
    eye2 = jnp.where(ri == ci, 1.0, 0.0).astype(F32)
    strict = (same_head & (ss < tt), same_head & (ss > tt))
    incl = (same_head & (ss <= tt), same_head & (ss >= tt))
    tj = lax.broadcasted_iota(jnp.int32, (c_len, c2), 0)
    sj = lax.broadcasted_iota(jnp.int32, (c_len, c2), 1)
    cum_both = jnp.where(((sj < c_len) & (sj <= tj)) | ((sj >= c_len) & (sj - c_len >= tj)), 1.0, 0.0).astype(BF16)
    sum_bd_b = sum_bd.astype(BF16)
    mean_bd_b = mean_bd.astype(BF16)

    def bdot(a, b):
        return _dot(a.astype(BF16), b.astype(BF16))

    def split2(x):
        hi = x.astype(BF16)
        return hi, (x - hi.astype(F32)).astype(BF16)

    def head_reduce(x, bd_b):
        return _dot(x.astype(BF16), bd_b)

    def stack_heads(x):
        return jnp.concatenate([jnp.where(lo, x, 0.0), jnp.where(lo, 0.0, x)], axis=0)

    def fold_heads(x):
        return x[0:c_len] + x[c_len:c2]

    def slot(bi, c):
        return bi * nc + c

    def conv_tokens(ref, w_ref, bi, c, t0):
        x = ref[bi, pl.ds(t0, c_len), :]
        if isinstance(c, int):
            zero_row = jnp.zeros((1, LANES), F32)
            prev = ref[bi, pl.ds(t0 - 1, 1), :] if c > 0 else zero_row
            nxt = ref[bi, pl.ds(t0 + c_len, 1), :] if c < nc - 1 else zero_row
        else:
            prev = ref[bi, pl.ds(jnp.maximum(t0 - 1, 0), 1), :]
            nxt = ref[bi, pl.ds(jnp.minimum(t0 + c_len, t_seq - 1), 1), :]
            prev = jnp.where(c > 0, prev, 0.0)
            nxt = jnp.where(c < nc - 1, nxt, 0.0)
        xm = jnp.where(row_c == 0, prev, pltpu.roll(x, 1, 0))
        xp = jnp.where(row_c == c_len - 1, nxt, pltpu.roll(x, c_len - 1, 0))
        return xm * w_ref[0:1, :] + x * w_ref[1:2, :] + xp * w_ref[2:3, :]

    def chunk_group(chunks):
        n = len(chunks)
        t0s = [c * c_len if isinstance(c, int) else pl.multiple_of(c * c_len, c_len) for _, c in chunks]
        rc = [conv_tokens(r_ref, cwr_ref, bi, c, t0) for (bi, c), t0 in zip(chunks, t0s)]
        kc = [conv_tokens(k_ref, cwk_ref, bi, c, t0) for (bi, c), t0 in zip(chunks, t0s)]
        vc = [conv_tokens(v_ref, cwv_ref, bi, c, t0) for (bi, c), t0 in zip(chunks, t0s)]
        kkr = [k * kk_ref[...] for k in kc]
        kk = [x * lax.rsqrt(head_reduce(x * x, sum_bd_b) + 1e-12) for x in kkr]
        tw = [jnp.tanh(lora_ref[bi, pl.ds(t0, c_len), 0:2 * LORA_W]) for (bi, _), t0 in zip(chunks, t0s)]
        xa = [lora_ref[bi, pl.ds(t0, c_len), 2 * LORA_W:2 * LORA_W + 2 * LORA_A] for (bi, _), t0 in zip(chunks, t0s)]
        v_hs = [stack_heads(v).astype(BF16) for v in vc]
        prob =[(i, d) for i in range(n) for d in range(2)]
        pre_w = [_dot(x.astype(BF16), w2_ref[...]) for x in tw]
        pre_a = [_dot(x.astype(BF16), a2_ref[...]) for x in xa]
        lw = [-DECAY_SCALE * _sigmoid(w0_ref[d:d + 1, :] + pre_w[i][:, LANES * d:LANES * (d + 1)]) for i, d in prob]
        ag = [_sigmoid(a0_ref[d:d + 1, :] + pre_a[i][:, LANES * d:LANES * (d + 1)]) for i, d in prob]
        kd = [kc[i] * (1.0 + (ag[p] - 1.0) * ka_ref[...]) for p, (i, d) in enumerate(prob)]
        kb = [kk[i] * ag[p] for p, (i, d) in enumerate(prob)]
        for i in range(n):
            bon = head_reduce(rc[i] * (kd[2 * i] + kd[2 * i + 1]) * rk_ref[...], sum_bd_b) * vc[i]
            bon_s[chunks[i][0], pl.ds(t0s[i], c_len), :] = bon
        lw_sp = [split2(x) for x in lw]
        zc = jnp.zeros((c_len, LANES), BF16)
        cum_res = [_dot(cum_both, jnp.concatenate(
            [jnp.concatenate([lw_sp[2 * i][0], lw_sp[2 * i][1], zc, zc], axis=1),
             jnp.concatenate([zc, zc, lw_sp[2 * i + 1][0], lw_sp[2 * i + 1][1]], axis=1)], axis=0)) for i in range(n)]
        cs_in = [cum_res[i][:, 2 * LANES * d:2 * LANES * d + LANES] + cum_res[i][:, 2 * LANES * d + LANES:2 * LANES * (d + 1)]
                 for i, d in prob]
        tot = [cs_in[p][c_len - 1:c_len, :] if d == 0 else cs_in[p][0:1, :] for p, (i, d) in enumerate(prob)]
        e_neg = [jnp.exp(-x) for x in cs_in]
        e_rem = [jnp.exp(tot[p] - cs_in[p]) for p in range(len(prob))]
        a_t = [-kk[i] * jnp.exp(cs_in[p] - (lw_sp[p][0].astype(F32) + lw_sp[p][1].astype(F32)))
               for p, (i, d) in enumerate(prob)]
        r_t = [rc[i] * jnp.exp(cs_in[p]) for p, (i, d) in enumerate(prob)]
        a_hs = [stack_heads(x).astype(BF16) for x in a_t]
        lhs = [jnp.concatenate([a_hs[p], stack_heads(r_t[p]).astype(BF16)], axis=0) for p in range(len(prob))]
        rhs = [jnp.concatenate([stack_heads(kd[p] * e_neg[p]), stack_heads(kb[p] * e_neg[p])], axis=0).astype(BF16)
               for p in range(len(prob))]
        m4 = [_dot_nt(a, b) for a, b in zip(lhs, rhs)]
        a_ak = [jnp.where(strict[d], m4[p][0:c2, 0:c2], 0.0).astype(BF16) for p, (i, d) in enumerate(prob)]
        l_ab = [jnp.where(strict[d], m4[p][0:c2, c2:2 * c2], 0.0) for p, (i, d) in enumerate(prob)]
        a_rk = [jnp.where(incl[d], m4[p][c2:2 * c2, 0:c2], 0.0).astype(BF16) for p, (i, d) in enumerate(prob)]
        for p, (i, d) in enumerate(prob):
            arb_s[d, slot(*chunks[i])] = jnp.where(incl[d], m4[p][c2:2 * c2, c2:2 * c2], 0.0).astype(BF16)
        t_inv = [eye2 + x for x in l_ab]
        pw_b = [x.astype(BF16) for x in l_ab]
        pw_b = [_dot(x, x).astype(BF16) for x in pw_b]
        for _ in range(4):
            res = [_dot(x, jnp.concatenate([x, t.astype(BF16)], axis=1)) for x, t in zip(pw_b, t_inv)]
            pw_b = [r[:, 0:c2].astype(BF16) for r in res]
            t_inv = [t + r[:, c2:2 * c2] for t, r in zip(t_inv, res)]
        t_inv = [t + _dot(x, t.astype(BF16)) for x, t in zip(pw_b, t_inv)]
        t_b = [t.astype(BF16) for t in t_inv]
        av = [_dot(a_ak[p], v_hs[i]).astype(BF16) for p, (i, d) in enumerate(prob)]
        ov = [fold_heads(_dot(a_rk[p], v_hs[i])) for p, (i, d) in enumerate(prob)]
        tt = [_dot(t, jnp.concatenate([a, x], axis=1)) for t, a, x in zip(t_b, a_hs, av)]
        ta = [fold_heads(r[:, 0:LANES]) for r in tt]
        tv = [fold_heads(r[:, LANES:2 * LANES]) for r in tt]
        kbt = [jnp.concatenate([kd[p] * e_rem[p], kb[p] * e_rem[p]], axis=0).T.astype(BF16) for p in range(len(prob))]
        zero_c = jnp.zeros((c_len, LANES), BF16)
        upd_rhs = [jnp.concatenate([jnp.concatenate([vc[i].astype(BF16), zero_c], axis=1),
                                    jnp.concatenate([tv[p].astype(BF16), ta[p].astype(BF16)], axis=1)], axis=0)
                   for p, (i, d) in enumerate(prob)]
        nw = [_dot(a, b) for a, b in zip(kbt, upd_rhs)]
        for p, (i, d) in enumerate(prob):
            c = slot(*chunks[i])
            tar_s[d, c] = jnp.concatenate([ta[p], r_t[p]], axis=0).astype(BF16)
            tv_s[d, c] = tv[p]
            ov_s[d, c] = ov[p]
            n_s[d, c] = jnp.where(bd_mask, nw[p][:, 0:LANES], 0.0)
            w_s[d, c] = jnp.where(bd_mask, nw[p][:, LANES:2 * LANES], 0.0).astype(BF16)
            gc_s[d, c] = jnp.broadcast_to(jnp.exp(tot[p]), (LANES, LANES)).T

    bd_mask = sum_bd > 0.5
    group1 = 8
    if nb > 1 or nc <= group1:
        every = [(bi, c) for bi in range(nb) for c in range(nc)]
        for g0 in range(0, len(every), group1):
            chunk_group(every[g0:g0 + group1])
    else:
        assert nb == 1 and nc % group1 == 0

        def phase1(i, carry):
            chunk_group([(0, i * group1 + j) for j in range(group1)])
            return carry

        lax.fori_loop(0, nc // group1, phase1, 0)

    chains = [(bi, d) for bi in range(nb) for d in range(2)]

    def phase2(i, carry):
        cs = [slot(bi, i if d == 0 else nc - 1 - i) for bi, d in chains]
        st_b = [x.astype(BF16) for x in carry]
        for (bi, d), c, x in zip(chains, cs, st_b):
            st_s[d, c] = x
        prod = [_dot(w_s[d, c], x) for (bi, d), c, x in zip(chains, cs, st_b)]
        return tuple(gc_s[d, c] * s + pr + n_s[d, c] for (bi, d), c, s, pr in zip(chains, cs, carry, prod))

    if latent:
        init = tuple(st0_ref[bi, d, 0] for bi, d in chains)
    else:
        init = (jnp.zeros((LANES, LANES), F32),) * len(chains)
    fin = lax.fori_loop(0, nc, phase2, init)
    if sf_ref is not None:
        for (bi, d), st in zip(chains, fin):
            s_vk = st.T
            sf_ref[bi, d, 0] = s_vk[0:R_HEAD, 0:R_HEAD]
            sf_ref[bi, d, 1] = pltpu.roll(s_vk[R_HEAD:2 * R_HEAD, :], R_HEAD, 1)[:, 0:R_HEAD]

    def out_group(chunks):
        prob = [(bi, c, slot(bi, c), d) for bi, c in chunks for d in range(2)]
        both = [_dot(tar_s[d, s], st_s[d, s]) for _, _, s, d in prob]
        u = [both[p][0:c_len] + tv_s[d, s] for p, (_, _, s, d) in enumerate(prob)]
        ou = [_dot(arb_s[d, s], stack_heads(u[p]).astype(BF16)) for p, (_, _, s, d) in enumerate(prob)]
        o = [ov_s[d, s] + both[p][c_len:c2] + fold_heads(ou[p]) for p, (_, _, s, d) in enumerate(prob)]
        of = [o[2 * j] + o[2 * j + 1] for j in range(len(chunks))]
        xc = [x - head_reduce(x, mean_bd_b) for x in of]
        var = [head_reduce(x * x, mean_bd_b) for x in xc]
        for j, (bi, c) in enumerate(chunks):
            t0 = c * c_len if isinstance(c, int) else pl.multiple_of(c * c_len, c_len)
            on = xc[j] * lax.rsqrt(var[j] + GN_EPS) * lng_ref[...] + lnb_ref[...]
            xg = lora_ref[bi, pl.ds(t0, c_len), 2 * LORA_W + 2 * LORA_A:LORA_ALL]
            g_out = _dot(_sigmoid(xg).astype(BF16), g2_ref[...])
            o_ref[bi, pl.ds(t0, c_len), :] = ((on + bon_s[bi, pl.ds(t0, c_len), :]) * g_out).astype(o_ref.dtype)

    group3 = 8
    if nb > 1 or nc <= group3:
        every = [(bi, c) for bi in range(nb) for c in range(nc)]
        for g0 in range(0, len(every), group3):
            out_group(every[g0:g0 + group3])
    else:
        assert nb == 1 and nc % group3 == 0

        def phase3(i, carry):
            out_group([(0, i * group3 + j) for j in range(group3)])
            return carry

        lax.fori_loop(0, nc // group3, phase3, 0)


def _wkv(rkv, lora, wts, st0):
    b, t, _ = rkv.shape
    latent = st0 is not None
    nc = t // WKV_C
    npair = RW // LANES
    nb = 4 if (4 * nc <= 16 and b % 4 == 0) else 1
    kern = functools.partial(_wkv_kernel, t_seq=t, latent=latent, nb=nb)

    def seq_spec(col0):
        return pl.BlockSpec((nb, t, LANES), lambda bi, p: (bi, 0, col0 + p))

    def row_spec(name, rows, col0=0, width=LANES):
        return _stacked_spec(wts[name], (rows, width), lambda bi, p: (0, col0 + p))

    in_specs = [
        seq_spec(0), seq_spec(npair), seq_spec(2 * npair),
        pl.BlockSpec((nb, t, LORA_ALL), lambda bi, p: (bi, 0, 0)),
        row_spec("conv", 3, 0), row_spec("conv", 3, npair), row_spec("conv", 3, 2 * npair),
        row_spec("w0", 2), row_spec("a0", 2),
        row_spec("w2", LANES, width=2 * LANES), row_spec("a2", LANES, width=2 * LANES),
        row_spec("g2", LANES),
        row_spec("k_k", 1), row_spec("k_a", 1), row_spec("r_k", 1), row_spec("ln_g", 1), row_spec("ln_b", 1),
    ]
    args = [rkv, rkv, rkv, lora] + [wts[n][0] for n in ("conv", "conv", "conv", "w0", "a0", "w2", "a2", "g2",
                                                        "k_k", "k_a", "r_k", "ln_g", "ln_b")]
    out_specs = [seq_spec(0)]
    out_shape = [jax.ShapeDtypeStruct((b, t, RW), BF16)]
    if latent:
        in_specs.append(pl.BlockSpec((nb, 2, 1, LANES, LANES), lambda bi, p: (bi, 0, p, 0, 0)))
        args.append(st0)
    else:
        out_specs.append(pl.BlockSpec((nb, 2, 2, R_HEAD, R_HEAD), lambda bi, p: (bi, 0, p, 0, 0)))
        out_shape.append(jax.ShapeDtypeStruct((b, 2, R_HEADS, R_HEAD, R_HEAD), F32))
    chunk_rows = pltpu.VMEM((2, nb * nc, WKV_C, LANES), F32)
    chunk_sq = pltpu.VMEM((2, nb * nc, LANES, LANES), F32)
    chunk_sq_b = pltpu.VMEM((2, nb * nc, LANES, LANES), BF16)
    res = pl.pallas_call(
        kern,
        grid=(b // nb, npair),
        in_specs=in_specs,
        out_specs=out_specs,
        out_shape=out_shape,
        scratch_shapes=[chunk_sq_b, chunk_rows, chunk_rows, chunk_sq_b, chunk_sq_b, chunk_sq, chunk_sq, chunk_sq_b,
                        pltpu.VMEM((nb, t, LANES), F32)],
        compiler_params=_cparams(("arbitrary", "arbitrary")),
        name="wkv_lat" if latent else "wkv_ctx",
    )(*args)
    return (res[0], None) if latent else (res[0], res[1])


def _merge_kernel(x_ref, oa_ref, su_ref, sv_ref, or_ref, gates_ref, gt_ref, sg_ref, ws_ref, bs_ref,
                  pa_ref, ps_ref, pr_ref, wo_ref, o_ref, *, row0, rstride, bps):
    row = row0 + (pl.program_id(0) // bps) * rstride if rstride else row0
    d = D_MODEL
    sgu_rows = []
    for c in range(x_ref.shape[0] // CHUNK):
        rows = slice(CHUNK * c, CHUNK * (c + 1))
        v = sv_ref[rows, :]
        vn = (v * lax.rsqrt(jnp.mean(v * v, axis=-1, keepdims=True) + EPS) * sg_ref[...]).astype(BF16)
        sgu_rows.append(jnp.concatenate(
            [(su_ref[rows, LANES * g:LANES * (g + 1)]
              * (_dot(ws_ref[g], vn[:, LANES * g:LANES * (g + 1)]) + bs_ref[g])).astype(BF16)
             for g in range(SGU_GROUPS)], axis=1))
    o_sgu = jnp.concatenate(sgu_rows, axis=0)
    gate = lambda j: _sigmoid(gates_ref[:, j * d:(j + 1) * d].astype(F32))
    merged = (gate(0) * _dot(oa_ref[...].astype(BF16), pa_ref[...])
              + gate(1) * _dot(o_sgu, ps_ref[...])
              + gate(2) * _dot(or_ref[...].astype(BF16), pr_ref[...]))
    o_ref[...] = x_ref[...] + _mod_row(gt_ref, row) * _dot(merged.astype(BF16), wo_ref[...])


def _merge(x2, oa, su, sv, orw, gates, mod_l, sgu_g, sgu_ws, sgu_bs, pa, ps, pr, wo, row0, rstride, t_seq):
    m, d = x2.shape
    tm = min(TOKEN_TILE, t_seq if rstride else m)
    assert tm % CHUNK == 0 and t_seq % CHUNK == 0
    kern = functools.partial(_merge_kernel, row0=row0, rstride=rstride, bps=max(t_seq // tm, 1))
    tok = lambda w: pl.BlockSpec((tm, w), lambda i: (i, 0))
    weights = (sgu_g, sgu_ws, sgu_bs, pa, ps, pr, wo)
    return pl.pallas_call(
        kern,
        grid=(m // tm,),
        in_specs=[tok(d), tok(ATTN_W), tok(SGU_W), tok(SGU_W), tok(RW), tok(3 * d),
                  _stacked_spec(mod_l, (8, d), lambda i: (0, 2))]
                 + [_stacked_spec(a) for a in weights],
        out_specs=tok(d),
        out_shape=jax.ShapeDtypeStruct((m, d), F32),
        compiler_params=_cparams(("arbitrary",)),
        name="merge",
    )(x2, oa, su, sv, orw, gates, mod_l[0], *[a[0] for a in weights])


def _moe_kernel(x_ref, sh_ref, sc_ref, gt_ref, g2_ref, rw_ref, rb_ref, shg_ref, shu_ref, shd_ref,
                wg_ref, wu_ref, wd_ref, o_ref, h_s, comb_s, acc_s, *, row0, rstride, bps):
    row = row0 + (pl.program_id(0) // bps) * rstride if rstride else row0
    e = pl.program_id(1)
    tm = x_ref.shape[0]

    @pl.when(e == 0)
    def _():
        x = x_ref[...]
        h = x * lax.rsqrt(jnp.mean(x * x, axis=-1, keepdims=True) + EPS) * g2_ref[...]
        h = h * (1.0 + _mod_row(sc_ref, row)) + _mod_row(sh_ref, row)
        hb = h.astype(BF16)
        h_lo = (h - hb.astype(F32)).astype(BF16)
        hw = _dot(hb, rw_ref[...])
        logits = hw[:, 0:LANES] + hw[:, LANES:2 * LANES] + _dot(h_lo, rw_ref[:, 0:LANES])
        scores = _sigmoid(logits)
        sc_t = scores.T[0:N_EXPERTS]
        sel = (scores + rb_ref[...]).T[0:N_EXPERTS]
        eid = lax.broadcasted_iota(jnp.int32, (N_EXPERTS, tm), 0).astype(F32)
        picked = jnp.zeros((N_EXPERTS, tm), jnp.bool_)
        for _ in range(TOP_K):
            best = jnp.max(sel, axis=0, keepdims=True)
            first = jnp.min(jnp.where(sel == best, eid, float(N_EXPERTS)), axis=0, keepdims=True)
            hit = eid == first
            picked = jnp.logical_or(picked, hit)
            sel = jnp.where(hit, -jnp.inf, sel)
        sw = jnp.where(picked, sc_t, 0.0)
        comb_t = sw / jnp.sum(sw, axis=0, keepdims=True) * ROUTED_SCALE
        comb = jnp.concatenate([comb_t, jnp.zeros((LANES - N_EXPERTS, tm), F32)], axis=0).T
        for b in range(N_EXPERTS // MOE_EB):
            comb_s[b] = comb if b == 0 else pltpu.roll(comb, LANES - MOE_EB * b, 1)
        h_s[...] = hb
        acc_s[...] = _dot((_silu(_dot(hb, shg_ref[...])) * _dot(hb, shu_ref[...])).astype(BF16), shd_ref[...])

    hb = h_s[...]
    comb = comb_s[e]
    hid = []
    for j in range(MOE_EB):
        gu = _dot(hb, jnp.concatenate([wg_ref[j], wu_ref[j]], axis=1))
        cw = jnp.broadcast_to(comb[:, j:j + 1], (tm, D_EXPERT))
        hid.append((_silu(gu[:, 0:D_EXPERT]) * gu[:, D_EXPERT:2 * D_EXPERT] * cw).astype(BF16))
    acc_s[...] += _dot(jnp.concatenate(hid, axis=1), wd_ref[...])

    @pl.when(e == pl.num_programs(1) - 1)
    def _():
        o_ref[...] = x_ref[...] + _mod_row(gt_ref, row) * acc_s[...]


def _moe(x2, mod_l, g2, rw_hl, rb_p, shg, shu, shd, wg_l, wu_l, wd_l, row0, rstride, t_seq):
    m, d = x2.shape
    (wg, layer), (wu, _), (wd, _) = wg_l, wu_l, wd_l
    tm = min(1024, t_seq if rstride else m)
    n_hid = MOE_EB * D_EXPERT
    kern = functools.partial(_moe_kernel, row0=row0, rstride=rstride, bps=max(t_seq // tm, 1))
    small = (g2, rw_hl, rb_p, shg, shu, shd)
    return pl.pallas_call(
        kern,
        grid=(m // tm, N_EXPERTS // MOE_EB),
        in_specs=[
            pl.BlockSpec((tm, d), lambda i, e: (i, 0)),
            _stacked_spec(mod_l, (8, d), lambda i, e: (0, 3)),
            _stacked_spec(mod_l, (8, d), lambda i, e: (0, 4)),
            _stacked_spec(mod_l, (8, d), lambda i, e: (0, 5)),
            *[_stacked_spec(a) for a in small],
            pl.BlockSpec((None, MOE_EB, d, D_EXPERT), lambda i, e: (layer, e, 0, 0)),
            pl.BlockSpec((None, MOE_EB, d, D_EXPERT), lambda i, e: (layer, e, 0, 0)),
            pl.BlockSpec((None, n_hid, d), lambda i, e: (layer, e, 0)),
        ],
        out_specs=pl.BlockSpec((tm, d), lambda i, e: (i, 0)),
        out_shape=jax.ShapeDtypeStruct((m, d), F32),
        scratch_shapes=[pltpu.VMEM((tm, d), BF16), pltpu.VMEM((N_EXPERTS // MOE_EB, tm, LANES), F32),
                        pltpu.VMEM((tm, d), F32)],
        compiler_params=_cparams(("arbitrary", "arbitrary")),
        name="moe",
    )(x2, mod_l[0], mod_l[0], mod_l[0], *[a[0] for a in small], wg, wu, wd)


def _layer(x, mod_l, w, row0, rstride, ctx, rope_tabs):
    b, t, d = x.shape
    x2 = x.reshape(b * t, d)
    q, kv, su, sv, rkv, lora, gates = _inproj(x2, mod_l, w["norm1_g"], w["w_in"], row0, rstride, t)
    if ctx is None:
        o_attn, k_new = _attention(q.reshape(b, t, -1), kv.reshape(b, t, -1), w["q_norm"], w["k_norm"], None, None)
        st0 = None
    else:
        o_attn, k_new = _attention(q.reshape(b, t, -1), kv.reshape(b, t, -1), w["q_norm"], w["k_norm"],
                                   ctx[:2], rope_tabs)
        st0 = ctx[2]
    o_wkv, s_fin = _wkv(rkv.reshape(b, t, -1), lora.reshape(b, t, -1), w["wkv"], st0)
    x1 = _merge(x2, o_attn.reshape(b * t, -1), su, sv, o_wkv.reshape(b * t, -1), gates, mod_l,
                w["sgu_norm_g"], w["sgu_ws"], w["sgu_bs"],
                w["proj_attn"], w["proj_sgu"], w["proj_rwkv"], w["w_out"], row0, rstride, t)
    x_out = _moe(x1, mod_l, w["norm2_g"], w["router_w_hl"], w["router_bias"],
                 w["sh_gate"], w["sh_up"], w["sh_down"],
                 w["exp_gate"], w["exp_up"], w["exp_down"], row0, rstride, t)
    return x_out.reshape(b, t, d), (k_new, kv, s_fin)


def _rope_tables(n_tok):
    t = jnp.arange(n_tok)
    row = (t // GRID_W).astype(F32)
    col = (t % GRID_W).astype(F32)
    inv = ROPE_THETA ** (-jnp.arange(N_FREQ, dtype=F32) / N_FREQ)
    ang = jnp.stack([row[:, None] * inv[None, :], col[:, None] * inv[None, :]], axis=1)
    cos, sin = jnp.cos(ang), jnp.sin(ang)
    cos64 = jnp.stack([cos, cos], axis=2).reshape(n_tok, HEAD_DIM)
    sin64 = jnp.stack([-sin, sin], axis=2).reshape(n_tok, HEAD_DIM)
    return jnp.tile(cos64, (1, LANES // HEAD_DIM)), jnp.tile(sin64, (1, LANES // HEAD_DIM))


def _lora_pair_layout(w):
    npair = RW // LANES
    n_l, _, k, _ = w.shape
    w4 = w.reshape(n_l, 2, k, npair, LANES)
    bd = jnp.einsum("ldkpc,de->ldkpec", w4, jnp.eye(2, dtype=w.dtype))
    return bd.reshape(n_l, 2 * k, npair * 2 * LANES).astype(BF16)


def _layer_weights(l, mod_w, mod_b, norm1_g, norm2_g, w_in, q_norm, k_norm, sgu_norm_g, sgu_ws, sgu_bs, rwkv_conv,
                   rwkv_w0, rwkv_w2, rwkv_a0, rwkv_a2, rwkv_g2, rwkv_k_k, rwkv_k_a, rwkv_r_k, rwkv_ln_g, rwkv_ln_b,
                   proj_attn, proj_sgu, proj_rwkv, w_out, router_w, router_bias, exp_gate, exp_up, exp_down,
                   sh_gate, sh_up, sh_down):
    d = D_MODEL
    n_l = w_in.shape[0]
    row = lambda a: a.reshape(n_l, 1, -1)
    pad_e = LANES - N_EXPERTS
    rw_pad = jnp.pad(router_w, ((0, 0), (0, 0), (0, pad_e)))
    stacked = {
        "norm1_g": row(norm1_g), "norm2_g": row(norm2_g),
        "w_in": w_in.astype(BF16),
        "q_norm": jnp.tile(row(q_norm), (1, 1, LANES // HEAD_DIM)),
        "k_norm": jnp.tile(row(k_norm), (1, 1, LANES // HEAD_DIM)),
        "sgu_norm_g": row(sgu_norm_g),
        "sgu_ws": sgu_ws.astype(BF16),
        "sgu_bs": jnp.broadcast_to(sgu_bs[:, :, :, None], (n_l, SGU_GROUPS, CHUNK, LANES)),
        "wkv": {
            "conv": rwkv_conv,
            "w0": rwkv_w0, "a0": rwkv_a0,
            "w2": _lora_pair_layout(rwkv_w2), "a2": _lora_pair_layout(rwkv_a2),
            "g2": rwkv_g2.astype(BF16),
            "k_k": row(rwkv_k_k), "k_a": row(rwkv_k_a), "r_k": row(rwkv_r_k),
            "ln_g": row(rwkv_ln_g), "ln_b": row(rwkv_ln_b),
        },
        "proj_attn": proj_attn.astype(BF16), "proj_sgu": proj_sgu.astype(BF16),
        "proj_rwkv": proj_rwkv.astype(BF16), "w_out": w_out.astype(BF16),
        "router_w_hl": jnp.concatenate([rw_pad.astype(BF16),
                                        (rw_pad - rw_pad.astype(BF16).astype(F32)).astype(BF16)], axis=-1),
        "router_bias": jnp.pad(row(router_bias), ((0, 0), (0, 0), (0, pad_e))),
        "sh_gate": sh_gate.astype(BF16), "sh_up": sh_up.astype(BF16), "sh_down": sh_down.astype(BF16),
        "exp_gate": exp_gate.astype(BF16), "exp_up": exp_up.astype(BF16),
        "exp_down": exp_down.astype(BF16).reshape(n_l, N_EXPERTS * D_EXPERT, d),
    }
    return jax.tree.map(lambda a: (a, l), stacked)


def _state_to_blockdiag_t(s):
    b = s.shape[0]
    st = jnp.swapaxes(s, -1, -2).reshape(b, 2, R_HEADS // 2, 2, R_HEAD, R_HEAD)
    bd = jnp.einsum("bdpjkv,ji->bdpjkiv", st, jnp.eye(2, dtype=s.dtype))
    return bd.reshape(b, 2, R_HEADS // 2, LANES, LANES)


def kernel(x_prompt, x_sample, cache_k, cache_v, state_wkv, c, c_ctx, mod_w, mod_b, norm1_g, norm2_g, w_in, q_norm, k_norm, sgu_norm_g, sgu_ws, sgu_bs, rwkv_conv, rwkv_w0, rwkv_w2, rwkv_a0, rwkv_a2, rwkv_g2, rwkv_k_k, rwkv_k_a, rwkv_r_k, rwkv_ln_g, rwkv_ln_b, proj_attn, proj_sgu, proj_rwkv, w_out, router_w, router_bias, exp_gate, exp_up, exp_down, sh_gate, sh_up, sh_down):
    params = (mod_w, mod_b, norm1_g, norm2_g, w_in, q_norm, k_norm, sgu_norm_g, sgu_ws, sgu_bs, rwkv_conv,
              rwkv_w0, rwkv_w2, rwkv_a0, rwkv_a2, rwkv_g2, rwkv_k_k, rwkv_k_a, rwkv_r_k, rwkv_ln_g, rwkv_ln_b,
              proj_attn, proj_sgu, proj_rwkv, w_out, router_w, router_bias, exp_gate, exp_up, exp_down,
              sh_gate, sh_up, sh_down)
    n_b, n_t, d = x_prompt.shape
    n_db, n_dt, _ = x_sample.shape
    assert n_db + 1 <= 8
    cmat = jnp.zeros((8, d), F32).at[0].set(c_ctx).at[1:1 + n_db].set(c)
    mod = _modulation(cmat, mod_w, mod_b)
    rope_tabs = _rope_tables(n_dt)
    xp, xs = x_prompt, x_sample
    ks, vs, ss = [], [], []
    for l in range(DEPTH):
        w = _layer_weights(l, *params)
        xp, (k_new, kv, s_fin) = _layer(xp, (mod, l), w, 0, 0, None, None)
        ks.append(k_new.reshape(n_b, n_t, N_KV_HEADS, HEAD_DIM))
        vs.append(kv.reshape(n_b, n_t, 2 * KV_W)[:, :, KV_W:].reshape(n_b, n_t, N_KV_HEADS, HEAD_DIM))
        ss.append(s_fin)
        ctx = (cache_k[:, l].reshape(n_db, -1, KV_W), cache_v[:, l].reshape(n_db, -1, KV_W),
               _state_to_blockdiag_t(state_wkv[:, l]))
        xs, _ = _layer(xs, (mod, l), w, 1, 1, ctx, rope_tabs)
    return (xp, xs, jnp.stack(ks, axis=1), jnp.stack(vs, axis=1), jnp.stack(ss, axis=1))
```
